```python
import jax, jax.numpy as jnp
from jax import lax
import numpy as np

D_MODEL = 4096
BATCH = 2
SEQ = 4096
DEPTH = 2

GRID_W = 64
CTX_LEN = 256
MIX_WIDTH = D_MODEL
N_GROUPS = 4
GROUP_WIDTH = MIX_WIDTH // N_GROUPS
HEAD_DIM = 128
ATT_HEADS = GROUP_WIDTH // HEAD_DIM
ATT_KV_HEADS = ATT_HEADS // 4
HGRN_HEADS = GROUP_WIDTH // HEAD_DIM
HGRN_KEY_DIM = 128
HGRN_VAL_DIM = GROUP_WIDTH // HGRN_HEADS
HGRN_KEY_WIDTH = HGRN_HEADS * HGRN_KEY_DIM
HGRN_CHUNK = 64
CONV_WIDTH = 31
MLA_HEADS = GROUP_WIDTH // HEAD_DIM
MLA_Q_RANK = 768
MLA_KV_RANK = 512
MLA_NOPE_DIM = 128
MLA_ROPE_DIM = 64
MLA_V_DIM = GROUP_WIDTH // MLA_HEADS
MLA_QK_DIM = MLA_NOPE_DIM + MLA_ROPE_DIM

Q_BLOCK = 128
ROPE_THETA = 10000.0
EPS = 1e-6

IN_SPLITS = (
    ATT_HEADS * HEAD_DIM, ATT_KV_HEADS * HEAD_DIM, ATT_KV_HEADS * HEAD_DIM, GROUP_WIDTH,
    HGRN_KEY_WIDTH, GROUP_WIDTH, HGRN_KEY_WIDTH, HGRN_KEY_WIDTH, GROUP_WIDTH,
    GROUP_WIDTH, GROUP_WIDTH, GROUP_WIDTH,
    MLA_Q_RANK, MLA_KV_RANK, MLA_ROPE_DIM, GROUP_WIDTH,
)
IN_WIDTH = sum(IN_SPLITS)

kernel_name = "hybrid_parallel_groups_gqa_hgrn2_conformer_mla"


def rms_norm(x, w):
    xf = x.astype(jnp.float32)
    y = xf * lax.rsqrt(jnp.mean(xf * xf, axis=-1, keepdims=True) + EPS)
    return (y * w.astype(jnp.float32)).astype(x.dtype)


def layer_norm(x, w, b):
    xf = x.astype(jnp.float32)
    mu = jnp.mean(xf, axis=-1, keepdims=True)
    var = jnp.mean(jnp.square(xf - mu), axis=-1, keepdims=True)
    y = (xf - mu) * lax.rsqrt(var + EPS)
    return (y * w.astype(jnp.float32) + b.astype(jnp.float32)).astype(x.dtype)


def heads(t, n):
    return t.reshape(t.shape[:-1] + (n, t.shape[-1] // n))


def split_cols(t):
    idx = [int(i) for i in np.cumsum(IN_SPLITS)[:-1]]
    return jnp.split(t, idx, axis=-1)


def axial_rope(x, rows, cols):
    half = x.shape[-1] // 2
    quarter = half // 2
    inv_freq = ROPE_THETA ** (-jnp.arange(quarter, dtype=jnp.float32) / quarter)

    def rotate(xa, pos):
        ang = pos.astype(jnp.float32)[:, None] * inv_freq
        cos = jnp.cos(ang)[None, :, None, :].astype(x.dtype)
        sin = jnp.sin(ang)[None, :, None, :].astype(x.dtype)
        x1, x2 = jnp.split(xa, 2, axis=-1)
        return jnp.concatenate([x1 * cos - x2 * sin, x2 * cos + x1 * sin], axis=-1)

    return jnp.concatenate([rotate(x[..., :half], rows), rotate(x[..., half:], cols)], axis=-1)


def block_attention(q, k, v, scale):
    b, lq, hq, dk = q.shape
    hkv, dv = k.shape[2], v.shape[-1]
    grp = hq // hkv
    nb = lq // Q_BLOCK
    qb = q.reshape(b, nb, Q_BLOCK, hkv, grp, dk).transpose(1, 0, 2, 3, 4, 5)

    def attend(qblk):
        s = jnp.einsum('bqhgd,bkhd->bhgqk', qblk, k).astype(jnp.float32) * scale
        p = jax.nn.softmax(s, axis=-1).astype(v.dtype)
        return jnp.einsum('bhgqk,bkhe->bqhge', p, v)

    o = lax.map(attend, qb)
    return o.transpose(1, 0, 2, 3, 4, 5).reshape(b, lq, hq * dv)


def gla_chunk_scan(q, k, v, log_f, s0):
    b, l, h, _ = q.shape
    dv = v.shape[-1]
    nc = l // HGRN_CHUNK

    def chunks(t):
        return t.reshape(b, nc, HGRN_CHUNK, h, t.shape[-1]).transpose(1, 0, 3, 2, 4)

    lower = jnp.tril(jnp.ones((HGRN_CHUNK, HGRN_CHUNK), dtype=bool))[:, :, None]

    def step(state, xs):
        qc, kc, vc, lf = xs
        g = jnp.cumsum(lf, axis=2)
        rel = jnp.exp(jnp.where(lower, g[:, :, :, None, :] - g[:, :, None, :, :], -jnp.inf))
        scores = jnp.einsum('bhtd,bhtsd,bhsd->bhts', qc, rel, kc)
        o = (jnp.einsum('bhts,bhse->bhte', scores, vc)
             + jnp.einsum('bhtd,bhde->bhte', qc * jnp.exp(g), state))
        g_last = g[:, :, -1:, :]
        state = (jnp.exp(g_last[:, :, 0, :])[..., None] * state
                 + jnp.einsum('bhsd,bhse->bhde', kc * jnp.exp(g_last - g), vc))
        return state, o

    s_fin, o = lax.scan(step, s0, (chunks(q), chunks(k), chunks(v), chunks(log_f)))
    o = o.transpose(1, 0, 3, 2, 4).reshape(b, l, h, dv)
    return o, s_fin


def hgrn2_gates(f, lb):
    log_f = jax.nn.log_sigmoid(f) + jnp.logaddexp(0.0, jnp.log(lb) - f)
    k = (1.0 - lb) * jax.nn.sigmoid(-f)
    return k, log_f


def gqa_branch(lat, ctx, p, rows, cols, ctx_out):
    def proj(q, k, v):
        q = rms_norm(heads(q, ATT_HEADS), p['att_q_norm'])
        k = rms_norm(heads(k, ATT_KV_HEADS), p['att_k_norm'])
        return q, k, heads(v, ATT_KV_HEADS)

    ql, kl, vl = proj(*lat[:3])
    qc, kc, vc = proj(*ctx[:3])
    ql = axial_rope(ql, rows, cols)
    kl = axial_rope(kl, rows, cols)
    scale = HEAD_DIM ** -0.5
    k_all = jnp.concatenate([kc, kl], axis=1)
    v_all = jnp.concatenate([vc, vl], axis=1)
    yl = block_attention(ql, k_all, v_all, scale) * jax.nn.silu(lat[3])
    yc = block_attention(qc, kc, vc, scale) * jax.nn.silu(ctx[3]) if ctx_out else None
    return yl, yc


def hgrn2_branch(lat, ctx, lb, o_norm, ctx_out):
    lb = lb.reshape(2, HGRN_HEADS, HGRN_KEY_DIM)
    o_norm = o_norm.reshape(HGRN_HEADS, HGRN_VAL_DIM)

    def prep(q, i, f_fw, f_bw):
        q = heads(jax.nn.silu(q.astype(jnp.float32)), HGRN_HEADS) * HGRN_KEY_DIM ** -0.5
        v = heads(i.astype(jnp.float32), HGRN_HEADS)
        fw = hgrn2_gates(heads(f_fw.astype(jnp.float32), HGRN_HEADS), lb[0])
        bw = hgrn2_gates(heads(f_bw.astype(jnp.float32), HGRN_HEADS), lb[1])
        return q, v, fw, bw

    ql, vl, fwl, bwl = prep(*lat[:4])
    qc, vc, fwc, bwc = prep(*ctx[:4])
    zero = jnp.zeros((ql.shape[0], HGRN_HEADS, HGRN_KEY_DIM, HGRN_VAL_DIM), jnp.float32)

    def rev(t):
        return jnp.flip(t, axis=1)

    oc_fw, sc_fw = gla_chunk_scan(qc, fwc[0], vc, fwc[1], zero)
    ol_fw, _ = gla_chunk_scan(ql, fwl[0], vl, fwl[1], sc_fw)
    oc_bw, sc_bw = gla_chunk_scan(rev(qc), rev(bwc[0]), rev(vc), rev(bwc[1]), zero)
    ol_bw, _ = gla_chunk_scan(rev(ql), rev(bwl[0]), rev(vl), rev(bwl[1]), sc_bw)

    def readout(o, gate):
        o = rms_norm(o, o_norm)
        return o.reshape(o.shape[:2] + (-1,)).astype(gate.dtype) * jax.nn.silu(gate)

    yl = readout(ol_fw + rev(ol_bw), lat[4])
    yc = readout(oc_fw + rev(oc_bw), ctx[4]) if ctx_out else None
    return yl, yc


def conv_branch(u, glu, gate, p):
    xg = u * jax.nn.sigmoid(glu)
    y = lax.conv_general_dilated(
        xg, p['conv_w'][:, None, :].astype(xg.dtype), window_strides=(1,), padding='SAME',
        dimension_numbers=('NWC', 'WIO', 'NWC'), feature_group_count=xg.shape[-1]) + p['conv_b']
    y = jax.nn.silu(layer_norm(y, p['conv_ln_w'], p['conv_ln_b']))
    return y * jax.nn.silu(gate)


def mla_branch(lat, ctx, p, rows, cols, ctx_out):
    def proj(cq, ckv, k_rope):
        q = heads(rms_norm(cq, p['mla_q_norm']) @ p['mla_w_uq'], MLA_HEADS)
        kv = heads(rms_norm(ckv, p['mla_kv_norm']) @ p['mla_w_ukv'], MLA_HEADS)
        k_nope, v = jnp.split(kv, [MLA_NOPE_DIM], axis=-1)
        k_rope = jnp.broadcast_to(k_rope[:, :, None, :], k_nope.shape[:-1] + (MLA_ROPE_DIM,))
        k = jnp.concatenate([k_nope, k_rope], axis=-1)
        return rms_norm(q, p['mla_qk_q_norm']), rms_norm(k, p['mla_qk_k_norm']), v

    def rope_tail(t):
        return jnp.concatenate([t[..., :MLA_NOPE_DIM], axial_rope(t[..., MLA_NOPE_DIM:], rows, cols)], axis=-1)

    ql, kl, vl = proj(*lat[:3])
    qc, kc, vc = proj(*ctx[:3])
    ql = rope_tail(ql)
    kl = rope_tail(kl)
    scale = MLA_QK_DIM ** -0.5
    k_all = jnp.concatenate([kc, kl], axis=1)
    v_all = jnp.concatenate([vc, vl], axis=1)
    yl = block_attention(ql, k_all, v_all, scale) * jax.nn.silu(lat[3])
    yc = block_attention(qc, kc, vc, scale) * jax.nn.silu(ctx[3]) if ctx_out else None
    return yl, yc


def hybrid_mixer(h, hc, p, lb, rows, cols, ctx_out):
    lat = split_cols(h @ p['w_in'])
    ctx = split_cols(hc @ p['w_in'])
    ya, yca = gqa_branch(lat[0:4], ctx[0:4], p, rows, cols, ctx_out)
    yb, ycb = hgrn2_branch(lat[4:9], ctx[4:9], lb, p['hgrn_o_norm'], ctx_out)
    yc = conv_branch(lat[9], lat[10], lat[11], p)
    yd, ycd = mla_branch(lat[12:16], ctx[12:16], p, rows, cols, ctx_out)
    out = jnp.concatenate([ya, yb, yc, yd], axis=-1) @ p['w_out']
    if not ctx_out:
        return out, None
    ycc = conv_branch(ctx[9], ctx[10], ctx[11], p)
    out_c = jnp.concatenate([yca, ycb, ycc, ycd], axis=-1) @ p['w_out']
    return out, out_c


def setup_inputs(seed: int = 0) -> dict:
    key = jax.random.key(seed)
    ks = jax.random.split(key, 24)
    f32 = jnp.float32

    def nrm(k, shape, s):
        return jax.random.normal(k, shape, f32) * s

    def gain(k, shape):
        return 1.0 + 0.01 * jax.random.normal(k, shape, f32)

    return {
        'x': nrm(ks[0], (BATCH, SEQ, D_MODEL), 1.0),
        'c': nrm(ks[1], (BATCH, D_MODEL), 1.0),
        'ctx': nrm(ks[2], (BATCH, CTX_LEN, D_MODEL), 1.0),
        'c_ctx': nrm(ks[3], (D_MODEL,), 1.0),
        'w_mod': nrm(ks[4], (DEPTH, D_MODEL, 3 * D_MODEL), 0.5 * D_MODEL ** -0.5),
        'b_mod': nrm(ks[5], (DEPTH, 3 * D_MODEL), 0.01),
        'norm_w': gain(ks[6], (DEPTH, D_MODEL)),
        'w_in': nrm(ks[7], (DEPTH, D_MODEL, IN_WIDTH), D_MODEL ** -0.5),
        'w_out': nrm(ks[8], (DEPTH, MIX_WIDTH, D_MODEL), MIX_WIDTH ** -0.5),
        'att_q_norm': gain(ks[9], (DEPTH, HEAD_DIM)),
        'att_k_norm': gain(ks[10], (DEPTH, HEAD_DIM)),
        'hgrn_lb_logits': nrm(ks[11], (DEPTH, 2, HGRN_KEY_WIDTH), 0.5),
        'hgrn_o_norm': gain(ks[12], (DEPTH, GROUP_WIDTH)),
        'conv_w': nrm(ks[13], (DEPTH, CONV_WIDTH, GROUP_WIDTH), CONV_WIDTH ** -0.5),
        'conv_b': nrm(ks[14], (DEPTH, GROUP_WIDTH), 0.01),
        'conv_ln_w': gain(ks[15], (DEPTH, GROUP_WIDTH)),
        'conv_ln_b': nrm(ks[16], (DEPTH, GROUP_WIDTH), 0.01),
        'mla_q_norm': gain(ks[17], (DEPTH, MLA_Q_RANK)),
        'mla_kv_norm': gain(ks[18], (DEPTH, MLA_KV_RANK)),
        'mla_w_uq': nrm(ks[19], (DEPTH, MLA_Q_RANK, MLA_HEADS * MLA_QK_DIM), MLA_Q_RANK ** -0.5),
        'mla_w_ukv': nrm(ks[20], (DEPTH, MLA_KV_RANK, MLA_HEADS * (MLA_NOPE_DIM + MLA_V_DIM)), MLA_KV_RANK ** -0.5),
        'mla_qk_q_norm': gain(ks[21], (DEPTH, MLA_QK_DIM)),
        'mla_qk_k_norm': gain(ks[22], (DEPTH, MLA_QK_DIM)),
    }


def reference(x, c, ctx, c_ctx, w_mod, b_mod, norm_w, w_in, w_out, att_q_norm, att_k_norm,
              hgrn_lb_logits, hgrn_o_norm, conv_w, conv_b, conv_ln_w, conv_ln_b,
              mla_q_norm, mla_kv_norm, mla_w_uq, mla_w_ukv, mla_qk_q_norm, mla_qk_k_norm):
    seq_len = x.shape[1]
    ROWS = seq_len // GRID_W
    rows = jnp.repeat(jnp.arange(ROWS, dtype=jnp.int32), GRID_W)
    cols = jnp.tile(jnp.arange(GRID_W, dtype=jnp.int32), ROWS)

    lb_all = jnp.cumsum(jax.nn.softmax(hgrn_lb_logits.astype(jnp.float32), axis=0), axis=0)
    lb_all = lb_all - lb_all[0:1]

    silu_c = jax.nn.silu(c)
    silu_cc = jax.nn.silu(c_ctx)
    for l in range(DEPTH):
        ctx_out = l < DEPTH - 1
        mod = silu_c @ w_mod[l] + b_mod[l]
        mod_c = silu_cc @ w_mod[l] + b_mod[l]
        shift, scale, gate = jnp.split(mod, 3, axis=-1)
        shift_c, scale_c, gate_c = jnp.split(mod_c, 3, axis=-1)
        h = rms_norm(x, norm_w[l]) * (1.0 + scale[:, None, :]) + shift[:, None, :]
        hc = rms_norm(ctx, norm_w[l]) * (1.0 + scale_c) + shift_c
        p = dict(w_in=w_in[l], w_out=w_out[l], att_q_norm=att_q_norm[l], att_k_norm=att_k_norm[l],
                 hgrn_o_norm=hgrn_o_norm[l], conv_w=conv_w[l], conv_b=conv_b[l],
                 conv_ln_w=conv_ln_w[l], conv_ln_b=conv_ln_b[l], mla_q_norm=mla_q_norm[l],
                 mla_kv_norm=mla_kv_norm[l], mla_w_uq=mla_w_uq[l], mla_w_ukv=mla_w_ukv[l],
                 mla_qk_q_norm=mla_qk_q_norm[l], mla_qk_k_norm=mla_qk_k_norm[l])
        out, out_c = hybrid_mixer(h, hc, p, lb_all[l], rows, cols, ctx_out)
        x = x + gate[:, None, :] * out
        if ctx_out:
            ctx = ctx + gate_c * out_c
    return x
```

```python
import functools

import numpy as np
import jax
import jax.numpy as jnp
from jax import lax
from jax.experimental import pallas as pl
from jax.experimental.pallas import tpu as pltpu

F32 = jnp.float32
BF16 = jnp.bfloat16

EPS = 1e-6
ROPE_THETA = 10000.0
GRID_W = 64
HEAD_DIM = 128
GROUP_WIDTH = 1024
N_HEADS = 8
ATT_KV_HEADS = 2
MLA_Q_RANK = 768
MLA_KV_RANK = 512
MLA_ROPE_DIM = 64
MLA_QK_DIM = 192
MLA_PAD_DIM = 256
CONV_WIDTH = 31
CONV_HALO = 16
HGRN_CHUNK = 64
HGRN_LEVELS = 6
ROW_TILE = 256

C_U, C_GLU, C_G = 0, 1024, 2048
A_Q, A_K, A_V, A_G = 3072, 4096, 4352, 4608
B_Q, B_I, B_FF, B_FB, B_G = 5632, 6656, 7680, 8704, 9728
D_G, D_CKV, D_CQ, D_KR = 10752, 11776, 12288, 13056
IN_PAD = 13312

VMEM_LIMIT = 56 * 1024 * 1024


def _cparams(sem):
    return pltpu.CompilerParams(dimension_semantics=sem, vmem_limit_bytes=VMEM_LIMIT)


def _silu(x):
    return x * jax.nn.sigmoid(x)


def _dot(a, b):
    return jnp.dot(a, b, preferred_element_type=F32)


def _dot_nt(a, b):
    return lax.dot_general(a, b, (((1,), (1,)), ((), ())), preferred_element_type=F32)


def _mod_kernel(c_ref, w_ref, b_ref, o_ref):
    a = _silu(c_ref[...]).astype(BF16)
    o_ref[0] = _dot(a, w_ref[0].astype(BF16)) + b_ref[0]


def _modulation(c8, w_mod, b_mod):
    depth, d, n = w_mod.shape
    tn = 512
    return pl.pallas_call(
        _mod_kernel,
        grid=(depth, n // tn),
        in_specs=[pl.BlockSpec((8, d), lambda l, j: (0, 0)),
                  pl.BlockSpec((1, d, tn), lambda l, j: (l, 0, j)),
                  pl.BlockSpec((1, 1, tn), lambda l, j: (l, 0, j))],
        out_specs=pl.BlockSpec((1, 8, tn), lambda l, j: (l, 0, j)),
        out_shape=jax.ShapeDtypeStruct((depth, 8, n), F32),
        compiler_params=_cparams(("parallel", "parallel")),
        name="modulation",
    )(c8, w_mod, b_mod.reshape(depth, 1, n))


def _norm_kernel(xc_ref, xl_ref, mod_ref, nw_ref, o_ref, *, d):
    b = pl.program_id(0)
    t = pl.program_id(1)

    def body(x, row):
        m = mod_ref[pl.ds(row, 1), :]
        shift = m[:, :d]
        scale = m[:, d:2 * d]
        ms = jnp.mean(x * x, axis=-1, keepdims=True)
        y = x * lax.rsqrt(ms + EPS) * nw_ref[...]
        o_ref[0] = (y * (1.0 + scale) + shift).astype(o_ref.dtype)

    @pl.when(t == 0)
    def _():
        body(xc_ref[0], 2)

    @pl.when(t > 0)
    def _():
        body(xl_ref[0], b)


def _norm_modulate(xc, xl, lat_off, mod, nw, n_tiles):
    bsz, _, d = xc.shape
    return pl.pallas_call(
        functools.partial(_norm_kernel, d=d),
        grid=(bsz, n_tiles),
        in_specs=[pl.BlockSpec((1, ROW_TILE, d), lambda b, t: (b, 0, 0)),
                  pl.BlockSpec((1, ROW_TILE, d), lambda b, t: (b, jnp.maximum(t - 1, 0) + lat_off, 0)),
                  pl.BlockSpec((8, 3 * d), lambda b, t: (0, 0)),
                  pl.BlockSpec((1, d), lambda b, t: (0, 0))],
        out_specs=pl.BlockSpec((1, ROW_TILE, d), lambda b, t: (b, t, 0)),
        out_shape=jax.ShapeDtypeStruct((bsz, n_tiles * ROW_TILE, d), BF16),
        compiler_params=_cparams(("parallel", "parallel")),
        name="norm_modulate",
    )(xc, xl, mod, nw.reshape(1, d))


def _mm_kernel(a_ref, w_ref, o_ref):
    o_ref[...] = _dot(a_ref[...], w_ref[...]).astype(o_ref.dtype)


def _pick_tile(n, cands):
    for c in cands:
        if n % c == 0:
            return c
    raise ValueError(f"no tile for {n}")


def _matmul(a, w, out_dtype):
    m, k = a.shape
    _, n = w.shape
    tm = _pick_tile(m, (1088, 1024, 768, 512, 256))
    tn = _pick_tile(n, (512, 256, 128))
    return pl.pallas_call(
        _mm_kernel,
        grid=(m // tm, n // tn),
        in_specs=[pl.BlockSpec((tm, k), lambda i, j: (i, 0)),
                  pl.BlockSpec((k, tn), lambda i, j: (0, j))],
        out_specs=pl.BlockSpec((tm, tn), lambda i, j: (i, j)),
        out_shape=jax.ShapeDtypeStruct((m, n), out_dtype),
        compiler_params=_cparams(("parallel", "parallel")),
        name="in_proj",
    )(a, w)


def _rope_tables(seq, ctx_len, rope_dim):
    quarter = rope_dim // 4
    inv = ROPE_THETA ** (-jnp.arange(quarter, dtype=F32) / quarter)
    pos = jnp.arange(seq, dtype=jnp.int32)
    ar = (pos // GRID_W).astype(F32)[:, None] * inv
    ac = (pos % GRID_W).astype(F32)[:, None] * inv
    z = jnp.zeros_like(ar)
    cos = jnp.concatenate([jnp.cos(ar), jnp.cos(ar), jnp.cos(ac), jnp.cos(ac)], axis=-1)
    sa = jnp.concatenate([-jnp.sin(ar), z, -jnp.sin(ac), z], axis=-1)
    sb = jnp.concatenate([z, jnp.sin(ar), z, jnp.sin(ac)], axis=-1)

    def full(tab, fill):
        tab = jnp.pad(tab, ((0, 0), (0, 128 - rope_dim)), constant_values=fill)
        return jnp.concatenate([jnp.full((ctx_len, 128), fill, F32), tab], axis=0)

    return full(cos, 1.0), full(sa, 0.0), full(sb, 0.0)


def _rope(x, cos, sa, sb, quarter):
    return x * cos + pltpu.roll(x, 128 - quarter, 1) * sa + pltpu.roll(x, quarter, 1) * sb


def _gqa_prep_kernel(q_ref, k_ref, v_ref, cos_ref, sa_ref, sb_ref, qn_ref, kn_ref, qo_ref, ko_ref, vo_ref):
    cos, sa, sb = cos_ref[...], sa_ref[...], sb_ref[...]

    def head(x, w, scale):
        ms = jnp.mean(x * x, axis=-1, keepdims=True)
        y = x * lax.rsqrt(ms + EPS) * w
        return (_rope(y, cos, sa, sb, HEAD_DIM // 4) * scale).astype(BF16)

    for h in range(N_HEADS):
        sl = slice(h * HEAD_DIM, (h + 1) * HEAD_DIM)
        qo_ref[0, :, sl] = head(q_ref[0, :, sl], qn_ref[...], HEAD_DIM ** -0.5)
    for h in range(ATT_KV_HEADS):
        sl = slice(h * HEAD_DIM, (h + 1) * HEAD_DIM)
        ko_ref[0, :, sl] = head(k_ref[0, :, sl], kn_ref[...], 1.0)
    vo_ref[0] = v_ref[0].astype(BF16)


def _gqa_prep(p3, tabs, qn, kn):
    bsz, t, _ = p3.shape
    nt = t // ROW_TILE
    kvw = ATT_KV_HEADS * HEAD_DIM
    tab_spec = pl.BlockSpec((ROW_TILE, 128), lambda b, i: (i, 0))
    vec_spec = pl.BlockSpec((1, 128), lambda b, i: (0, 0))
    return pl.pallas_call(
        _gqa_prep_kernel,
        grid=(bsz, nt),
        in_specs=[pl.BlockSpec((1, ROW_TILE, GROUP_WIDTH), lambda b, i: (b, i, A_Q // GROUP_WIDTH)),
                  pl.BlockSpec((1, ROW_TILE, kvw), lambda b, i: (b, i, A_K // kvw)),
                  pl.BlockSpec((1, ROW_TILE, kvw), lambda b, i: (b, i, A_V // kvw)),
                  tab_spec, tab_spec, tab_spec, vec_spec, vec_spec],
        out_specs=[pl.BlockSpec((1, ROW_TILE, GROUP_WIDTH), lambda b, i: (b, i, 0)),
                   pl.BlockSpec((1, ROW_TILE, kvw), lambda b, i: (b, i, 0)),
                   pl.BlockSpec((1, ROW_TILE, kvw), lambda b, i: (b, i, 0))],
        out_shape=[jax.ShapeDtypeStruct((bsz, t, GROUP_WIDTH), BF16),
                   jax.ShapeDtypeStruct((bsz, t, kvw), BF16),
                   jax.ShapeDtypeStruct((bsz, t, kvw), BF16)],
        compiler_params=_cparams(("parallel", "parallel")),
        name="gqa_prep",
    )(p3, p3, p3, *tabs, qn.reshape(1, 128), kn.reshape(1, 128))


def _attn_kernel(q_ref, k_ref, v_ref, g_ref, o_ref, *, n_grp, dq, ctx_len, t0):
    t = pl.program_id(2) + t0

    def attend(lk):
        k = k_ref[0, :lk, :]
        v = v_ref[0, :lk, :]
        for g in range(n_grp):
            q = q_ref[0, :, g * dq:(g + 1) * dq]
            s = _dot_nt(q, k)
            m = jnp.max(s, axis=-1, keepdims=True)
            p = jnp.exp(s - m)
            l = jnp.sum(p, axis=-1, keepdims=True)
            o = _dot(p.astype(BF16), v) / l
            sl = slice(g * HEAD_DIM, (g + 1) * HEAD_DIM)
            o_ref[0, :, sl] = (o * _silu(g_ref[0, :, sl])).astype(o_ref.dtype)

    if t0 == 0:
        @pl.when(t == 0)
        def _():
            attend(ctx_len)

        @pl.when(t > 0)
        def _():
            attend(k_ref.shape[1])
    else:
        attend(k_ref.shape[1])


def _attention(q, k, v, p3, gate_off, n_grp, dq, ctx_len, ctx_out):
    bsz, t, _ = q.shape
    hkv = v.shape[-1] // HEAD_DIM
    t0 = 0 if ctx_out else 1
    nt = t // ROW_TILE - t0
    gw = n_grp * HEAD_DIM
    return pl.pallas_call(
        functools.partial(_attn_kernel, n_grp=n_grp, dq=dq, ctx_len=ctx_len, t0=t0),
        grid=(bsz, hkv, nt),
        in_specs=[pl.BlockSpec((1, ROW_TILE, n_grp * dq), lambda b, h, i: (b, i + t0, h)),
                  pl.BlockSpec((1, t, dq), lambda b, h, i: (b, 0, h)),
                  pl.BlockSpec((1, t, HEAD_DIM), lambda b, h, i: (b, 0, h)),
                  pl.BlockSpec((1, ROW_TILE, gw), lambda b, h, i: (b, i + t0, gate_off // gw + h))],
        out_specs=pl.BlockSpec((1, ROW_TILE, gw), lambda b, h, i: (b, i, h)),
        out_shape=jax.ShapeDtypeStruct((bsz, nt * ROW_TILE, hkv * gw), BF16),
        compiler_params=_cparams(("parallel", "parallel", "parallel")),
        name="attention",
    )(q, k, v, p3)


def _mla_up_kernel(cq_ref, ckv_ref, kr_ref, cos_ref, sa_ref, sb_ref, cqn_ref, ckvn_ref, wq_ref, wk_ref, wv_ref,
                   qn_ref, kn_ref, qo_ref, ko_ref, vo_ref):
    cos, sa, sb = cos_ref[...], sa_ref[...], sb_ref[...]
    quarter = MLA_ROPE_DIM // 4

    def rms(x, w):
        return x * lax.rsqrt(jnp.mean(x * x, axis=-1, keepdims=True) + EPS) * w

    cq = rms(cq_ref[0], cqn_ref[...]).astype(BF16)
    ckv = rms(ckv_ref[0], ckvn_ref[...]).astype(BF16)
    qf = _dot(cq, wq_ref[...])
    kf = _dot(ckv, wk_ref[...])
    vo_ref[0] = _dot(ckv, wv_ref[...]).astype(BF16)
    kr = kr_ref[0]
    kr_ss = jnp.sum(kr * kr, axis=-1, keepdims=True)
    qw1, qw2 = qn_ref[:, :128], qn_ref[:, 128:]
    kw1, kw2 = kn_ref[:, :128], kn_ref[:, 128:]
    scale = MLA_QK_DIM ** -0.5
    for h in range(N_HEADS):
        lo = h * MLA_PAD_DIM
        q1 = qf[:, lo:lo + 128]
        q2 = qf[:, lo + 128:lo + 256]
        ss = jnp.sum(q1 * q1, axis=-1, keepdims=True) + jnp.sum(q2 * q2, axis=-1, keepdims=True)
        r = lax.rsqrt(ss / MLA_QK_DIM + EPS) * scale
        qo_ref[0, :, lo:lo + 128] = (q1 * r * qw1).astype(BF16)
        qo_ref[0, :, lo + 128:lo + 256] = _rope(q2 * r * qw2, cos, sa, sb, quarter).astype(BF16)
        k1 = kf[:, h * 128:(h + 1) * 128]
        r = lax.rsqrt((jnp.sum(k1 * k1, axis=-1, keepdims=True) + kr_ss) / MLA_QK_DIM + EPS)
        ko_ref[0, :, lo:lo + 128] = (k1 * r * kw1).astype(BF16)
        ko_ref[0, :, lo + 128:lo + 256] = _rope(kr * r * kw2, cos, sa, sb, quarter).astype(BF16)


def _mla_up(p3, tabs, cqn, ckvn, wq, wk, wv, qn, kn):
    bsz, t, _ = p3.shape
    nt = t // ROW_TILE
    hw = N_HEADS * MLA_PAD_DIM
    tab_spec = pl.BlockSpec((ROW_TILE, 128), lambda b, i: (i, 0))

    def whole(a):
        return pl.BlockSpec(a.shape, lambda b, i: (0,) * a.ndim)

    args = (cqn.reshape(1, -1), ckvn.reshape(1, -1), wq, wk, wv, qn.reshape(1, -1), kn.reshape(1, -1))
    return pl.pallas_call(
        _mla_up_kernel,
        grid=(bsz, nt),
        in_specs=[pl.BlockSpec((1, ROW_TILE, MLA_Q_RANK), lambda b, i: (b, i, D_CQ // MLA_Q_RANK)),
                  pl.BlockSpec((1, ROW_TILE, MLA_KV_RANK), lambda b, i: (b, i, D_CKV // MLA_KV_RANK)),
                  pl.BlockSpec((1, ROW_TILE, 128), lambda b, i: (b, i, D_KR // 128)),
                  tab_spec, tab_spec, tab_spec] + [whole(a) for a in args],
        out_specs=[pl.BlockSpec((1, ROW_TILE, hw), lambda b, i: (b, i, 0)),
                   pl.BlockSpec((1, ROW_TILE, hw), lambda b, i: (b, i, 0)),
                   pl.BlockSpec((1, ROW_TILE, GROUP_WIDTH), lambda b, i: (b, i, 0))],
        out_shape=[jax.ShapeDtypeStruct((bsz, t, hw), BF16),
                   jax.ShapeDtypeStruct((bsz, t, hw), BF16),
                   jax.ShapeDtypeStruct((bsz, t, GROUP_WIDTH), BF16)],
        compiler_params=_cparams(("parallel", "parallel")),
        name="mla_up",
    )(p3, p3, p3, *tabs, *args)


def _hgrn_consts():
    c = HGRN_CHUNK
    t = np.arange(c)[:, None]
    s = np.arange(c)[None, :]
    cum = np.stack([s <= t, s >= t]).astype(np.float32)
    masks = np.zeros((2, HGRN_LEVELS + 1, c, c), np.float32)
    masks[:, 0] = np.eye(c)
    for lvl in range(1, HGRN_LEVELS + 1):
        m = c >> lvl
        same = (t // (2 * m)) == (s // (2 * m))
        t_up, s_up = (t // m) % 2 == 1, (s // m) % 2 == 1
        masks[0, lvl] = same & t_up & ~s_up
        masks[1, lvl] = same & ~t_up & s_up
    return jnp.asarray(cum, BF16), jnp.asarray(masks)


def _hgrn_kernel(q_ref, i_ref, ff_ref, fb_ref, gate_ref, oml_ref, llb_ref, on_ref, cum_ref, mask_ref, y_ref,
                 g_scr, of_scr, ob_scr, *, n_ctx_chunks, n_chunks):
    c = HGRN_CHUNK
    sub = lax.broadcasted_iota(jnp.int32, (8, 128), 0)

    def level_ref(d, lvl):
        m = c >> lvl
        blk = 2 * m
        def row(r):
            return g_scr[d, r:r + 1, :]
        idx = [b * blk + m - 1 + d for b in range(c // blk)]
        if blk >= 8:
            return jnp.concatenate([jnp.broadcast_to(row(r), (blk, 128)) for r in idx], axis=0)
        per = 8 // blk
        tiles = []
        for j in range(c // 8):
            tile = jnp.broadcast_to(row(idx[j * per + per - 1]), (8, 128))
            for u in range(per - 2, -1, -1):
                tile = jnp.where(sub < (u + 1) * blk, jnp.broadcast_to(row(idx[j * per + u]), (8, 128)), tile)
            tiles.append(tile)
        return jnp.concatenate(tiles, axis=0)

    def chain_step(d, r0, st):
        rows = pl.ds(pl.multiple_of(r0, c), c)
        qv = q_ref[0, rows, :]
        qc = _silu(qv) * (HEAD_DIM ** -0.5)
        v = i_ref[0, rows, :]
        f = (ff_ref if d == 0 else fb_ref)[0, rows, :]
        k = oml_ref[d:d + 1, :] / (1.0 + jnp.exp(f))
        a = llb_ref[d:d + 1, :] - f
        lf = (jnp.minimum(f, 0.0) - jnp.log(1.0 + jnp.exp(-jnp.abs(f)))
              + jnp.maximum(a, 0.0) + jnp.log(1.0 + jnp.exp(-jnp.abs(a))))
        hi = lf.astype(BF16)
        r1 = lf - hi.astype(F32)
        mid = r1.astype(BF16)
        lo = (r1 - mid.astype(F32)).astype(BF16)
        cum = cum_ref[d]
        g = _dot(cum, hi) + _dot(cum, mid) + _dot(cum, lo)
        g_scr[d] = g
        end = (c - 1) * (1 - d)
        gtot = g_scr[d, end:end + 1, :]
        vb = v.astype(BF16)
        o = _dot_nt((qc * jnp.exp(g)).astype(BF16), st.astype(BF16))
        scores = jnp.where(mask_ref[d, 0] > 0.5, _dot_nt(qc.astype(BF16), k.astype(BF16)), 0.0)
        for lvl in range(1, HGRN_LEVELS + 1):
            e = jnp.exp(-jnp.abs(g - level_ref(d, lvl)))
            s_l = _dot_nt((qc * e).astype(BF16), (k * e).astype(BF16))
            scores = jnp.where(mask_ref[d, lvl] > 0.5, s_l, scores)
        o = o + _dot(scores.astype(BF16), vb)
        (of_scr if d == 0 else ob_scr)[rows, :] = o
        kend = (k * jnp.exp(gtot - g)).astype(BF16)
        return st * jnp.exp(gtot) + _dot(v.T.astype(BF16), kend)

    def body(n, carry):
        st_f, st_b = carry
        cb = jnp.where(n < n_ctx_chunks, n_ctx_chunks - 1 - n, n_chunks + n_ctx_chunks - 1 - n)
        st_f = chain_step(0, n * c, st_f)
        st_b = chain_step(1, cb * c, st_b)
        return st_f, st_b

    zero = jnp.zeros((HEAD_DIM, HEAD_DIM), F32)
    lax.fori_loop(0, n_chunks, body, (zero, zero))

    def readout(i, carry):
        rows = pl.ds(pl.multiple_of(i * ROW_TILE, ROW_TILE), ROW_TILE)
        o = of_scr[rows, :] + ob_scr[rows, :]
        o = o * lax.rsqrt(jnp.mean(o * o, axis=-1, keepdims=True) + EPS) * on_ref[...]
        y_ref[0, rows, :] = (o * _silu(gate_ref[0, rows, :])).astype(y_ref.dtype)
        return carry

    lax.fori_loop(0, (n_chunks * c) // ROW_TILE, readout, 0)


def _hgrn(p3, one_minus_lb, log_lb, o_norm, ctx_len):
    bsz, t, _ = p3.shape
    cum, masks = _hgrn_consts()

    def col(off):
        return pl.BlockSpec((1, t, HEAD_DIM), lambda b, h: (b, 0, off // HEAD_DIM + h))

    return pl.pallas_call(
        functools.partial(_hgrn_kernel, n_ctx_chunks=ctx_len // HGRN_CHUNK, n_chunks=t // HGRN_CHUNK),
        grid=(bsz, N_HEADS),
        in_specs=[col(B_Q), col(B_I), col(B_FF), col(B_FB), col(B_G),
                  pl.BlockSpec((2, HEAD_DIM), lambda b, h: (0, h)),
                  pl.BlockSpec((2, HEAD_DIM), lambda b, h: (0, h)),
                  pl.BlockSpec((1, HEAD_DIM), lambda b, h: (0, h)),
                  pl.BlockSpec(cum.shape, lambda b, h: (0, 0, 0)),
                  pl.BlockSpec(masks.shape, lambda b, h: (0, 0, 0, 0))],
        out_specs=pl.BlockSpec((1, t, HEAD_DIM), lambda b, h: (b, 0, h)),
        out_shape=jax.ShapeDtypeStruct((bsz, t, GROUP_WIDTH), BF16),
        scratch_shapes=[pltpu.VMEM((2, HGRN_CHUNK, HEAD_DIM), F32),
                        pltpu.VMEM((t, HEAD_DIM), F32),
                        pltpu.VMEM((t, HEAD_DIM), F32)],
        compiler_params=_cparams(("parallel", "parallel")),
        name="hgrn2",
    )(p3, p3, p3, p3, p3, one_minus_lb, log_lb, o_norm.reshape(1, -1), cum, masks)


def _conv_kernel(u_ref, up_ref, un_ref, gl_ref, glp_ref, gln_ref, gate_ref, w_ref, cb_ref, lw_ref, lb_ref, y_ref,
                 xs_scr, acc_scr, *, t0, n_tiles):
    t = pl.program_id(1) + t0
    h = CONV_HALO
    prev_ok = (t >= 2).astype(F32)
    next_ok = jnp.logical_and(t >= 1, t < n_tiles - 1).astype(F32)
    n_lane_blocks = GROUP_WIDTH // 128
    x_prev = up_ref[0] * jax.nn.sigmoid(glp_ref[0]) * prev_ok
    x_main = u_ref[0] * jax.nn.sigmoid(gl_ref[0])
    x_next = un_ref[0] * jax.nn.sigmoid(gln_ref[0]) * next_ok
    for cb in range(n_lane_blocks):
        sl = slice(cb * 128, (cb + 1) * 128)
        xs_scr[cb, 0:h, :] = x_prev[:, sl]
        xs_scr[cb, h:h + ROW_TILE, :] = x_main[:, sl]
        xs_scr[cb, h + ROW_TILE:, :] = x_next[:, sl]
    rc = 64
    base = h - CONV_WIDTH // 2

    def lane_block(cb, carry):
        for r in range(ROW_TILE // rc):
            acc = jnp.zeros((rc, 128), F32)
            for tap in range(CONV_WIDTH):
                lo = base + r * rc + tap
                acc = acc + xs_scr[cb, lo:lo + rc, :] * w_ref[cb, tap:tap + 1, :]
            acc_scr[cb, r * rc:(r + 1) * rc, :] = acc
        return carry

    lax.fori_loop(0, n_lane_blocks, lane_block, 0)
    y = jnp.concatenate([acc_scr[cb] for cb in range(n_lane_blocks)], axis=1) + cb_ref[...]
    mu = jnp.mean(y, axis=-1, keepdims=True)
    yc = y - mu
    var = jnp.mean(yc * yc, axis=-1, keepdims=True)
    z = yc * lax.rsqrt(var + EPS) * lw_ref[...] + lb_ref[...]
    y_ref[0] = (_silu(z) * _silu(gate_ref[0])).astype(y_ref.dtype)


def _conv(p3, conv_w, conv_b, ln_w, ln_b, ctx_out):
    bsz, t, _ = p3.shape
    n_tiles = t // ROW_TILE
    t0 = 0 if ctx_out else 1
    hpt = ROW_TILE // CONV_HALO
    n_halo = t // CONV_HALO
    n_lane_blocks = GROUP_WIDTH // 128
    wpad = jnp.pad(conv_w, ((0, 32 - CONV_WIDTH), (0, 0))).reshape(32, n_lane_blocks, 128).transpose(1, 0, 2)

    def main(off):
        return pl.BlockSpec((1, ROW_TILE, GROUP_WIDTH), lambda b, i: (b, i + t0, off // GROUP_WIDTH))

    def prev(off):
        return pl.BlockSpec((1, CONV_HALO, GROUP_WIDTH),
                            lambda b, i: (b, jnp.maximum((i + t0) * hpt - 1, 0), off // GROUP_WIDTH))

    def nxt(off):
        return pl.BlockSpec((1, CONV_HALO, GROUP_WIDTH),
                            lambda b, i: (b, jnp.minimum((i + t0 + 1) * hpt, n_halo - 1), off // GROUP_WIDTH))

    vec = pl.BlockSpec((1, GROUP_WIDTH), lambda b, i: (0, 0))
    return pl.pallas_call(
        functools.partial(_conv_kernel, t0=t0, n_tiles=n_tiles),
        grid=(bsz, n_tiles - t0),
        in_specs=[main(C_U), prev(C_U), nxt(C_U), main(C_GLU), prev(C_GLU), nxt(C_GLU), main(C_G),
                  pl.BlockSpec((n_lane_blocks, 32, 128), lambda b, i: (0, 0, 0)), vec, vec, vec],
        out_specs=pl.BlockSpec((1, ROW_TILE, GROUP_WIDTH), lambda b, i: (b, i, 0)),
        out_shape=jax.ShapeDtypeStruct((bsz, (n_tiles - t0) * ROW_TILE, GROUP_WIDTH), BF16),
        scratch_shapes=[pltpu.VMEM((n_lane_blocks, ROW_TILE + 2 * CONV_HALO, 128), F32),
                        pltpu.VMEM((n_lane_blocks, ROW_TILE, 128), F32)],
        compiler_params=_cparams(("parallel", "parallel")),
        name="conformer_conv",
    )(p3, p3, p3, p3, p3, p3, p3, wpad, conv_b.reshape(1, -1), ln_w.reshape(1, -1), ln_b.reshape(1, -1))


def _out_kernel(ya_ref, yb_ref, yc_ref, yd_ref, w_ref, xc_ref, xl_ref, gate_ref, o_ref, *, t0):
    b = pl.program_id(1)
    t = pl.program_id(2) + t0
    gw = GROUP_WIDTH
    acc = _dot(ya_ref[0], w_ref[0:gw, :])
    acc = acc + _dot(yb_ref[0], w_ref[gw:2 * gw, :])
    acc = acc + _dot(yc_ref[0], w_ref[2 * gw:3 * gw, :])
    acc = acc + _dot(yd_ref[0], w_ref[3 * gw:4 * gw, :])

    def finish(x, row):
        o_ref[0] = x + gate_ref[pl.ds(row, 1), :] * acc

    if t0 == 0:
        @pl.when(t == 0)
        def _():
            finish(xc_ref[0], 2)

        @pl.when(t > 0)
        def _():
            finish(xl_ref[0], b)
    else:
        finish(xl_ref[0], b)


def _out_proj(ys, w_out, xc, xl, lat_off, mod, ctx_out):
    bsz, rows, _ = ys[0].shape
    d = w_out.shape[1]
    t0 = 0 if ctx_out else 1
    nt = rows // ROW_TILE
    tn = 1024
    y_spec = pl.BlockSpec((1, ROW_TILE, GROUP_WIDTH), lambda j, b, i: (b, i, 0))
    yb_spec = pl.BlockSpec((1, ROW_TILE, GROUP_WIDTH), lambda j, b, i: (b, i + t0, 0))
    return pl.pallas_call(
        functools.partial(_out_kernel, t0=t0),
        grid=(d // tn, bsz, nt),
        in_specs=[y_spec, yb_spec, y_spec, y_spec,
                  pl.BlockSpec((4 * GROUP_WIDTH, tn), lambda j, b, i: (0, j)),
                  pl.BlockSpec((1, ROW_TILE, tn), lambda j, b, i: (b, 0, j)),
                  pl.BlockSpec((1, ROW_TILE, tn), lambda j, b, i: (b, jnp.maximum(i + t0 - 1, 0) + lat_off, j)),
                  pl.BlockSpec((8, tn), lambda j, b, i: (0, 2 * d // tn + j))],
        out_specs=pl.BlockSpec((1, ROW_TILE, tn), lambda j, b, i: (b, i, j)),
        out_shape=jax.ShapeDtypeStruct((bsz, nt * ROW_TILE, d), F32),
        compiler_params=_cparams(("parallel", "parallel", "parallel")),
        name="out_proj",
    )(*ys, w_out, xc, xl, mod)


def _pad_in_proj_weight(w):
    d = w.shape[0]
    z = lambda n: jnp.zeros((d, n), w.dtype)
    a, b, c, dd = w[:, 0:2560], w[:, 2560:7680], w[:, 7680:10752], w[:, 10752:13120]
    cq, ckv, kr, dg = dd[:, 0:768], dd[:, 768:1280], dd[:, 1280:1344], dd[:, 1344:2368]
    cols = [c, a, b, dg, ckv, cq, kr, z(128 - MLA_ROPE_DIM), z(IN_PAD - 13184)]
    return jnp.concatenate(cols, axis=1).astype(BF16)


def kernel(x, c, ctx, c_ctx, w_mod, b_mod, norm_w, w_in, w_out, att_q_norm, att_k_norm, hgrn_lb_logits, hgrn_o_norm,
           conv_w, conv_b, conv_ln_w, conv_ln_b, mla_q_norm, mla_kv_norm, mla_w_uq, mla_w_ukv, mla_qk_q_norm,
           mla_qk_k_norm):
    bsz, seq, d = x.shape
    ctx_len = ctx.shape[1]
    depth = w_mod.shape[0]
    t = ctx_len + seq
    assert ctx_len == ROW_TILE and seq % ROW_TILE == 0 and bsz <= 2
    n_tiles = t // ROW_TILE

    lb_all = jnp.cumsum(jax.nn.softmax(hgrn_lb_logits.astype(F32), axis=0), axis=0)
    lb_all = lb_all - lb_all[0:1]

    c8 = jnp.zeros((8, d), F32).at[0:bsz].set(c).at[2].set(c_ctx)
    mod_all = _modulation(c8, w_mod, b_mod)

    gqa_tabs = _rope_tables(seq, ctx_len, HEAD_DIM)
    mla_tabs = _rope_tables(seq, ctx_len, MLA_ROPE_DIM)

    xc, xl, lat_off = ctx, x, 0
    for l in range(depth):
        ctx_out = l < depth - 1
        mod = mod_all[l]
        h = _norm_modulate(xc, xl, lat_off, mod, norm_w[l], n_tiles)
        p = _matmul(h.reshape(bsz * t, d), _pad_in_proj_weight(w_in[l]), F32)
        p3 = p.reshape(bsz, t, IN_PAD)

        qa, ka, va = _gqa_prep(p3, gqa_tabs, att_q_norm[l], att_k_norm[l])
        ya = _attention(qa, ka, va, p3, A_G, N_HEADS // ATT_KV_HEADS, HEAD_DIM, ctx_len, ctx_out)

        yb = _hgrn(p3, 1.0 - lb_all[l], jnp.log(lb_all[l]), hgrn_o_norm[l], ctx_len)

        yc = _conv(p3, conv_w[l], conv_b[l], conv_ln_w[l], conv_ln_b[l], ctx_out)

        wq = jnp.pad(mla_w_uq[l].reshape(MLA_Q_RANK, N_HEADS, MLA_QK_DIM),
                     ((0, 0), (0, 0), (0, MLA_PAD_DIM - MLA_QK_DIM))).reshape(MLA_Q_RANK, -1).astype(BF16)
        wkv = mla_w_ukv[l].reshape(MLA_KV_RANK, N_HEADS, 2 * HEAD_DIM)
        wk = wkv[:, :, :HEAD_DIM].reshape(MLA_KV_RANK, -1).astype(BF16)
        wv = wkv[:, :, HEAD_DIM:].reshape(MLA_KV_RANK, -1).astype(BF16)
        qn = jnp.pad(mla_qk_q_norm[l], (0, MLA_PAD_DIM - MLA_QK_DIM))
        kn = jnp.pad(mla_qk_k_norm[l], (0, MLA_PAD_DIM - MLA_QK_DIM))
        qd, kd, vd = _mla_up(p3, mla_tabs, mla_q_norm[l], mla_kv_norm[l], wq, wk, wv, qn, kn)
        yd = _attention(qd, kd, vd, p3, D_G, 1, MLA_PAD_DIM, ctx_len, ctx_out)

        x_new = _out_proj((ya, yb, yc, yd), w_out[l].astype(BF16), xc, xl, lat_off, mod, ctx_out)
        xc, xl, lat_off = x_new, x_new, 1
    return x_new
```

```python
import functools

import numpy as np
import jax
import jax.numpy as jnp
from jax import lax
from jax.experimental import pallas as pl
from jax.experimental.pallas import tpu as pltpu

F32 = jnp.float32
BF16 = jnp.bfloat16

EPS = 1e-6
ROPE_THETA = 10000.0
GRID_W = 64
HEAD_DIM = 128
GROUP_WIDTH = 1024
N_HEADS = 8
ATT_KV_HEADS = 2
MLA_Q_RANK = 768
MLA_KV_RANK = 512
MLA_ROPE_DIM = 64
MLA_QK_DIM = 192
MLA_PAD_DIM = 256
CONV_WIDTH = 31
CONV_HALO = 16
HGRN_CHUNK = 64
HGRN_LEVELS = 6
HGRN_HEADS_PER_STEP = 2
ROW_TILE = 256

IN_TILE = 512
C_U, C_GLU, C_G = 0, 1024, 2048
A_Q, A_K, A_V, A_G = 3072, 4096, 4352, 4608
B_Q, B_I, B_FF, B_FB, B_G = 5632, 6656, 7680, 8704, 9728
ABC_WIDTH = 10752
D_G, D_CKV, D_CQ, D_KR = 0, 1024, 1536, 2304
D_WIDTH = 2560

VMEM_LIMIT = 56 * 1024 * 1024


def _cparams(sem):
    return pltpu.CompilerParams(dimension_semantics=sem, vmem_limit_bytes=VMEM_LIMIT)


def _silu(x):
    return x * jax.nn.sigmoid(x)


def _dot(a, b):
    return jnp.dot(a, b, preferred_element_type=F32)


def _dot_nt(a, b):
    return lax.dot_general(a, b, (((1,), (1,)), ((), ())), preferred_element_type=F32)


def _mod_kernel(c_ref, w_ref, b_ref, o_ref):
    a = _silu(c_ref[...]).astype(BF16)
    o_ref[0] = _dot(a, w_ref[0].astype(BF16)) + b_ref[0]


def _modulation(c8, w_mod, b_mod):
    depth, d, n = w_mod.shape
    tn = 512
    return pl.pallas_call(
        _mod_kernel,
        grid=(depth, n // tn),
        in_specs=[pl.BlockSpec((8, d), lambda l, j: (0, 0)),
                  pl.BlockSpec((1, d, tn), lambda l, j: (l, 0, j)),
                  pl.BlockSpec((1, 1, tn), lambda l, j: (l, 0, j))],
        out_specs=pl.BlockSpec((1, 8, tn), lambda l, j: (l, 0, j)),
        out_shape=jax.ShapeDtypeStruct((depth, 8, n), F32),
        compiler_params=_cparams(("parallel", "parallel")),
        name="modulation",
    )(c8, w_mod, b_mod.reshape(depth, 1, n))


def _norm_kernel(xc_ref, xl_ref, mod_ref, nw_ref, o_ref, *, d):
    b = pl.program_id(0)
    t = pl.program_id(1)

    def body(x, row):
        m = mod_ref[pl.ds(row, 1), :]
        shift = m[:, :d]
        scale = m[:, d:2 * d]
        ms = jnp.mean(x * x, axis=-1, keepdims=True)
        y = x * lax.rsqrt(ms + EPS) * nw_ref[...]
        o_ref[0] = (y * (1.0 + scale) + shift).astype(o_ref.dtype)

    @pl.when(t == 0)
    def _():
        body(xc_ref[0], 2)

    @pl.when(t > 0)
    def _():
        body(xl_ref[0], b)


def _norm_modulate(xc, xl, lat_off, mod, nw, n_tiles):
    bsz, _, d = xc.shape
    return pl.pallas_call(
        functools.partial(_norm_kernel, d=d),
        grid=(bsz, n_tiles),
        in_specs=[pl.BlockSpec((1, ROW_TILE, d), lambda b, t: (b, 0, 0)),
                  pl.BlockSpec((1, ROW_TILE, d), lambda b, t: (b, jnp.maximum(t - 1, 0) + lat_off, 0)),
                  pl.BlockSpec((8, 3 * d), lambda b, t: (0, 0)),
                  pl.BlockSpec((1, d), lambda b, t: (0, 0))],
        out_specs=pl.BlockSpec((1, ROW_TILE, d), lambda b, t: (b, t, 0)),
        out_shape=jax.ShapeDtypeStruct((bsz, n_tiles * ROW_TILE, d), BF16),
        compiler_params=_cparams(("parallel", "parallel")),
        name="norm_modulate",
    )(xc, xl, mod, nw.reshape(1, d))


def _mm_kernel(a_ref, w_ref, o_ref):
    o_ref[...] = _dot(a_ref[...], w_ref[...]).astype(o_ref.dtype)


def _pick_tile(n, cands):
    for c in cands:
        if n % c == 0:
            return c
    raise ValueError(f"no tile for {n}")


def _matmul(a, w, out_dtype):
    m, k = a.shape
    _, n = w.shape
    tm = _pick_tile(m, (1088, 1024, 768, 512, 256))
    tn = _pick_tile(n, (512, 256, 128))
    return pl.pallas_call(
        _mm_kernel,
        grid=(m // tm, n // tn),
        in_specs=[pl.BlockSpec((tm, k), lambda i, j: (i, 0)),
                  pl.BlockSpec((k, tn), lambda i, j: (0, j))],
        out_specs=pl.BlockSpec((tm, tn), lambda i, j: (i, j)),
        out_shape=jax.ShapeDtypeStruct((m, n), out_dtype),
        compiler_params=_cparams(("parallel", "parallel")),
        name="in_proj_d",
    )(a, w)


def _mm_wcast_kernel(a_ref, w_ref, o_ref, wb_scr):
    @pl.when(pl.program_id(1) == 0)
    def _():
        wb_scr[...] = w_ref[0].astype(BF16)

    o_ref[...] = _dot(a_ref[...], wb_scr[...]).astype(o_ref.dtype)


def _in_proj_abc(a, w_in, layer):
    m, k = a.shape
    tm = _pick_tile(m, (1088, 1024, 768, 512, 256))
    n_tiles = ABC_WIDTH // IN_TILE
    n_c = (C_G + GROUP_WIDTH) // IN_TILE
    return pl.pallas_call(
        _mm_wcast_kernel,
        grid=(n_tiles, m // tm),
        in_specs=[pl.BlockSpec((tm, k), lambda j, i: (i, 0)),
                  pl.BlockSpec((1, k, IN_TILE),
                               lambda j, i: (layer, 0, jnp.where(j < n_c, j + n_tiles - n_c, j - n_c)))],
        out_specs=pl.BlockSpec((tm, IN_TILE), lambda j, i: (i, j)),
        out_shape=jax.ShapeDtypeStruct((m, ABC_WIDTH), BF16),
        scratch_shapes=[pltpu.VMEM((k, IN_TILE), BF16)],
        compiler_params=_cparams(("parallel", "arbitrary")),
        name="in_proj_abc",
    )(a, w_in)


def _rope_tables(seq, ctx_len, rope_dim):
    quarter = rope_dim // 4
    inv = ROPE_THETA ** (-jnp.arange(quarter, dtype=F32) / quarter)
    pos = jnp.arange(seq, dtype=jnp.int32)
    ar = (pos // GRID_W).astype(F32)[:, None] * inv
    ac = (pos % GRID_W).astype(F32)[:, None] * inv
    z = jnp.zeros_like(ar)
    cos = jnp.concatenate([jnp.cos(ar), jnp.cos(ar), jnp.cos(ac), jnp.cos(ac)], axis=-1)
    sa = jnp.concatenate([-jnp.sin(ar), z, -jnp.sin(ac), z], axis=-1)
    sb = jnp.concatenate([z, jnp.sin(ar), z, jnp.sin(ac)], axis=-1)

    def full(tab, fill):
        tab = jnp.pad(tab, ((0, 0), (0, 128 - rope_dim)), constant_values=fill)
        return jnp.concatenate([jnp.full((ctx_len, 128), fill, F32), tab], axis=0)

    return full(cos, 1.0), full(sa, 0.0), full(sb, 0.0)


def _rope(x, cos, sa, sb, quarter):
    return x * cos + pltpu.roll(x, 128 - quarter, 1) * sa + pltpu.roll(x, quarter, 1) * sb


def _gqa_prep_kernel(q_ref, k_ref, v_ref, cos_ref, sa_ref, sb_ref, qn_ref, kn_ref, qo_ref, ko_ref, vo_ref):
    cos, sa, sb = cos_ref[...], sa_ref[...], sb_ref[...]

    def head(x, w, scale):
        ms = jnp.mean(x * x, axis=-1, keepdims=True)
        y = x * lax.rsqrt(ms + EPS) * w
        return (_rope(y, cos, sa, sb, HEAD_DIM // 4) * scale).astype(BF16)

    for h in range(N_HEADS):
        sl = slice(h * HEAD_DIM, (h + 1) * HEAD_DIM)
        qo_ref[0, :, sl] = head(q_ref[0, :, sl].astype(F32), qn_ref[...], HEAD_DIM ** -0.5)
    for h in range(ATT_KV_HEADS):
        sl = slice(h * HEAD_DIM, (h + 1) * HEAD_DIM)
        ko_ref[0, :, sl] = head(k_ref[0, :, sl].astype(F32), kn_ref[...], 1.0)
    vo_ref[0] = v_ref[0]


def _gqa_prep(p3, tabs, qn, kn):
    bsz, t, _ = p3.shape
    nt = t // ROW_TILE
    kvw = ATT_KV_HEADS * HEAD_DIM
    tab_spec = pl.BlockSpec((ROW_TILE, 128), lambda b, i: (i, 0))
    vec_spec = pl.BlockSpec((1, 128), lambda b, i: (0, 0))
    return pl.pallas_call(
        _gqa_prep_kernel,
        grid=(bsz, nt),
        in_specs=[pl.BlockSpec((1, ROW_TILE, GROUP_WIDTH), lambda b, i: (b, i, A_Q // GROUP_WIDTH)),
                  pl.BlockSpec((1, ROW_TILE, kvw), lambda b, i: (b, i, A_K // kvw)),
                  pl.BlockSpec((1, ROW_TILE, kvw), lambda b, i: (b, i, A_V // kvw)),
                  tab_spec, tab_spec, tab_spec, vec_spec, vec_spec],
        out_specs=[pl.BlockSpec((1, ROW_TILE, GROUP_WIDTH), lambda b, i: (b, i, 0)),
                   pl.BlockSpec((1, ROW_TILE, kvw), lambda b, i: (b, i, 0)),
                   pl.BlockSpec((1, ROW_TILE, kvw), lambda b, i: (b, i, 0))],
        out_shape=[jax.ShapeDtypeStruct((bsz, t, GROUP_WIDTH), BF16),
                   jax.ShapeDtypeStruct((bsz, t, kvw), BF16),
                   jax.ShapeDtypeStruct((bsz, t, kvw), BF16)],
        compiler_params=_cparams(("parallel", "parallel")),
        name="gqa_prep",
    )(p3, p3, p3, *tabs, qn.reshape(1, 128), kn.reshape(1, 128))


def _attn_kernel(q_ref, k_ref, v_ref, g_ref, o_ref, *, n_grp, dq, ctx_len, t0):
    t = pl.program_id(2) + t0

    def attend(lk):
        k = k_ref[0, :lk, :]
        v = v_ref[0, :lk, :]
        for g in range(n_grp):
            q = q_ref[0, :, g * dq:(g + 1) * dq]
            s = _dot_nt(q, k)
            m = jnp.max(s, axis=-1, keepdims=True)
            p = jnp.exp(s - m)
            l = jnp.sum(p, axis=-1, keepdims=True)
            o = _dot(p.astype(BF16), v) / l
            sl = slice(g * HEAD_DIM, (g + 1) * HEAD_DIM)
            o_ref[0, :, sl] = (o * _silu(g_ref[0, :, sl].astype(F32))).astype(o_ref.dtype)

    if t0 == 0:
        @pl.when(t == 0)
        def _():
            attend(ctx_len)

        @pl.when(t > 0)
        def _():
            attend(k_ref.shape[1])
    else:
        attend(k_ref.shape[1])


def _attention(q, k, v, p3, gate_off, n_grp, dq, ctx_len, ctx_out):
    bsz, t, _ = q.shape
    hkv = v.shape[-1] // HEAD_DIM
    t0 = 0 if ctx_out else 1
    nt = t // ROW_TILE - t0
    gw = n_grp * HEAD_DIM
    return pl.pallas_call(
        functools.partial(_attn_kernel, n_grp=n_grp, dq=dq, ctx_len=ctx_len, t0=t0),
        grid=(bsz, hkv, nt),
        in_specs=[pl.BlockSpec((1, ROW_TILE, n_grp * dq), lambda b, h, i: (b, i + t0, h)),
                  pl.BlockSpec((1, t, dq), lambda b, h, i: (b, 0, h)),
                  pl.BlockSpec((1, t, HEAD_DIM), lambda b, h, i: (b, 0, h)),
                  pl.BlockSpec((1, ROW_TILE, gw), lambda b, h, i: (b, i + t0, gate_off // gw + h))],
        out_specs=pl.BlockSpec((1, ROW_TILE, gw), lambda b, h, i: (b, i, h)),
        out_shape=jax.ShapeDtypeStruct((bsz, nt * ROW_TILE, hkv * gw), BF16),
        compiler_params=_cparams(("parallel", "parallel", "parallel")),
        name="attention",
    )(q, k, v, p3)


def _mla_up_kernel(cq_ref, ckv_ref, kr_ref, cos_ref, sa_ref, sb_ref, cqn_ref, ckvn_ref, wq_ref, wk_ref, wv_ref,
                   qn_ref, kn_ref, qo_ref, ko_ref, vo_ref):
    cos, sa, sb = cos_ref[...], sa_ref[...], sb_ref[...]
    quarter = MLA_ROPE_DIM // 4

    def rms(x, w):
        return x * lax.rsqrt(jnp.mean(x * x, axis=-1, keepdims=True) + EPS) * w

    cq = rms(cq_ref[0].astype(F32), cqn_ref[...]).astype(BF16)
    ckv = rms(ckv_ref[0].astype(F32), ckvn_ref[...]).astype(BF16)
    qf = _dot(cq, wq_ref[...])
    kf = _dot(ckv, wk_ref[...])
    vo_ref[0] = _dot(ckv, wv_ref[...]).astype(BF16)
    kr = kr_ref[0].astype(F32)
    kr_ss = jnp.sum(kr * kr, axis=-1, keepdims=True)
    qw1, qw2 = qn_ref[:, :128], qn_ref[:, 128:]
    kw1, kw2 = kn_ref[:, :128], kn_ref[:, 128:]
    scale = MLA_QK_DIM ** -0.5
    for h in range(N_HEADS):
        lo = h * MLA_PAD_DIM
        q1 = qf[:, lo:lo + 128]
        q2 = qf[:, lo + 128:lo + 256]
        ss = jnp.sum(q1 * q1, axis=-1, keepdims=True) + jnp.sum(q2 * q2, axis=-1, keepdims=True)
        r = lax.rsqrt(ss / MLA_QK_DIM + EPS) * scale
        qo_ref[0, :, lo:lo + 128] = (q1 * r * qw1).astype(BF16)
        qo_ref[0, :, lo + 128:lo + 256] = _rope(q2 * r * qw2, cos, sa, sb, quarter).astype(BF16)
        k1 = kf[:, h * 128:(h + 1) * 128]
        r = lax.rsqrt((jnp.sum(k1 * k1, axis=-1, keepdims=True) + kr_ss) / MLA_QK_DIM + EPS)
        ko_ref[0, :, lo:lo + 128] = (k1 * r * kw1).astype(BF16)
        ko_ref[0, :, lo + 128:lo + 256] = _rope(kr * r * kw2, cos, sa, sb, quarter).astype(BF16)


def _mla_up(p3, tabs, cqn, ckvn, wq, wk, wv, qn, kn):
    bsz, t, _ = p3.shape
    nt = t // ROW_TILE
    hw = N_HEADS * MLA_PAD_DIM
    tab_spec = pl.BlockSpec((ROW_TILE, 128), lambda b, i: (i, 0))

    def whole(a):
        return pl.BlockSpec(a.shape, lambda b, i: (0,) * a.ndim)

    args = (cqn.reshape(1, -1), ckvn.reshape(1, -1), wq, wk, wv, qn.reshape(1, -1), kn.reshape(1, -1))
    return pl.pallas_call(
        _mla_up_kernel,
        grid=(bsz, nt),
        in_specs=[pl.BlockSpec((1, ROW_TILE, MLA_Q_RANK), lambda b, i: (b, i, D_CQ // MLA_Q_RANK)),
                  pl.BlockSpec((1, ROW_TILE, MLA_KV_RANK), lambda b, i: (b, i, D_CKV // MLA_KV_RANK)),
                  pl.BlockSpec((1, ROW_TILE, 128), lambda b, i: (b, i, D_KR // 128)),
                  tab_spec, tab_spec, tab_spec] + [whole(a) for a in args],
        out_specs=[pl.BlockSpec((1, ROW_TILE, hw), lambda b, i: (b, i, 0)),
                   pl.BlockSpec((1, ROW_TILE, hw), lambda b, i: (b, i, 0)),
                   pl.BlockSpec((1, ROW_TILE, GROUP_WIDTH), lambda b, i: (b, i, 0))],
        out_shape=[jax.ShapeDtypeStruct((bsz, t, hw), BF16),
                   jax.ShapeDtypeStruct((bsz, t, hw), BF16),
                   jax.ShapeDtypeStruct((bsz, t, GROUP_WIDTH), BF16)],
        compiler_params=_cparams(("parallel", "parallel")),
        name="mla_up",
    )(p3, p3, p3, *tabs, *args)


def _hgrn_consts():
    c = HGRN_CHUNK
    t = np.arange(c)[:, None]
    s = np.arange(c)[None, :]
    cum = np.stack([s <= t, s >= t]).astype(np.float32)
    masks = np.zeros((2, HGRN_LEVELS + 1, c, c), np.float32)
    masks[:, 0] = np.eye(c)
    for lvl in range(1, HGRN_LEVELS + 1):
        m = c >> lvl
        same = (t // (2 * m)) == (s // (2 * m))
        t_up, s_up = (t // m) % 2 == 1, (s // m) % 2 == 1
        masks[0, lvl] = same & t_up & ~s_up
        masks[1, lvl] = same & ~t_up & s_up
    return jnp.asarray(cum, BF16), jnp.asarray(masks)


def _hgrn_kernel(q_ref, i_ref, ff_ref, fb_ref, gate_ref, oml_ref, llb_ref, on_ref, cum_ref, mask_ref, y_ref,
                 *scr, n_ctx_chunks, n_chunks):
    c = HGRN_CHUNK
    n_chain = 2 * HGRN_HEADS_PER_STEP
    g_scrs, o_scrs = scr[:n_chain], scr[n_chain:]
    sub = lax.broadcasted_iota(jnp.int32, (8, 128), 0)

    def level_ref(g_scr, d, lvl):
        m = c >> lvl
        blk = 2 * m
        def row(r):
            return g_scr[r:r + 1, :]
        idx = [b * blk + m - 1 + d for b in range(c // blk)]
        if blk >= 8:
            return jnp.concatenate([jnp.broadcast_to(row(r), (blk, 128)) for r in idx], axis=0)
        per = 8 // blk
        tiles = []
        for j in range(c // 8):
            tile = jnp.broadcast_to(row(idx[j * per + per - 1]), (8, 128))
            for u in range(per - 2, -1, -1):
                tile = jnp.where(sub < (u + 1) * blk, jnp.broadcast_to(row(idx[j * per + u]), (8, 128)), tile)
            tiles.append(tile)
        return jnp.concatenate(tiles, axis=0)

    chains = [(hh, d) for hh in range(HGRN_HEADS_PER_STEP) for d in range(2)]

    def body(n, carry):
        cb = jnp.where(n < n_ctx_chunks, n_ctx_chunks - 1 - n, n_chunks + n_ctx_chunks - 1 - n)
        rows_d = [pl.ds(pl.multiple_of(n * c, c), c), pl.ds(pl.multiple_of(cb * c, c), c)]
        qc, k, vb, g, gtot, o, scores = {}, {}, {}, {}, {}, {}, {}
        for ci, (hh, d) in enumerate(chains):
            rows = rows_d[d]
            cols = slice(hh * HEAD_DIM, (hh + 1) * HEAD_DIM)
            qv = q_ref[0, rows, cols].astype(F32)
            qc[ci] = _silu(qv) * (HEAD_DIM ** -0.5)
            vb[ci] = i_ref[0, rows, cols]
            f = (ff_ref if d == 0 else fb_ref)[0, rows, cols].astype(F32)
            k[ci] = oml_ref[d:d + 1, cols] / (1.0 + jnp.exp(f))
            a = llb_ref[d:d + 1, cols] - f
            lf = (jnp.minimum(f, 0.0) - jnp.log(1.0 + jnp.exp(-jnp.abs(f)))
                  + jnp.maximum(a, 0.0) + jnp.log(1.0 + jnp.exp(-jnp.abs(a))))
            hi = lf.astype(BF16)
            r1 = lf - hi.astype(F32)
            mid = r1.astype(BF16)
            lo = (r1 - mid.astype(F32)).astype(BF16)
            cum = cum_ref[d]
            g[ci] = _dot(cum, hi) + _dot(cum, mid) + _dot(cum, lo)
        for ci, (hh, d) in enumerate(chains):
            g_scrs[ci][...] = g[ci]
        for ci, (hh, d) in enumerate(chains):
            end = (c - 1) * (1 - d)
            gtot[ci] = g_scrs[ci][end:end + 1, :]
            o[ci] = _dot_nt((qc[ci] * jnp.exp(g[ci])).astype(BF16), carry[ci].astype(BF16))
            scores[ci] = jnp.where(mask_ref[d, 0] > 0.5, _dot_nt(qc[ci].astype(BF16), k[ci].astype(BF16)), 0.0)
        for lvl in range(1, HGRN_LEVELS + 1):
            for ci, (hh, d) in enumerate(chains):
                e = jnp.exp(-jnp.abs(g[ci] - level_ref(g_scrs[ci], d, lvl)))
                s_l = _dot_nt((qc[ci] * e).astype(BF16), (k[ci] * e).astype(BF16))
                scores[ci] = jnp.where(mask_ref[d, lvl] > 0.5, s_l, scores[ci])
        new = []
        for ci, (hh, d) in enumerate(chains):
            o_scrs[ci][rows_d[d], :] = o[ci] + _dot(scores[ci].astype(BF16), vb[ci])
            kend = (k[ci] * jnp.exp(gtot[ci] - g[ci])).astype(BF16)
            vt = vb[ci].astype(F32).T.astype(BF16)
            new.append(carry[ci] * jnp.exp(gtot[ci]) + _dot(vt, kend))
        return tuple(new)

    zero = jnp.zeros((HEAD_DIM, HEAD_DIM), F32)
    lax.fori_loop(0, n_chunks, body, (zero,) * n_chain)

    def readout(i, carry):
        rows = pl.ds(pl.multiple_of(i * ROW_TILE, ROW_TILE), ROW_TILE)
        for hh in range(HGRN_HEADS_PER_STEP):
            cols = slice(hh * HEAD_DIM, (hh + 1) * HEAD_DIM)
            o = o_scrs[2 * hh][rows, :] + o_scrs[2 * hh + 1][rows, :]
            o = o * lax.rsqrt(jnp.mean(o * o, axis=-1, keepdims=True) + EPS) * on_ref[:, cols]
            y_ref[0, rows, cols] = (o * _silu(gate_ref[0, rows, cols].astype(F32))).astype(y_ref.dtype)
        return carry

    lax.fori_loop(0, (n_chunks * c) // ROW_TILE, readout, 0)


def _hgrn(p3, one_minus_lb, log_lb, o_norm, ctx_len):
    bsz, t, _ = p3.shape
    cum, masks = _hgrn_consts()
    bw = HGRN_HEADS_PER_STEP * HEAD_DIM
    n_chain = 2 * HGRN_HEADS_PER_STEP

    def col(off):
        return pl.BlockSpec((1, t, bw), lambda b, h: (b, 0, off // bw + h))

    return pl.pallas_call(
        functools.partial(_hgrn_kernel, n_ctx_chunks=ctx_len // HGRN_CHUNK, n_chunks=t // HGRN_CHUNK),
        grid=(bsz, GROUP_WIDTH // bw),
        in_specs=[col(B_Q), col(B_I), col(B_FF), col(B_FB), col(B_G),
                  pl.BlockSpec((2, bw), lambda b, h: (0, h)),
                  pl.BlockSpec((2, bw), lambda b, h: (0, h)),
                  pl.BlockSpec((1, bw), lambda b, h: (0, h)),
                  pl.BlockSpec(cum.shape, lambda b, h: (0, 0, 0)),
                  pl.BlockSpec(masks.shape, lambda b, h: (0, 0, 0, 0))],
        out_specs=pl.BlockSpec((1, t, bw), lambda b, h: (b, 0, h)),
        out_shape=jax.ShapeDtypeStruct((bsz, t, GROUP_WIDTH), BF16),
        scratch_shapes=([pltpu.VMEM((HGRN_CHUNK, HEAD_DIM), F32)] * n_chain
                        + [pltpu.VMEM((t, HEAD_DIM), F32)] * n_chain),
        compiler_params=_cparams(("parallel", "parallel")),
        name="hgrn2",
    )(p3, p3, p3, p3, p3, one_minus_lb, log_lb, o_norm.reshape(1, -1), cum, masks)


def _conv_kernel(u_ref, up_ref, un_ref, gl_ref, glp_ref, gln_ref, gate_ref, w_ref, cb_ref, lw_ref, lb_ref, y_ref,
                 xs_scr, acc_scr, *, t0, n_tiles):
    t = pl.program_id(1) + t0
    h = CONV_HALO
    prev_ok = (t >= 2).astype(F32)
    next_ok = jnp.logical_and(t >= 1, t < n_tiles - 1).astype(F32)
    n_lane_blocks = GROUP_WIDTH // 128
    x_prev = up_ref[0].astype(F32) * jax.nn.sigmoid(glp_ref[0].astype(F32)) * prev_ok
    x_main = u_ref[0].astype(F32) * jax.nn.sigmoid(gl_ref[0].astype(F32))
    x_next = un_ref[0].astype(F32) * jax.nn.sigmoid(gln_ref[0].astype(F32)) * next_ok
    for cb in range(n_lane_blocks):
        sl = slice(cb * 128, (cb + 1) * 128)
        xs_scr[cb, 0:h, :] = x_prev[:, sl]
        xs_scr[cb, h:h + ROW_TILE, :] = x_main[:, sl]
        xs_scr[cb, h + ROW_TILE:, :] = x_next[:, sl]
    rc = 64
    base = h - CONV_WIDTH // 2

    def lane_block(cb, carry):
        for r in range(ROW_TILE // rc):
            acc = jnp.zeros((rc, 128), F32)
            for tap in range(CONV_WIDTH):
                lo = base + r * rc + tap
                acc = acc + xs_scr[cb, lo:lo + rc, :] * w_ref[cb, tap:tap + 1, :]
            acc_scr[cb, r * rc:(r + 1) * rc, :] = acc
        return carry

    lax.fori_loop(0, n_lane_blocks, lane_block, 0)
    y = jnp.concatenate([acc_scr[cb] for cb in range(n_lane_blocks)], axis=1) + cb_ref[...]
    mu = jnp.mean(y, axis=-1, keepdims=True)
    yc = y - mu
    var = jnp.mean(yc * yc, axis=-1, keepdims=True)
    z = yc * lax.rsqrt(var + EPS) * lw_ref[...] + lb_ref[...]
    y_ref[0] = (_silu(z) * _silu(gate_ref[0].astype(F32))).astype(y_ref.dtype)


def _conv(p3, conv_w, conv_b, ln_w, ln_b, ctx_out):
    bsz, t, _ = p3.shape
    n_tiles = t // ROW_TILE
    t0 = 0 if ctx_out else 1
    hpt = ROW_TILE // CONV_HALO
    n_halo = t // CONV_HALO
    n_lane_blocks = GROUP_WIDTH // 128
    wpad = jnp.pad(conv_w, ((0, 32 - CONV_WIDTH), (0, 0))).reshape(32, n_lane_blocks, 128).transpose(1, 0, 2)

    def main(off):
        return pl.BlockSpec((1, ROW_TILE, GROUP_WIDTH), lambda b, i: (b, i + t0, off // GROUP_WIDTH))

    def prev(off):
        return pl.BlockSpec((1, CONV_HALO, GROUP_WIDTH),
                            lambda b, i: (b, jnp.maximum((i + t0) * hpt - 1, 0), off // GROUP_WIDTH))

    def nxt(off):
        return pl.BlockSpec((1, CONV_HALO, GROUP_WIDTH),
                            lambda b, i: (b, jnp.minimum((i + t0 + 1) * hpt, n_halo - 1), off // GROUP_WIDTH))

    vec = pl.BlockSpec((1, GROUP_WIDTH), lambda b, i: (0, 0))
    return pl.pallas_call(
        functools.partial(_conv_kernel, t0=t0, n_tiles=n_tiles),
        grid=(bsz, n_tiles - t0),
        in_specs=[main(C_U), prev(C_U), nxt(C_U), main(C_GLU), prev(C_GLU), nxt(C_GLU), main(C_G),
                  pl.BlockSpec((n_lane_blocks, 32, 128), lambda b, i: (0, 0, 0)), vec, vec, vec],
        out_specs=pl.BlockSpec((1, ROW_TILE, GROUP_WIDTH), lambda b, i: (b, i, 0)),
        out_shape=jax.ShapeDtypeStruct((bsz, (n_tiles - t0) * ROW_TILE, GROUP_WIDTH), BF16),
        scratch_shapes=[pltpu.VMEM((n_lane_blocks, ROW_TILE + 2 * CONV_HALO, 128), F32),
                        pltpu.VMEM((n_lane_blocks, ROW_TILE, 128), F32)],
        compiler_params=_cparams(("parallel", "parallel")),
        name="conformer_conv",
    )(p3, p3, p3, p3, p3, p3, p3, wpad, conv_b.reshape(1, -1), ln_w.reshape(1, -1), ln_b.reshape(1, -1))


def _out_kernel(ya_ref, yb_ref, yc_ref, yd_ref, w_ref, xc_ref, xl_ref, gate_ref, o_ref, wb_scr, *, t0):
    b = pl.program_id(1)
    t = pl.program_id(2) + t0
    gw = GROUP_WIDTH

    @pl.when(jnp.logical_and(b == 0, pl.program_id(2) == 0))
    def _():
        wb_scr[...] = w_ref[0].astype(BF16)

    acc = _dot(ya_ref[0], wb_scr[0:gw, :])
    acc = acc + _dot(yb_ref[0], wb_scr[gw:2 * gw, :])
    acc = acc + _dot(yc_ref[0], wb_scr[2 * gw:3 * gw, :])
    acc = acc + _dot(yd_ref[0], wb_scr[3 * gw:4 * gw, :])

    def finish(x, row):
        o_ref[0] = x + gate_ref[pl.ds(row, 1), :] * acc

    if t0 == 0:
        @pl.when(t == 0)
        def _():
            finish(xc_ref[0], 2)

        @pl.when(t > 0)
        def _():
            finish(xl_ref[0], b)
    else:
        finish(xl_ref[0], b)


def _out_proj(ys, w_out, layer, xc, xl, lat_off, mod, ctx_out):
    bsz, rows, _ = ys[0].shape
    d = w_out.shape[2]
    t0 = 0 if ctx_out else 1
    nt = rows // ROW_TILE
    tn = 512
    y_spec = pl.BlockSpec((1, ROW_TILE, GROUP_WIDTH), lambda j, b, i: (b, i, 0))
    yb_spec = pl.BlockSpec((1, ROW_TILE, GROUP_WIDTH), lambda j, b, i: (b, i + t0, 0))
    return pl.pallas_call(
        functools.partial(_out_kernel, t0=t0),
        grid=(d // tn, bsz, nt),
        in_specs=[y_spec, yb_spec, y_spec, y_spec,
                  pl.BlockSpec((1, 4 * GROUP_WIDTH, tn), lambda j, b, i: (layer, 0, j)),
                  pl.BlockSpec((1, ROW_TILE, tn), lambda j, b, i: (b, 0, j)),
                  pl.BlockSpec((1, ROW_TILE, tn), lambda j, b, i: (b, jnp.maximum(i + t0 - 1, 0) + lat_off, j)),
                  pl.BlockSpec((8, tn), lambda j, b, i: (0, 2 * d // tn + j))],
        out_specs=pl.BlockSpec((1, ROW_TILE, tn), lambda j, b, i: (b, i, j)),
        out_shape=jax.ShapeDtypeStruct((bsz, nt * ROW_TILE, d), F32),
        scratch_shapes=[pltpu.VMEM((4 * GROUP_WIDTH, tn), BF16)],
        compiler_params=_cparams(("parallel", "arbitrary", "arbitrary")),
        name="out_proj",
    )(*ys, w_out, xc, xl, mod)


def _group_d_weight(w):
    dd = w[:, ABC_WIDTH:]
    cq, ckv, kr, dg = dd[:, 0:768], dd[:, 768:1280], dd[:, 1280:1344], dd[:, 1344:2368]
    z = jnp.zeros((w.shape[0], D_WIDTH - D_KR - MLA_ROPE_DIM), w.dtype)
    return jnp.concatenate([dg, ckv, cq, kr, z], axis=1).astype(BF16)


def kernel(x, c, ctx, c_ctx, w_mod, b_mod, norm_w, w_in, w_out, att_q_norm, att_k_norm, hgrn_lb_logits, hgrn_o_norm,
           conv_w, conv_b, conv_ln_w, conv_ln_b, mla_q_norm, mla_kv_norm, mla_w_uq, mla_w_ukv, mla_qk_q_norm,
           mla_qk_k_norm):
    bsz, seq, d = x.shape
    ctx_len = ctx.shape[1]
    depth = w_mod.shape[0]
    t = ctx_len + seq
    assert ctx_len == ROW_TILE and seq % ROW_TILE == 0 and bsz <= 2
    n_tiles = t // ROW_TILE

    lb_all = jnp.cumsum(jax.nn.softmax(hgrn_lb_logits.astype(F32), axis=0), axis=0)
    lb_all = lb_all - lb_all[0:1]

    c8 = jnp.zeros((8, d), F32).at[0:bsz].set(c).at[2].set(c_ctx)
    mod_all = _modulation(c8, w_mod, b_mod)

    gqa_tabs = _rope_tables(seq, ctx_len, HEAD_DIM)
    mla_tabs = _rope_tables(seq, ctx_len, MLA_ROPE_DIM)

    xc, xl, lat_off = ctx, x, 0
    for l in range(depth):
        ctx_out = l < depth - 1
        mod = mod_all[l]
        h = _norm_modulate(xc, xl, lat_off, mod, norm_w[l], n_tiles)
        h2 = h.reshape(bsz * t, d)
        p3 = _in_proj_abc(h2, w_in, l).reshape(bsz, t, ABC_WIDTH)
        pd3 = _matmul(h2, _group_d_weight(w_in[l]), BF16).reshape(bsz, t, D_WIDTH)

        qa, ka, va = _gqa_prep(p3, gqa_tabs, att_q_norm[l], att_k_norm[l])
        ya = _attention(qa, ka, va, p3, A_G, N_HEADS // ATT_KV_HEADS, HEAD_DIM, ctx_len, ctx_out)

        yb = _hgrn(p3, 1.0 - lb_all[l], jnp.log(lb_all[l]), hgrn_o_norm[l], ctx_len)

        yc = _conv(p3, conv_w[l], conv_b[l], conv_ln_w[l], conv_ln_b[l], ctx_out)

        wq = jnp.pad(mla_w_uq[l].reshape(MLA_Q_RANK, N_HEADS, MLA_QK_DIM),
                     ((0, 0), (0, 0), (0, MLA_PAD_DIM - MLA_QK_DIM))).reshape(MLA_Q_RANK, -1).astype(BF16)
        wkv = mla_w_ukv[l].reshape(MLA_KV_RANK, N_HEADS, 2 * HEAD_DIM)
        wk = wkv[:, :, :HEAD_DIM].reshape(MLA_KV_RANK, -1).astype(BF16)
        wv = wkv[:, :, HEAD_DIM:].reshape(MLA_KV_RANK, -1).astype(BF16)
        qn = jnp.pad(mla_qk_q_norm[l], (0, MLA_PAD_DIM - MLA_QK_DIM))
        kn = jnp.pad(mla_qk_k_norm[l], (0, MLA_PAD_DIM - MLA_QK_DIM))
        qd, kd, vd = _mla_up(pd3, mla_tabs, mla_q_norm[l], mla_kv_norm[l], wq, wk, wv, qn, kn)
        yd = _attention(qd, kd, vd, pd3, D_G, 1, MLA_PAD_DIM, ctx_len, ctx_out)

        x_new = _out_proj((ya, yb, yc, yd), w_out, l, xc, xl, lat_off, mod, ctx_out)
        xc, xl, lat_off = x_new, x_new, 1
    return x_new
```

```python
import functools

import numpy as np
import jax
import jax.numpy as jnp
from jax import lax
from jax.experimental import pallas as pl
from jax.experimental.pallas import tpu as pltpu

F32 = jnp.float32
BF16 = jnp.bfloat16

EPS = 1e-6
ROPE_THETA = 10000.0
GRID_W = 64
HEAD_DIM = 128
GROUP_WIDTH = 1024
N_HEADS = 8
ATT_KV_HEADS = 2
MLA_Q_RANK = 768
MLA_KV_RANK = 512
MLA_ROPE_DIM = 64
MLA_QK_DIM = 192
MLA_PAD_DIM = 256
CONV_WIDTH = 31
CONV_HALO = 16
HGRN_CHUNK = 64
HGRN_LEVELS = 6
HGRN_HEADS_PER_STEP = 2
ROW_TILE = 256

IN_TILE = 512
C_U, C_GLU, C_G = 0, 1024, 2048
A_Q, A_K, A_V, A_G = 3072, 4096, 4352, 4608
B_Q, B_I, B_FF, B_FB, B_G = 5632, 6656, 7680, 8704, 9728
ABC_WIDTH = 10752
D_G, D_CKV, D_CQ, D_KR = 0, 1024, 1536, 2304
D_WIDTH = 2560

VMEM_LIMIT = 56 * 1024 * 1024


def _cparams(sem):
    return pltpu.CompilerParams(dimension_semantics=sem, vmem_limit_bytes=VMEM_LIMIT)


def _silu(x):
    return x * jax.nn.sigmoid(x)


def _dot(a, b):
    return jnp.dot(a, b, preferred_element_type=F32)


def _dot_nt(a, b):
    return lax.dot_general(a, b, (((1,), (1,)), ((), ())), preferred_element_type=F32)


def _mod_kernel(c_ref, w_ref, b_ref, o_ref):
    a = _silu(c_ref[...]).astype(BF16)
    o_ref[0] = _dot(a, w_ref[0].astype(BF16)) + b_ref[0]


def _modulation(c8, w_mod, b_mod):
    depth, d, n = w_mod.shape
    tn = 512
    return pl.pallas_call(
        _mod_kernel,
        grid=(depth, n // tn),
        in_specs=[pl.BlockSpec((8, d), lambda l, j: (0, 0)),
                  pl.BlockSpec((1, d, tn), lambda l, j: (l, 0, j)),
                  pl.BlockSpec((1, 1, tn), lambda l, j: (l, 0, j))],
        out_specs=pl.BlockSpec((1, 8, tn), lambda l, j: (l, 0, j)),
        out_shape=jax.ShapeDtypeStruct((depth, 8, n), F32),
        compiler_params=_cparams(("parallel", "parallel")),
        name="modulation",
    )(c8, w_mod, b_mod.reshape(depth, 1, n))


def _norm_kernel(xc_ref, xl_ref, mod_ref, nw_ref, o_ref, *, d, n_lat):
    b = pl.program_id(0)
    t = pl.program_id(1)

    def body(x, row):
        m = mod_ref[pl.ds(row, 1), :]
        shift = m[:, :d]
        scale = m[:, d:2 * d]
        ms = jnp.mean(x * x, axis=-1, keepdims=True)
        y = x * lax.rsqrt(ms + EPS) * nw_ref[...]
        o_ref[0] = (y * (1.0 + scale) + shift).astype(o_ref.dtype)

    @pl.when(t == n_lat)
    def _():
        body(xc_ref[0], 2)

    @pl.when(t < n_lat)
    def _():
        body(xl_ref[0], b)


def _norm_modulate(xc, xl, mod, nw, n_tiles):
    bsz, _, d = xc.shape
    n_lat = n_tiles - 1
    return pl.pallas_call(
        functools.partial(_norm_kernel, d=d, n_lat=n_lat),
        grid=(bsz, n_tiles),
        in_specs=[pl.BlockSpec((1, ROW_TILE, d), lambda b, t: (b, 0, 0)),
                  pl.BlockSpec((1, ROW_TILE, d), lambda b, t: (b, jnp.minimum(t, n_lat - 1), 0)),
                  pl.BlockSpec((8, 3 * d), lambda b, t: (0, 0)),
                  pl.BlockSpec((1, d), lambda b, t: (0, 0))],
        out_specs=pl.BlockSpec((1, ROW_TILE, d), lambda b, t: (b, t, 0)),
        out_shape=jax.ShapeDtypeStruct((bsz, n_tiles * ROW_TILE, d), BF16),
        compiler_params=_cparams(("parallel", "parallel")),
        name="norm_modulate",
    )(xc, xl, mod, nw.reshape(1, d))


def _mm_kernel(a_ref, w_ref, o_ref):
    o_ref[...] = _dot_nt(a_ref[...], w_ref[...]).astype(o_ref.dtype)


def _pick_tile(n, cands):
    for c in cands:
        if n % c == 0:
            return c
    raise ValueError(f"no tile for {n}")


def _matmul_nt(a, wt, out_dtype):
    m, k = a.shape
    n, _ = wt.shape
    tm = _pick_tile(m, (1088, 1024, 768, 512, 256))
    tn = _pick_tile(n, (512, 256, 128))
    return pl.pallas_call(
        _mm_kernel,
        grid=(m // tm, n // tn),
        in_specs=[pl.BlockSpec((tm, k), lambda i, j: (i, 0)),
                  pl.BlockSpec((tn, k), lambda i, j: (j, 0))],
        out_specs=pl.BlockSpec((tm, tn), lambda i, j: (i, j)),
        out_shape=jax.ShapeDtypeStruct((m, n), out_dtype),
        compiler_params=_cparams(("parallel", "parallel")),
        name="in_proj_d",
    )(a, wt)


def _mm_wcast_kernel(a_ref, w_ref, o_ref, wb_scr):
    @pl.when(pl.program_id(1) == 0)
    def _():
        wb_scr[...] = w_ref[0].astype(BF16)

    o_ref[...] = _dot_nt(a_ref[...], wb_scr[...]).astype(o_ref.dtype)


def _in_proj_abc(a, w_in_t, layer):
    m, k = a.shape
    tm = _pick_tile(m, (1088, 1024, 768, 512, 256))
    n_tiles = ABC_WIDTH // IN_TILE
    n_c = (C_G + GROUP_WIDTH) // IN_TILE
    return pl.pallas_call(
        _mm_wcast_kernel,
        grid=(n_tiles, m // tm),
        in_specs=[pl.BlockSpec((tm, k), lambda j, i: (i, 0)),
                  pl.BlockSpec((1, IN_TILE, k),
                               lambda j, i: (layer, jnp.where(j < n_c, j + n_tiles - n_c, j - n_c), 0))],
        out_specs=pl.BlockSpec((tm, IN_TILE), lambda j, i: (i, j)),
        out_shape=jax.ShapeDtypeStruct((m, ABC_WIDTH), BF16),
        scratch_shapes=[pltpu.VMEM((IN_TILE, k), BF16)],
        compiler_params=_cparams(("parallel", "arbitrary")),
        name="in_proj_abc",
    )(a, w_in_t)


def _rope_tables(seq, ctx_len, rope_dim):
    quarter = rope_dim // 4
    inv = ROPE_THETA ** (-jnp.arange(quarter, dtype=F32) / quarter)
    pos = jnp.arange(seq, dtype=jnp.int32)
    ar = (pos // GRID_W).astype(F32)[:, None] * inv
    ac = (pos % GRID_W).astype(F32)[:, None] * inv
    z = jnp.zeros_like(ar)
    cos = jnp.concatenate([jnp.cos(ar), jnp.cos(ar), jnp.cos(ac), jnp.cos(ac)], axis=-1)
    sa = jnp.concatenate([-jnp.sin(ar), z, -jnp.sin(ac), z], axis=-1)
    sb = jnp.concatenate([z, jnp.sin(ar), z, jnp.sin(ac)], axis=-1)

    def full(tab, fill):
        tab = jnp.pad(tab, ((0, 0), (0, 128 - rope_dim)), constant_values=fill)
        return jnp.concatenate([tab, jnp.full((ctx_len, 128), fill, F32)], axis=0)

    return full(cos, 1.0), full(sa, 0.0), full(sb, 0.0)


def _rope(x, cos, sa, sb, quarter):
    return x * cos + pltpu.roll(x, 128 - quarter, 1) * sa + pltpu.roll(x, quarter, 1) * sb


def _gqa_prep_kernel(q_ref, k_ref, v_ref, cos_ref, sa_ref, sb_ref, qn_ref, kn_ref, qo_ref, ko_ref, vo_ref):
    cos, sa, sb = cos_ref[...], sa_ref[...], sb_ref[...]

    def head(x, w, scale):
        ms = jnp.mean(x * x, axis=-1, keepdims=True)
        y = x * lax.rsqrt(ms + EPS) * w
        return (_rope(y, cos, sa, sb, HEAD_DIM // 4) * scale).astype(BF16)

    for h in range(N_HEADS):
        sl = slice(h * HEAD_DIM, (h + 1) * HEAD_DIM)
        qo_ref[0, :, sl] = head(q_ref[0, :, sl].astype(F32), qn_ref[...], HEAD_DIM ** -0.5)
    for h in range(ATT_KV_HEADS):
        sl = slice(h * HEAD_DIM, (h + 1) * HEAD_DIM)
        ko_ref[0, :, sl] = head(k_ref[0, :, sl].astype(F32), kn_ref[...], 1.0)
        _store_value_ext(vo_ref, h, v_ref[0, :, sl])


def _store_value_ext(vo_ref, h, v):
    vo_ref[0, :, 2 * h * HEAD_DIM:(2 * h + 1) * HEAD_DIM] = v
    vo_ref[0, :, (2 * h + 1) * HEAD_DIM:(2 * h + 2) * HEAD_DIM] = jnp.ones(v.shape, v.dtype)


def _gqa_prep(p3, tabs, qn, kn):
    bsz, t, _ = p3.shape
    nt = t // ROW_TILE
    kvw = ATT_KV_HEADS * HEAD_DIM
    tab_spec = pl.BlockSpec((ROW_TILE, 128), lambda b, i: (i, 0))
    vec_spec = pl.BlockSpec((1, 128), lambda b, i: (0, 0))
    return pl.pallas_call(
        _gqa_prep_kernel,
        grid=(bsz, nt),
        in_specs=[pl.BlockSpec((1, ROW_TILE, GROUP_WIDTH), lambda b, i: (b, i, A_Q // GROUP_WIDTH)),
                  pl.BlockSpec((1, ROW_TILE, kvw), lambda b, i: (b, i, A_K // kvw)),
                  pl.BlockSpec((1, ROW_TILE, kvw), lambda b, i: (b, i, A_V // kvw)),
                  tab_spec, tab_spec, tab_spec, vec_spec, vec_spec],
        out_specs=[pl.BlockSpec((1, ROW_TILE, GROUP_WIDTH), lambda b, i: (b, i, 0)),
                   pl.BlockSpec((1, ROW_TILE, kvw), lambda b, i: (b, i, 0)),
                   pl.BlockSpec((1, ROW_TILE, 2 * kvw), lambda b, i: (b, i, 0))],
        out_shape=[jax.ShapeDtypeStruct((bsz, t, GROUP_WIDTH), BF16),
                   jax.ShapeDtypeStruct((bsz, t, kvw), BF16),
                   jax.ShapeDtypeStruct((bsz, t, 2 * kvw), BF16)],
        compiler_params=_cparams(("parallel", "parallel")),
        name="gqa_prep",
    )(p3, p3, p3, *tabs, qn.reshape(1, 128), kn.reshape(1, 128))


def _attn_kernel(q_ref, k_ref, v_ref, g_ref, o_ref, *, n_heads, dq, shared_kv, n_lat):
    i = pl.program_id(2)
    vw = 2 * HEAD_DIM

    def attend(lo):
        def kv(j):
            if shared_kv:
                return k_ref[0, lo:, :], v_ref[0, lo:, :]
            return k_ref[0, lo:, j * dq:(j + 1) * dq], v_ref[0, lo:, j * vw:(j + 1) * vw]

        s = [None] * n_heads
        p = [None] * n_heads
        for j in range(n_heads + 2):
            if j < n_heads:
                s[j] = _dot_nt(q_ref[0, :, j * dq:(j + 1) * dq], kv(j)[0])
            if 1 <= j <= n_heads:
                m = jnp.max(s[j - 1], axis=-1, keepdims=True)
                p[j - 1] = jnp.exp((s[j - 1] - m).astype(BF16))
                s[j - 1] = None
            if j >= 2:
                oe = _dot(p[j - 2], kv(j - 2)[1])
                p[j - 2] = None
                o = oe[:, :HEAD_DIM] / oe[:, HEAD_DIM:]
                sl = slice((j - 2) * HEAD_DIM, (j - 1) * HEAD_DIM)
                o_ref[0, :, sl] = (o * _silu(g_ref[0, :, sl].astype(F32))).astype(o_ref.dtype)

    @pl.when(i < n_lat)
    def _():
        attend(0)

    @pl.when(i == n_lat)
    def _():
        attend(n_lat * ROW_TILE)


def _attention(q, k, v_ext, p3, gate_off, n_heads, dq, shared_kv, ctx_out):
    bsz, t, _ = q.shape
    n_lat = t // ROW_TILE - 1
    nt = n_lat + (1 if ctx_out else 0)
    n_kv = 1 if shared_kv else n_heads
    gw = n_heads * HEAD_DIM
    n_steps = q.shape[-1] // (n_heads * dq)
    return pl.pallas_call(
        functools.partial(_attn_kernel, n_heads=n_heads, dq=dq, shared_kv=shared_kv, n_lat=n_lat),
        grid=(bsz, n_steps, nt),
        in_specs=[pl.BlockSpec((1, ROW_TILE, n_heads * dq), lambda b, h, i: (b, i, h)),
                  pl.BlockSpec((1, t, n_kv * dq), lambda b, h, i: (b, 0, h)),
                  pl.BlockSpec((1, t, n_kv * 2 * HEAD_DIM), lambda b, h, i: (b, 0, h)),
                  pl.BlockSpec((1, ROW_TILE, gw), lambda b, h, i: (b, i, gate_off // gw + h))],
        out_specs=pl.BlockSpec((1, ROW_TILE, gw), lambda b, h, i: (b, i, h)),
        out_shape=jax.ShapeDtypeStruct((bsz, nt * ROW_TILE, n_steps * gw), BF16),
        compiler_params=_cparams(("parallel", "parallel", "parallel")),
        name="attention",
    )(q, k, v_ext, p3)


def _mla_up_kernel(cq_ref, ckv_ref, kr_ref, cos_ref, sa_ref, sb_ref, cqn_ref, ckvn_ref, wq_ref, wk_ref, wv_ref,
                   qn_ref, kn_ref, qo_ref, ko_ref, vo_ref):
    cos, sa, sb = cos_ref[...], sa_ref[...], sb_ref[...]
    quarter = MLA_ROPE_DIM // 4

    def rms(x, w):
        return x * lax.rsqrt(jnp.mean(x * x, axis=-1, keepdims=True) + EPS) * w

    cq = rms(cq_ref[0].astype(F32), cqn_ref[...]).astype(BF16)
    ckv = rms(ckv_ref[0].astype(F32), ckvn_ref[...]).astype(BF16)
    qf = _dot(cq, wq_ref[...])
    kf = _dot(ckv, wk_ref[...])
    vf = _dot(ckv, wv_ref[...]).astype(BF16)
    kr = kr_ref[0].astype(F32)
    kr_ss = jnp.sum(kr * kr, axis=-1, keepdims=True)
    qw1, qw2 = qn_ref[:, :128], qn_ref[:, 128:]
    kw1, kw2 = kn_ref[:, :128], kn_ref[:, 128:]
    scale = MLA_QK_DIM ** -0.5
    for h in range(N_HEADS):
        lo = h * MLA_PAD_DIM
        q1 = qf[:, lo:lo + 128]
        q2 = qf[:, lo + 128:lo + 256]
        ss = jnp.sum(q1 * q1, axis=-1, keepdims=True) + jnp.sum(q2 * q2, axis=-1, keepdims=True)
        r = lax.rsqrt(ss / MLA_QK_DIM + EPS) * scale
        qo_ref[0, :, lo:lo + 128] = (q1 * r * qw1).astype(BF16)
        qo_ref[0, :, lo + 128:lo + 256] = _rope(q2 * r * qw2, cos, sa, sb, quarter).astype(BF16)
        k1 = kf[:, h * 128:(h + 1) * 128]
        r = lax.rsqrt((jnp.sum(k1 * k1, axis=-1, keepdims=True) + kr_ss) / MLA_QK_DIM + EPS)
        ko_ref[0, :, lo:lo + 128] = (k1 * r * kw1).astype(BF16)
        ko_ref[0, :, lo + 128:lo + 256] = _rope(kr * r * kw2, cos, sa, sb, quarter).astype(BF16)
        _store_value_ext(vo_ref, h, vf[:, h * 128:(h + 1) * 128])


def _mla_up(p3, tabs, cqn, ckvn, wq, wk, wv, qn, kn):
    bsz, t, _ = p3.shape
    nt = t // ROW_TILE
    hw = N_HEADS * MLA_PAD_DIM
    tab_spec = pl.BlockSpec((ROW_TILE, 128), lambda b, i: (i, 0))

    def whole(a):
        return pl.BlockSpec(a.shape, lambda b, i: (0,) * a.ndim)

    args = (cqn.reshape(1, -1), ckvn.reshape(1, -1), wq, wk, wv, qn.reshape(1, -1), kn.reshape(1, -1))
    return pl.pallas_call(
        _mla_up_kernel,
        grid=(bsz, nt),
        in_specs=[pl.BlockSpec((1, ROW_TILE, MLA_Q_RANK), lambda b, i: (b, i, D_CQ // MLA_Q_RANK)),
                  pl.BlockSpec((1, ROW_TILE, MLA_KV_RANK), lambda b, i: (b, i, D_CKV // MLA_KV_RANK)),
                  pl.BlockSpec((1, ROW_TILE, 128), lambda b, i: (b, i, D_KR // 128)),
                  tab_spec, tab_spec, tab_spec] + [whole(a) for a in args],
        out_specs=[pl.BlockSpec((1, ROW_TILE, hw), lambda b, i: (b, i, 0)),
                   pl.BlockSpec((1, ROW_TILE, hw), lambda b, i: (b, i, 0)),
                   pl.BlockSpec((1, ROW_TILE, 2 * GROUP_WIDTH), lambda b, i: (b, i, 0))],
        out_shape=[jax.ShapeDtypeStruct((bsz, t, hw), BF16),
                   jax.ShapeDtypeStruct((bsz, t, hw), BF16),
                   jax.ShapeDtypeStruct((bsz, t, 2 * GROUP_WIDTH), BF16)],
        compiler_params=_cparams(("parallel", "parallel")),
        name="mla_up",
    )(p3, p3, p3, *tabs, *args)


def _hgrn_consts():
    c = HGRN_CHUNK
    t = np.arange(c)[:, None]
    s = np.arange(c)[None, :]
    cum = np.stack([s <= t, s >= t]).astype(np.float32)
    masks = np.zeros((2, HGRN_LEVELS + 1, c, c), np.float32)
    masks[:, 0] = np.eye(c)
    for lvl in range(1, HGRN_LEVELS + 1):
        m = c >> lvl
        same = (t // (2 * m)) == (s // (2 * m))
        t_up, s_up = (t // m) % 2 == 1, (s // m) % 2 == 1
        masks[0, lvl] = same & t_up & ~s_up
        masks[1, lvl] = same & ~t_up & s_up
    return jnp.asarray(cum, BF16), jnp.asarray(masks)


def _hgrn_kernel(q_ref, i_ref, ff_ref, fb_ref, gate_ref, oml_ref, llb_ref, on_ref, cum_ref, mask_ref, y_ref,
                 *scr, n_ctx_chunks, n_chunks):
    c = HGRN_CHUNK
    n_chain = 2 * HGRN_HEADS_PER_STEP
    g_scrs, o_scrs = scr[:n_chain], scr[n_chain:]
    sub = lax.broadcasted_iota(jnp.int32, (8, 128), 0)

    def level_ref(g_scr, d, lvl):
        m = c >> lvl
        blk = 2 * m
        def row(r):
            return g_scr[r:r + 1, :]
        idx = [b * blk + m - 1 + d for b in range(c // blk)]
        if blk >= 8:
            return jnp.concatenate([jnp.broadcast_to(row(r), (blk, 128)) for r in idx], axis=0)
        per = 8 // blk
        tiles = []
        for j in range(c // 8):
            tile = jnp.broadcast_to(row(idx[j * per + per - 1]), (8, 128))
            for u in range(per - 2, -1, -1):
                tile = jnp.where(sub < (u + 1) * blk, jnp.broadcast_to(row(idx[j * per + u]), (8, 128)), tile)
            tiles.append(tile)
        return jnp.concatenate(tiles, axis=0)

    chains = [(hh, d) for hh in range(HGRN_HEADS_PER_STEP) for d in range(2)]

    def body(n, carry):
        n_lat_chunks = n_chunks - n_ctx_chunks
        cf = jnp.where(n < n_ctx_chunks, n_lat_chunks + n, n - n_ctx_chunks)
        cb = n_chunks - 1 - n
        rows_d = [pl.ds(pl.multiple_of(cf * c, c), c), pl.ds(pl.multiple_of(cb * c, c), c)]
        qc, k, vb, g, gtot, o, scores = {}, {}, {}, {}, {}, {}, {}
        for ci, (hh, d) in enumerate(chains):
            rows = rows_d[d]
            cols = slice(hh * HEAD_DIM, (hh + 1) * HEAD_DIM)
            qv = q_ref[0, rows, cols].astype(F32)
            qc[ci] = _silu(qv) * (HEAD_DIM ** -0.5)
            vb[ci] = i_ref[0, rows, cols]
            f = (ff_ref if d == 0 else fb_ref)[0, rows, cols].astype(F32)
            k[ci] = oml_ref[d:d + 1, cols] / (1.0 + jnp.exp(f))
            a = llb_ref[d:d + 1, cols] - f
            lf = (jnp.minimum(f, 0.0) - jnp.log(1.0 + jnp.exp(-jnp.abs(f)))
                  + jnp.maximum(a, 0.0) + jnp.log(1.0 + jnp.exp(-jnp.abs(a))))
            hi = lf.astype(BF16)
            r1 = lf - hi.astype(F32)
            mid = r1.astype(BF16)
            lo = (r1 - mid.astype(F32)).astype(BF16)
            cum = cum_ref[d]
            g[ci] = _dot(cum, hi) + _dot(cum, mid) + _dot(cum, lo)
        for ci, (hh, d) in enumerate(chains):
            g_scrs[ci][...] = g[ci]
        for ci, (hh, d) in enumerate(chains):
            end = (c - 1) * (1 - d)
            gtot[ci] = g_scrs[ci][end:end + 1, :]
            o[ci] = _dot_nt((qc[ci] * jnp.exp(g[ci])).astype(BF16), carry[ci].astype(BF16))
            scores[ci] = jnp.where(mask_ref[d, 0] > 0.5, _dot_nt(qc[ci].astype(BF16), k[ci].astype(BF16)), 0.0)
        for lvl in range(1, HGRN_LEVELS + 1):
            for ci, (hh, d) in enumerate(chains):
                e = jnp.exp(-jnp.abs(g[ci] - level_ref(g_scrs[ci], d, lvl)))
                s_l = _dot_nt((qc[ci] * e).astype(BF16), (k[ci] * e).astype(BF16))
                scores[ci] = jnp.where(mask_ref[d, lvl] > 0.5, s_l, scores[ci])
        new = []
        for ci, (hh, d) in enumerate(chains):
            o_scrs[ci][rows_d[d], :] = o[ci] + _dot(scores[ci].astype(BF16), vb[ci])
            kend = (k[ci] * jnp.exp(gtot[ci] - g[ci])).astype(BF16)
            vt = vb[ci].astype(F32).T.astype(BF16)
            new.append(carry[ci] * jnp.exp(gtot[ci]) + _dot(vt, kend))
        return tuple(new)

    zero = jnp.zeros((HEAD_DIM, HEAD_DIM), F32)
    lax.fori_loop(0, n_chunks, body, (zero,) * n_chain)

    n_row_tiles = (n_chunks * c) // ROW_TILE

    def readout(i, carry):
        rows = pl.ds(pl.multiple_of(i * ROW_TILE, ROW_TILE), ROW_TILE)
        for hh in range(HGRN_HEADS_PER_STEP):
            cols = slice(hh * HEAD_DIM, (hh + 1) * HEAD_DIM)
            o = o_scrs[2 * hh][rows, :] + o_scrs[2 * hh + 1][rows, :]
            o = o * lax.rsqrt(jnp.mean(o * o, axis=-1, keepdims=True) + EPS) * on_ref[:, cols]
            y_ref[0, rows, cols] = (o * _silu(gate_ref[0, rows, cols].astype(F32))).astype(y_ref.dtype)
        return carry

    lax.fori_loop(0, n_row_tiles, readout, 0)


def _hgrn(p3, one_minus_lb, log_lb, o_norm, ctx_len):
    bsz, t, _ = p3.shape
    cum, masks = _hgrn_consts()
    bw = HGRN_HEADS_PER_STEP * HEAD_DIM
    n_chain = 2 * HGRN_HEADS_PER_STEP

    def col(off):
        return pl.BlockSpec((1, t, bw), lambda b, h: (b, 0, off // bw + h))

    return pl.pallas_call(
        functools.partial(_hgrn_kernel, n_ctx_chunks=ctx_len // HGRN_CHUNK, n_chunks=t // HGRN_CHUNK),
        grid=(bsz, GROUP_WIDTH // bw),
        in_specs=[col(B_Q), col(B_I), col(B_FF), col(B_FB), col(B_G),
                  pl.BlockSpec((2, bw), lambda b, h: (0, h)),
                  pl.BlockSpec((2, bw), lambda b, h: (0, h)),
                  pl.BlockSpec((1, bw), lambda b, h: (0, h)),
                  pl.BlockSpec(cum.shape, lambda b, h: (0, 0, 0)),
                  pl.BlockSpec(masks.shape, lambda b, h: (0, 0, 0, 0))],
        out_specs=pl.BlockSpec((1, t, bw), lambda b, h: (b, 0, h)),
        out_shape=jax.ShapeDtypeStruct((bsz, t, GROUP_WIDTH), BF16),
        scratch_shapes=([pltpu.VMEM((HGRN_CHUNK, HEAD_DIM), F32)] * n_chain
                        + [pltpu.VMEM((t, HEAD_DIM), F32)] * n_chain),
        compiler_params=_cparams(("parallel", "parallel")),
        name="hgrn2",
    )(p3, p3, p3, p3, p3, one_minus_lb, log_lb, o_norm.reshape(1, -1), cum, masks)


def _conv_kernel(u_ref, up_ref, un_ref, gl_ref, glp_ref, gln_ref, gate_ref, w_ref, cb_ref, lw_ref, lb_ref, y_ref,
                 xs_scr, acc_scr, *, n_lat):
    t = pl.program_id(1)
    h = CONV_HALO
    prev_ok = jnp.logical_and(t >= 1, t < n_lat).astype(F32)
    next_ok = (t < n_lat - 1).astype(F32)
    n_lane_blocks = GROUP_WIDTH // 128
    x_prev = up_ref[0].astype(F32) * jax.nn.sigmoid(glp_ref[0].astype(F32)) * prev_ok
    x_main = u_ref[0].astype(F32) * jax.nn.sigmoid(gl_ref[0].astype(F32))
    x_next = un_ref[0].astype(F32) * jax.nn.sigmoid(gln_ref[0].astype(F32)) * next_ok
    for cb in range(n_lane_blocks):
        sl = slice(cb * 128, (cb + 1) * 128)
        xs_scr[cb, 0:h, :] = x_prev[:, sl]
        xs_scr[cb, h:h + ROW_TILE, :] = x_main[:, sl]
        xs_scr[cb, h + ROW_TILE:, :] = x_next[:, sl]
    rc = 64
    base = h - CONV_WIDTH // 2

    def lane_block(cb, carry):
        for r in range(ROW_TILE // rc):
            acc = jnp.zeros((rc, 128), F32)
            for tap in range(CONV_WIDTH):
                lo = base + r * rc + tap
                acc = acc + xs_scr[cb, lo:lo + rc, :] * w_ref[cb, tap:tap + 1, :]
            acc_scr[cb, r * rc:(r + 1) * rc, :] = acc
        return carry

    lax.fori_loop(0, n_lane_blocks, lane_block, 0)
    y = jnp.concatenate([acc_scr[cb] for cb in range(n_lane_blocks)], axis=1) + cb_ref[...]
    mu = jnp.mean(y, axis=-1, keepdims=True)
    yc = y - mu
    var = jnp.mean(yc * yc, axis=-1, keepdims=True)
    z = yc * lax.rsqrt(var + EPS) * lw_ref[...] + lb_ref[...]
    y_ref[0] = (_silu(z) * _silu(gate_ref[0].astype(F32))).astype(y_ref.dtype)


def _conv(p3, conv_w, conv_b, ln_w, ln_b, ctx_out):
    bsz, t, _ = p3.shape
    n_lat = t // ROW_TILE - 1
    nt = n_lat + (1 if ctx_out else 0)
    hpt = ROW_TILE // CONV_HALO
    n_halo = t // CONV_HALO
    n_lane_blocks = GROUP_WIDTH // 128
    wpad = jnp.pad(conv_w, ((0, 32 - CONV_WIDTH), (0, 0))).reshape(32, n_lane_blocks, 128).transpose(1, 0, 2)

    def main(off):
        return pl.BlockSpec((1, ROW_TILE, GROUP_WIDTH), lambda b, i: (b, i, off // GROUP_WIDTH))

    def prev(off):
        return pl.BlockSpec((1, CONV_HALO, GROUP_WIDTH),
                            lambda b, i: (b, jnp.maximum(i * hpt - 1, 0), off // GROUP_WIDTH))

    def nxt(off):
        return pl.BlockSpec((1, CONV_HALO, GROUP_WIDTH),
                            lambda b, i: (b, jnp.minimum((i + 1) * hpt, n_halo - 1), off // GROUP_WIDTH))

    vec = pl.BlockSpec((1, GROUP_WIDTH), lambda b, i: (0, 0))
    return pl.pallas_call(
        functools.partial(_conv_kernel, n_lat=n_lat),
        grid=(bsz, nt),
        in_specs=[main(C_U), prev(C_U), nxt(C_U), main(C_GLU), prev(C_GLU), nxt(C_GLU), main(C_G),
                  pl.BlockSpec((n_lane_blocks, 32, 128), lambda b, i: (0, 0, 0)), vec, vec, vec],
        out_specs=pl.BlockSpec((1, ROW_TILE, GROUP_WIDTH), lambda b, i: (b, i, 0)),
        out_shape=jax.ShapeDtypeStruct((bsz, nt * ROW_TILE, GROUP_WIDTH), BF16),
        scratch_shapes=[pltpu.VMEM((n_lane_blocks, ROW_TILE + 2 * CONV_HALO, 128), F32),
                        pltpu.VMEM((n_lane_blocks, ROW_TILE, 128), F32)],
        compiler_params=_cparams(("parallel", "parallel")),
        name="conformer_conv",
    )(p3, p3, p3, p3, p3, p3, p3, wpad, conv_b.reshape(1, -1), ln_w.reshape(1, -1), ln_b.reshape(1, -1))


def _out_kernel(ya_ref, yb_ref, yc_ref, yd_ref, w_ref, x_ref, gate_ref, o_ref, wb_scr, *, ctx_rows):
    b = pl.program_id(1)
    gw = GROUP_WIDTH

    @pl.when(jnp.logical_and(b == 0, pl.program_id(2) == 0))
    def _():
        wb_scr[...] = w_ref[0].astype(BF16)

    acc = _dot(ya_ref[0], wb_scr[0:gw, :])
    acc = acc + _dot(yb_ref[0], wb_scr[gw:2 * gw, :])
    acc = acc + _dot(yc_ref[0], wb_scr[2 * gw:3 * gw, :])
    acc = acc + _dot(yd_ref[0], wb_scr[3 * gw:4 * gw, :])

    row = 2 if ctx_rows else b
    o_ref[0] = x_ref[0] + gate_ref[pl.ds(row, 1), :] * acc


def _out_proj(ys, w_out, layer, x, mod, ctx_rows):
    bsz, rows, d = x.shape
    tm = ROW_TILE if ctx_rows else _pick_tile(rows, (1024, 512, 256))
    y_off = (ys[0].shape[1] - rows) // tm if ctx_rows else 0
    tn = 512
    y_spec = pl.BlockSpec((1, tm, GROUP_WIDTH), lambda j, b, i: (b, i + y_off, 0))
    return pl.pallas_call(
        functools.partial(_out_kernel, ctx_rows=ctx_rows),
        grid=(d // tn, bsz, rows // tm),
        in_specs=[y_spec, y_spec, y_spec, y_spec,
                  pl.BlockSpec((1, 4 * GROUP_WIDTH, tn), lambda j, b, i: (layer, 0, j)),
                  pl.BlockSpec((1, tm, tn), lambda j, b, i: (b, i, j)),
                  pl.BlockSpec((8, tn), lambda j, b, i: (0, 2 * d // tn + j))],
        out_specs=pl.BlockSpec((1, tm, tn), lambda j, b, i: (b, i, j)),
        out_shape=jax.ShapeDtypeStruct((bsz, rows, d), F32),
        scratch_shapes=[pltpu.VMEM((4 * GROUP_WIDTH, tn), BF16)],
        compiler_params=_cparams(("parallel", "arbitrary", "arbitrary")),
        name="out_proj",
    )(*ys, w_out, x, mod)


def _group_d_weight(wt):
    dd = wt[ABC_WIDTH:]
    cq, ckv, kr, dg = dd[0:768], dd[768:1280], dd[1280:1344], dd[1344:2368]
    z = jnp.zeros((D_WIDTH - D_KR - MLA_ROPE_DIM, wt.shape[1]), wt.dtype)
    return jnp.concatenate([dg, ckv, cq, kr, z], axis=0).astype(BF16)


def kernel(x, c, ctx, c_ctx, w_mod, b_mod, norm_w, w_in, w_out, att_q_norm, att_k_norm, hgrn_lb_logits, hgrn_o_norm,
           conv_w, conv_b, conv_ln_w, conv_ln_b, mla_q_norm, mla_kv_norm, mla_w_uq, mla_w_ukv, mla_qk_q_norm,
           mla_qk_k_norm):
    bsz, seq, d = x.shape
    ctx_len = ctx.shape[1]
    depth = w_mod.shape[0]
    t = ctx_len + seq
    assert ctx_len == ROW_TILE and seq % ROW_TILE == 0 and bsz <= 2
    n_tiles = t // ROW_TILE

    lb_all = jnp.cumsum(jax.nn.softmax(hgrn_lb_logits.astype(F32), axis=0), axis=0)
    lb_all = lb_all - lb_all[0:1]

    c8 = jnp.zeros((8, d), F32).at[0:bsz].set(c).at[2].set(c_ctx)
    mod_all = _modulation(c8, w_mod, b_mod)

    gqa_tabs = _rope_tables(seq, ctx_len, HEAD_DIM)
    mla_tabs = _rope_tables(seq, ctx_len, MLA_ROPE_DIM)

    w_in_t = jnp.swapaxes(w_in, 1, 2)

    xc, xl = ctx, x
    for l in range(depth):
        ctx_out = l < depth - 1
        mod = mod_all[l]
        h = _norm_modulate(xc, xl, mod, norm_w[l], n_tiles)
        h2 = h.reshape(bsz * t, d)
        p3 = _in_proj_abc(h2, w_in_t, l).reshape(bsz, t, ABC_WIDTH)
        pd3 = _matmul_nt(h2, _group_d_weight(w_in_t[l]), BF16).reshape(bsz, t, D_WIDTH)

        qa, ka, va = _gqa_prep(p3, gqa_tabs, att_q_norm[l], att_k_norm[l])
        ya = _attention(qa, ka, va, p3, A_G, N_HEADS // ATT_KV_HEADS, HEAD_DIM, True, ctx_out)

        yb = _hgrn(p3, 1.0 - lb_all[l], jnp.log(lb_all[l]), hgrn_o_norm[l], ctx_len)

        yc = _conv(p3, conv_w[l], conv_b[l], conv_ln_w[l], conv_ln_b[l], ctx_out)

        wq = jnp.pad(mla_w_uq[l].reshape(MLA_Q_RANK, N_HEADS, MLA_QK_DIM),
                     ((0, 0), (0, 0), (0, MLA_PAD_DIM - MLA_QK_DIM))).reshape(MLA_Q_RANK, -1).astype(BF16)
        wkv = mla_w_ukv[l].reshape(MLA_KV_RANK, N_HEADS, 2 * HEAD_DIM)
        wk = wkv[:, :, :HEAD_DIM].reshape(MLA_KV_RANK, -1).astype(BF16)
        wv = wkv[:, :, HEAD_DIM:].reshape(MLA_KV_RANK, -1).astype(BF16)
        qn = jnp.pad(mla_qk_q_norm[l], (0, MLA_PAD_DIM - MLA_QK_DIM))
        kn = jnp.pad(mla_qk_k_norm[l], (0, MLA_PAD_DIM - MLA_QK_DIM))
        qd, kd, vd = _mla_up(pd3, mla_tabs, mla_q_norm[l], mla_kv_norm[l], wq, wk, wv, qn, kn)
        yd = _attention(qd, kd, vd, pd3, D_G, 2, MLA_PAD_DIM, False, ctx_out)

        ys = (ya, yb, yc, yd)
        xl_new = _out_proj(ys, w_out, l, xl, mod, False)
        if ctx_out:
            xc = _out_proj(ys, w_out, l, xc, mod, True)
        xl = xl_new
    return xl
```

```python
import functools

import numpy as np
import jax
import jax.numpy as jnp
from jax import lax
from jax.experimental import pallas as pl
from jax.experimental.pallas import tpu as pltpu

F32 = jnp.float32
BF16 = jnp.bfloat16

EPS = 1e-6
ROPE_THETA = 10000.0
GRID_W = 64
HEAD_DIM = 128
GROUP_WIDTH = 1024
N_HEADS = 8
ATT_KV_HEADS = 2
MLA_Q_RANK = 768
MLA_KV_RANK = 512
MLA_ROPE_DIM = 64
MLA_QK_DIM = 192
MLA_PAD_DIM = 256
CONV_WIDTH = 31
CONV_HALO = 16
HGRN_CHUNK = 64
HGRN_LEVELS = 6
HGRN_HEADS_PER_STEP = 2
ROW_TILE = 256

IN_TILE = 512
C_U, C_GLU, C_G = 0, 1024, 2048
A_Q, A_K, A_V, A_G = 3072, 4096, 4352, 4608
B_Q, B_I, B_FF, B_FB, B_G = 5632, 6656, 7680, 8704, 9728
ABC_WIDTH = 10752
D_G, D_CKV, D_CQ, D_KR = 0, 1024, 1536, 2304
D_WIDTH = 2560

VMEM_LIMIT = 56 * 1024 * 1024


def _cparams(sem):
    return pltpu.CompilerParams(dimension_semantics=sem, vmem_limit_bytes=VMEM_LIMIT)


def _silu(x):
    return x * jax.nn.sigmoid(x)


def _dot(a, b):
    return jnp.dot(a, b, preferred_element_type=F32)


def _dot_nt(a, b):
    return lax.dot_general(a, b, (((1,), (1,)), ((), ())), preferred_element_type=F32)


def _mod_kernel(c_ref, w_ref, b_ref, o_ref):
    a = _silu(c_ref[...]).astype(BF16)
    o_ref[0] = _dot(a, w_ref[0].astype(BF16)) + b_ref[0]


def _modulation(c8, w_mod, b_mod):
    depth, d, n = w_mod.shape
    tn = 512
    return pl.pallas_call(
        _mod_kernel,
        grid=(depth, n // tn),
        in_specs=[pl.BlockSpec((8, d), lambda l, j: (0, 0)),
                  pl.BlockSpec((1, d, tn), lambda l, j: (l, 0, j)),
                  pl.BlockSpec((1, 1, tn), lambda l, j: (l, 0, j))],
        out_specs=pl.BlockSpec((1, 8, tn), lambda l, j: (l, 0, j)),
        out_shape=jax.ShapeDtypeStruct((depth, 8, n), F32),
        compiler_params=_cparams(("parallel", "parallel")),
        name="modulation",
    )(c8, w_mod, b_mod.reshape(depth, 1, n))


def _norm_kernel(xc_ref, xl_ref, mod_ref, nw_ref, o_ref, *, d, n_lat):
    b = pl.program_id(0)
    t = pl.program_id(1)

    def body(x, row):
        m = mod_ref[pl.ds(row, 1), :]
        shift = m[:, :d]
        scale = m[:, d:2 * d]
        ms = jnp.mean(x * x, axis=-1, keepdims=True)
        y = x * lax.rsqrt(ms + EPS) * nw_ref[...]
        o_ref[0] = (y * (1.0 + scale) + shift).astype(o_ref.dtype)

    @pl.when(t == n_lat)
    def _():
        body(xc_ref[0], 2)

    @pl.when(t < n_lat)
    def _():
        body(xl_ref[0], b)


def _norm_modulate(xc, xl, mod, nw, n_tiles):
    bsz, _, d = xc.shape
    n_lat = n_tiles - 1
    return pl.pallas_call(
        functools.partial(_norm_kernel, d=d, n_lat=n_lat),
        grid=(bsz, n_tiles),
        in_specs=[pl.BlockSpec((1, ROW_TILE, d), lambda b, t: (b, 0, 0)),
                  pl.BlockSpec((1, ROW_TILE, d), lambda b, t: (b, jnp.minimum(t, n_lat - 1), 0)),
                  pl.BlockSpec((8, 3 * d), lambda b, t: (0, 0)),
                  pl.BlockSpec((1, d), lambda b, t: (0, 0))],
        out_specs=pl.BlockSpec((1, ROW_TILE, d), lambda b, t: (b, t, 0)),
        out_shape=jax.ShapeDtypeStruct((bsz, n_tiles * ROW_TILE, d), BF16),
        compiler_params=_cparams(("parallel", "parallel")),
        name="norm_modulate",
    )(xc, xl, mod, nw.reshape(1, d))


def _mm_kernel(a_ref, w_ref, o_ref):
    o_ref[...] = _dot_nt(a_ref[...], w_ref[...]).astype(o_ref.dtype)


def _pick_tile(n, cands):
    for c in cands:
        if n % c == 0:
            return c
    raise ValueError(f"no tile for {n}")


def _matmul_nt(a, wt, out_dtype):
    m, k = a.shape
    n, _ = wt.shape
    tm = _pick_tile(m, (1088, 1024, 768, 512, 256))
    tn = _pick_tile(n, (512, 256, 128))
    return pl.pallas_call(
        _mm_kernel,
        grid=(m // tm, n // tn),
        in_specs=[pl.BlockSpec((tm, k), lambda i, j: (i, 0)),
                  pl.BlockSpec((tn, k), lambda i, j: (j, 0))],
        out_specs=pl.BlockSpec((tm, tn), lambda i, j: (i, j)),
        out_shape=jax.ShapeDtypeStruct((m, n), out_dtype),
        compiler_params=_cparams(("parallel", "parallel")),
        name="in_proj_d",
    )(a, wt)


def _mm_wcast_kernel(a_ref, w_ref, o_ref, wb_scr):
    @pl.when(pl.program_id(1) == 0)
    def _():
        wb_scr[...] = w_ref[0].astype(BF16)

    o_ref[...] = _dot_nt(a_ref[...], wb_scr[...]).astype(o_ref.dtype)


def _in_proj_abc(a, w_in_t, layer):
    m, k = a.shape
    tm = _pick_tile(m, (1088, 1024, 768, 512, 256))
    n_tiles = ABC_WIDTH // IN_TILE
    n_c = (C_G + GROUP_WIDTH) // IN_TILE
    return pl.pallas_call(
        _mm_wcast_kernel,
        grid=(n_tiles, m // tm),
        in_specs=[pl.BlockSpec((tm, k), lambda j, i: (i, 0)),
                  pl.BlockSpec((1, IN_TILE, k),
                               lambda j, i: (layer, jnp.where(j < n_c, j + n_tiles - n_c, j - n_c), 0))],
        out_specs=pl.BlockSpec((tm, IN_TILE), lambda j, i: (i, j)),
        out_shape=jax.ShapeDtypeStruct((m, ABC_WIDTH), BF16),
        scratch_shapes=[pltpu.VMEM((IN_TILE, k), BF16)],
        compiler_params=_cparams(("parallel", "arbitrary")),
        name="in_proj_abc",
    )(a, w_in_t)


def _rope_tables(seq, ctx_len, rope_dim):
    quarter = rope_dim // 4
    inv = ROPE_THETA ** (-jnp.arange(quarter, dtype=F32) / quarter)
    pos = jnp.arange(seq, dtype=jnp.int32)
    ar = (pos // GRID_W).astype(F32)[:, None] * inv
    ac = (pos % GRID_W).astype(F32)[:, None] * inv
    z = jnp.zeros_like(ar)
    cos = jnp.concatenate([jnp.cos(ar), jnp.cos(ar), jnp.cos(ac), jnp.cos(ac)], axis=-1)
    sa = jnp.concatenate([-jnp.sin(ar), z, -jnp.sin(ac), z], axis=-1)
    sb = jnp.concatenate([z, jnp.sin(ar), z, jnp.sin(ac)], axis=-1)

    def full(tab, fill):
        tab = jnp.pad(tab, ((0, 0), (0, 128 - rope_dim)), constant_values=fill)
        return jnp.concatenate([tab, jnp.full((ctx_len, 128), fill, F32)], axis=0)

    return full(cos, 1.0), full(sa, 0.0), full(sb, 0.0)


def _rope(x, cos, sa, sb, quarter):
    return x * cos + pltpu.roll(x, 128 - quarter, 1) * sa + pltpu.roll(x, quarter, 1) * sb


def _gqa_prep_kernel(q_ref, k_ref, v_ref, cos_ref, sa_ref, sb_ref, qn_ref, kn_ref, qo_ref, ko_ref, vo_ref):
    cos, sa, sb = cos_ref[...], sa_ref[...], sb_ref[...]

    def head(x, w, scale):
        ms = jnp.mean(x * x, axis=-1, keepdims=True)
        y = x * lax.rsqrt(ms + EPS) * w
        return (_rope(y, cos, sa, sb, HEAD_DIM // 4) * scale).astype(BF16)

    for h in range(N_HEADS):
        sl = slice(h * HEAD_DIM, (h + 1) * HEAD_DIM)
        qo_ref[0, :, sl] = head(q_ref[0, :, sl].astype(F32), qn_ref[...], HEAD_DIM ** -0.5)
    for h in range(ATT_KV_HEADS):
        sl = slice(h * HEAD_DIM, (h + 1) * HEAD_DIM)
        ko_ref[0, :, sl] = head(k_ref[0, :, sl].astype(F32), kn_ref[...], 1.0)
        _store_value_ext(vo_ref, h, v_ref[0, :, sl])


def _store_value_ext(vo_ref, h, v):
    vo_ref[0, :, 2 * h * HEAD_DIM:(2 * h + 1) * HEAD_DIM] = v
    vo_ref[0, :, (2 * h + 1) * HEAD_DIM:(2 * h + 2) * HEAD_DIM] = jnp.ones(v.shape, v.dtype)


def _gqa_prep(p3, tabs, qn, kn):
    bsz, t, _ = p3.shape
    nt = t // ROW_TILE
    kvw = ATT_KV_HEADS * HEAD_DIM
    tab_spec = pl.BlockSpec((ROW_TILE, 128), lambda b, i: (i, 0))
    vec_spec = pl.BlockSpec((1, 128), lambda b, i: (0, 0))
    return pl.pallas_call(
        _gqa_prep_kernel,
        grid=(bsz, nt),
        in_specs=[pl.BlockSpec((1, ROW_TILE, GROUP_WIDTH), lambda b, i: (b, i, A_Q // GROUP_WIDTH)),
                  pl.BlockSpec((1, ROW_TILE, kvw), lambda b, i: (b, i, A_K // kvw)),
                  pl.BlockSpec((1, ROW_TILE, kvw), lambda b, i: (b, i, A_V // kvw)),
                  tab_spec, tab_spec, tab_spec, vec_spec, vec_spec],
        out_specs=[pl.BlockSpec((1, ROW_TILE, GROUP_WIDTH), lambda b, i: (b, i, 0)),
                   pl.BlockSpec((1, ROW_TILE, kvw), lambda b, i: (b, i, 0)),
                   pl.BlockSpec((1, ROW_TILE, 2 * kvw), lambda b, i: (b, i, 0))],
        out_shape=[jax.ShapeDtypeStruct((bsz, t, GROUP_WIDTH), BF16),
                   jax.ShapeDtypeStruct((bsz, t, kvw), BF16),
                   jax.ShapeDtypeStruct((bsz, t, 2 * kvw), BF16)],
        compiler_params=_cparams(("parallel", "parallel")),
        name="gqa_prep",
    )(p3, p3, p3, *tabs, qn.reshape(1, 128), kn.reshape(1, 128))


def _attn_kernel(q_ref, k_ref, v_ref, g_ref, o_ref, *, n_heads, dq, shared_kv, n_lat):
    i = pl.program_id(2)
    vw = 2 * HEAD_DIM

    def attend(lo):
        def kv(j):
            if shared_kv:
                return k_ref[0, lo:, :], v_ref[0, lo:, :]
            return k_ref[0, lo:, j * dq:(j + 1) * dq], v_ref[0, lo:, j * vw:(j + 1) * vw]

        s = [None] * n_heads
        p = [None] * n_heads
        for j in range(n_heads + 2):
            if j < n_heads:
                s[j] = _dot_nt(q_ref[0, :, j * dq:(j + 1) * dq], kv(j)[0])
            if 1 <= j <= n_heads:
                m = jnp.max(s[j - 1], axis=-1, keepdims=True)
                p[j - 1] = jnp.exp((s[j - 1] - m).astype(BF16))
                s[j - 1] = None
            if j >= 2:
                oe = _dot(p[j - 2], kv(j - 2)[1])
                p[j - 2] = None
                o = oe[:, :HEAD_DIM] / oe[:, HEAD_DIM:]
                sl = slice((j - 2) * HEAD_DIM, (j - 1) * HEAD_DIM)
                o_ref[0, :, sl] = (o * _silu(g_ref[0, :, sl].astype(F32))).astype(o_ref.dtype)

    @pl.when(i < n_lat)
    def _():
        attend(0)

    @pl.when(i == n_lat)
    def _():
        attend(n_lat * ROW_TILE)


def _attention(q, k, v_ext, p3, gate_off, n_heads, dq, shared_kv, ctx_out):
    bsz, t, _ = q.shape
    n_lat = t // ROW_TILE - 1
    nt = n_lat + (1 if ctx_out else 0)
    n_kv = 1 if shared_kv else n_heads
    gw = n_heads * HEAD_DIM
    n_steps = q.shape[-1] // (n_heads * dq)
    return pl.pallas_call(
        functools.partial(_attn_kernel, n_heads=n_heads, dq=dq, shared_kv=shared_kv, n_lat=n_lat),
        grid=(bsz, n_steps, nt),
        in_specs=[pl.BlockSpec((1, ROW_TILE, n_heads * dq), lambda b, h, i: (b, i, h)),
                  pl.BlockSpec((1, t, n_kv * dq), lambda b, h, i: (b, 0, h)),
                  pl.BlockSpec((1, t, n_kv * 2 * HEAD_DIM), lambda b, h, i: (b, 0, h)),
                  pl.BlockSpec((1, ROW_TILE, gw), lambda b, h, i: (b, i, gate_off // gw + h))],
        out_specs=pl.BlockSpec((1, ROW_TILE, gw), lambda b, h, i: (b, i, h)),
        out_shape=jax.ShapeDtypeStruct((bsz, nt * ROW_TILE, n_steps * gw), BF16),
        compiler_params=_cparams(("parallel", "parallel", "parallel")),
        name="attention",
    )(q, k, v_ext, p3)


def _mla_up_kernel(cq_ref, ckv_ref, kr_ref, cos_ref, sa_ref, sb_ref, cqn_ref, ckvn_ref, wq_ref, wk_ref, wv_ref,
                   qn_ref, kn_ref, qo_ref, ko_ref, vo_ref):
    cos, sa, sb = cos_ref[...], sa_ref[...], sb_ref[...]
    quarter = MLA_ROPE_DIM // 4

    def rms(x, w):
        return x * lax.rsqrt(jnp.mean(x * x, axis=-1, keepdims=True) + EPS) * w

    cq = rms(cq_ref[0].astype(F32), cqn_ref[...]).astype(BF16)
    ckv = rms(ckv_ref[0].astype(F32), ckvn_ref[...]).astype(BF16)
    qf = _dot(cq, wq_ref[...])
    kf = _dot(ckv, wk_ref[...])
    vf = _dot(ckv, wv_ref[...]).astype(BF16)
    kr = kr_ref[0].astype(F32)
    kr_ss = jnp.sum(kr * kr, axis=-1, keepdims=True)
    qw1, qw2 = qn_ref[:, :128], qn_ref[:, 128:]
    kw1, kw2 = kn_ref[:, :128], kn_ref[:, 128:]
    scale = MLA_QK_DIM ** -0.5
    for h in range(N_HEADS):
        lo = h * MLA_PAD_DIM
        q1 = qf[:, lo:lo + 128]
        q2 = qf[:, lo + 128:lo + 256]
        ss = jnp.sum(q1 * q1 + q2 * q2, axis=-1, keepdims=True)
        r = lax.rsqrt(ss / MLA_QK_DIM + EPS) * scale
        qo_ref[0, :, lo:lo + 128] = (q1 * r * qw1).astype(BF16)
        qo_ref[0, :, lo + 128:lo + 256] = _rope(q2 * r * qw2, cos, sa, sb, quarter).astype(BF16)
        k1 = kf[:, h * 128:(h + 1) * 128]
        r = lax.rsqrt((jnp.sum(k1 * k1, axis=-1, keepdims=True) + kr_ss) / MLA_QK_DIM + EPS)
        ko_ref[0, :, lo:lo + 128] = (k1 * r * kw1).astype(BF16)
        ko_ref[0, :, lo + 128:lo + 256] = _rope(kr * r * kw2, cos, sa, sb, quarter).astype(BF16)
        _store_value_ext(vo_ref, h, vf[:, h * 128:(h + 1) * 128])


def _mla_up(p3, tabs, cqn, ckvn, wq, wk, wv, qn, kn):
    bsz, t, _ = p3.shape
    nt = t // ROW_TILE
    hw = N_HEADS * MLA_PAD_DIM
    tab_spec = pl.BlockSpec((ROW_TILE, 128), lambda b, i: (i, 0))

    def whole(a):
        return pl.BlockSpec(a.shape, lambda b, i: (0,) * a.ndim)

    args = (cqn.reshape(1, -1), ckvn.reshape(1, -1), wq, wk, wv, qn.reshape(1, -1), kn.reshape(1, -1))
    return pl.pallas_call(
        _mla_up_kernel,
        grid=(bsz, nt),
        in_specs=[pl.BlockSpec((1, ROW_TILE, MLA_Q_RANK), lambda b, i: (b, i, D_CQ // MLA_Q_RANK)),
                  pl.BlockSpec((1, ROW_TILE, MLA_KV_RANK), lambda b, i: (b, i, D_CKV // MLA_KV_RANK)),
                  pl.BlockSpec((1, ROW_TILE, 128), lambda b, i: (b, i, D_KR // 128)),
                  tab_spec, tab_spec, tab_spec] + [whole(a) for a in args],
        out_specs=[pl.BlockSpec((1, ROW_TILE, hw), lambda b, i: (b, i, 0)),
                   pl.BlockSpec((1, ROW_TILE, hw), lambda b, i: (b, i, 0)),
                   pl.BlockSpec((1, ROW_TILE, 2 * GROUP_WIDTH), lambda b, i: (b, i, 0))],
        out_shape=[jax.ShapeDtypeStruct((bsz, t, hw), BF16),
                   jax.ShapeDtypeStruct((bsz, t, hw), BF16),
                   jax.ShapeDtypeStruct((bsz, t, 2 * GROUP_WIDTH), BF16)],
        compiler_params=_cparams(("parallel", "parallel")),
        name="mla_up",
    )(p3, p3, p3, *tabs, *args)


def _hgrn_consts():
    c = HGRN_CHUNK
    t = np.arange(c)[:, None]
    s = np.arange(c)[None, :]
    cum = np.stack([s <= t, s >= t]).astype(np.float32)
    masks = np.zeros((2, HGRN_LEVELS + 1, c, c), np.float32)
    masks[:, 0] = np.eye(c)
    for lvl in range(1, HGRN_LEVELS + 1):
        m = c >> lvl
        same = (t // (2 * m)) == (s // (2 * m))
        t_up, s_up = (t // m) % 2 == 1, (s // m) % 2 == 1
        masks[0, lvl] = same & t_up & ~s_up
        masks[1, lvl] = same & ~t_up & s_up
    return jnp.asarray(cum, BF16), jnp.asarray(masks)


def _hgrn_kernel(q_ref, i_ref, ff_ref, fb_ref, gate_ref, oml_ref, llb_ref, on_ref, cum_ref, mask_ref, y_ref,
                 *scr, n_ctx_chunks, n_chunks):
    c = HGRN_CHUNK
    n_chain = 2 * HGRN_HEADS_PER_STEP
    g_scrs, o_scrs = scr[:n_chain], scr[n_chain:]
    sub = lax.broadcasted_iota(jnp.int32, (8, 128), 0)

    def level_ref(g_scr, d, lvl):
        m = c >> lvl
        blk = 2 * m
        def row(r):
            return g_scr[r:r + 1, :]
        idx = [b * blk + m - 1 + d for b in range(c // blk)]
        if blk >= 8:
            return jnp.concatenate([jnp.broadcast_to(row(r), (blk, 128)) for r in idx], axis=0)
        per = 8 // blk
        tiles = []
        for j in range(c // 8):
            tile = jnp.broadcast_to(row(idx[j * per + per - 1]), (8, 128))
            for u in range(per - 2, -1, -1):
                tile = jnp.where(sub < (u + 1) * blk, jnp.broadcast_to(row(idx[j * per + u]), (8, 128)), tile)
            tiles.append(tile)
        return jnp.concatenate(tiles, axis=0)

    chains = [(hh, d) for hh in range(HGRN_HEADS_PER_STEP) for d in range(2)]

    def chunk_rows(n):
        n_lat_chunks = n_chunks - n_ctx_chunks
        cf = jnp.where(n < n_ctx_chunks, n_lat_chunks + n, n - n_ctx_chunks)
        cb = n_chunks - 1 - n
        return [pl.ds(pl.multiple_of(cf * c, c), c), pl.ds(pl.multiple_of(cb * c, c), c)]

    def gate_decay_parts(n):
        rows_d = chunk_rows(n)
        out = []
        for hh, d in chains:
            cols = slice(hh * HEAD_DIM, (hh + 1) * HEAD_DIM)
            f = (ff_ref if d == 0 else fb_ref)[0, rows_d[d], cols].astype(F32)
            a = llb_ref[d:d + 1, cols] - f
            lf = (jnp.minimum(f, 0.0) - jnp.log(1.0 + jnp.exp(-jnp.abs(f)))
                  + jnp.maximum(a, 0.0) + jnp.log(1.0 + jnp.exp(-jnp.abs(a))))
            hi = lf.astype(BF16)
            r1 = lf - hi.astype(F32)
            mid = r1.astype(BF16)
            lo = (r1 - mid.astype(F32)).astype(BF16)
            cum = cum_ref[d]
            out.append((_dot(cum, hi), _dot(cum, mid), _dot(cum, lo)))
        return out

    def gate_qk(n):
        rows_d = chunk_rows(n)
        out = []
        for hh, d in chains:
            cols = slice(hh * HEAD_DIM, (hh + 1) * HEAD_DIM)
            qv = q_ref[0, rows_d[d], cols].astype(F32)
            qb = (_silu(qv) * (HEAD_DIM ** -0.5)).astype(BF16)
            f = (ff_ref if d == 0 else fb_ref)[0, rows_d[d], cols].astype(F32)
            kb = (oml_ref[d:d + 1, cols] / (1.0 + jnp.exp(f))).astype(BF16)
            out.append((qb, kb))
        return out

    def gate_join(qk, parts):
        return [(qb, kb, p0 + p1 + p2) for (qb, kb), (p0, p1, p2) in zip(qk, parts)]

    def body(n, carry):
        states, cur = carry
        n_next = jnp.minimum(n + 1, n_chunks - 1)
        rows_d = chunk_rows(n)
        vb, gtot, o, scores, o_intra, st_upd = {}, {}, {}, {}, {}, {}
        new = [None] * n_chain
        for ci, (hh, d) in enumerate(chains):
            g_scrs[ci][...] = cur[ci][2]

        def scores_of(group):
            for ci in group:
                hh, d = chains[ci]
                qb, kb, g = cur[ci]
                vb[ci] = i_ref[0, rows_d[d], hh * HEAD_DIM:(hh + 1) * HEAD_DIM]
                end = (c - 1) * (1 - d)
                gtot[ci] = g_scrs[ci][end:end + 1, :]
                o[ci] = _dot_nt(qb * jnp.exp(g).astype(BF16), states[ci].astype(BF16))
                scores[ci] = jnp.where(mask_ref[d, 0] > 0.5, _dot_nt(qb, kb), 0.0)
            for lvl in range(1, HGRN_LEVELS + 1):
                for ci in group:
                    d = chains[ci][1]
                    qb, kb, g = cur[ci]
                    e = jnp.exp(-jnp.abs(g - level_ref(g_scrs[ci], d, lvl))).astype(BF16)
                    s_l = _dot_nt(qb * e, kb * e)
                    scores[ci] = jnp.where(mask_ref[d, lvl] > 0.5, s_l, scores[ci])

        def issue_last(group):
            for ci in group:
                qb, kb, g = cur[ci]
                o_intra[ci] = _dot(scores[ci].astype(BF16), vb[ci])
                kend = kb * jnp.exp(gtot[ci] - g).astype(BF16)
                vt = vb[ci].astype(F32).T.astype(BF16)
                st_upd[ci] = _dot(vt, kend)

        def finish(group):
            for ci in group:
                o_scrs[ci][rows_d[chains[ci][1]], :] = o[ci] + o_intra[ci]
                new[ci] = states[ci] * jnp.exp(gtot[ci]) + st_upd[ci]

        every = range(n_chain)
        scores_of(every)
        nxt_parts = gate_decay_parts(n_next)
        issue_last(every)
        nxt_qk = gate_qk(n_next)
        finish(every)
        return tuple(new), gate_join(nxt_qk, nxt_parts)

    zero = jnp.zeros((HEAD_DIM, HEAD_DIM), F32)
    lax.fori_loop(0, n_chunks, body, ((zero,) * n_chain, gate_join(gate_qk(0), gate_decay_parts(0))))

    n_row_tiles = (n_chunks * c) // ROW_TILE

    def readout(i, carry):
        rows = pl.ds(pl.multiple_of(i * ROW_TILE, ROW_TILE), ROW_TILE)
        for hh in range(HGRN_HEADS_PER_STEP):
            cols = slice(hh * HEAD_DIM, (hh + 1) * HEAD_DIM)
            o = o_scrs[2 * hh][rows, :] + o_scrs[2 * hh + 1][rows, :]
            o = o * lax.rsqrt(jnp.mean(o * o, axis=-1, keepdims=True) + EPS) * on_ref[:, cols]
            y_ref[0, rows, cols] = (o * _silu(gate_ref[0, rows, cols].astype(F32))).astype(y_ref.dtype)
        return carry

    lax.fori_loop(0, n_row_tiles, readout, 0)


def _hgrn(p3, one_minus_lb, log_lb, o_norm, ctx_len):
    bsz, t, _ = p3.shape
    cum, masks = _hgrn_consts()
    bw = HGRN_HEADS_PER_STEP * HEAD_DIM
    n_chain = 2 * HGRN_HEADS_PER_STEP

    def col(off):
        return pl.BlockSpec((1, t, bw), lambda b, h: (b, 0, off // bw + h))

    return pl.pallas_call(
        functools.partial(_hgrn_kernel, n_ctx_chunks=ctx_len // HGRN_CHUNK, n_chunks=t // HGRN_CHUNK),
        grid=(bsz, GROUP_WIDTH // bw),
        in_specs=[col(B_Q), col(B_I), col(B_FF), col(B_FB), col(B_G),
                  pl.BlockSpec((2, bw), lambda b, h: (0, h)),
                  pl.BlockSpec((2, bw), lambda b, h: (0, h)),
                  pl.BlockSpec((1, bw), lambda b, h: (0, h)),
                  pl.BlockSpec(cum.shape, lambda b, h: (0, 0, 0)),
                  pl.BlockSpec(masks.shape, lambda b, h: (0, 0, 0, 0))],
        out_specs=pl.BlockSpec((1, t, bw), lambda b, h: (b, 0, h)),
        out_shape=jax.ShapeDtypeStruct((bsz, t, GROUP_WIDTH), BF16),
        scratch_shapes=([pltpu.VMEM((HGRN_CHUNK, HEAD_DIM), F32)] * n_chain
                        + [pltpu.VMEM((t, HEAD_DIM), F32)] * n_chain),
        compiler_params=_cparams(("parallel", "parallel")),
        name="hgrn2",
    )(p3, p3, p3, p3, p3, one_minus_lb, log_lb, o_norm.reshape(1, -1), cum, masks)


def _conv_kernel(u_ref, up_ref, un_ref, gl_ref, glp_ref, gln_ref, gate_ref, w_ref, cb_ref, lw_ref, lb_ref, y_ref,
                 xs_scr, acc_scr, *, n_lat):
    t = pl.program_id(1)
    h = CONV_HALO
    prev_ok = jnp.logical_and(t >= 1, t < n_lat).astype(F32)
    next_ok = (t < n_lat - 1).astype(F32)
    n_lane_blocks = GROUP_WIDTH // 128
    x_prev = up_ref[0].astype(F32) * jax.nn.sigmoid(glp_ref[0].astype(F32)) * prev_ok
    x_main = u_ref[0].astype(F32) * jax.nn.sigmoid(gl_ref[0].astype(F32))
    x_next = un_ref[0].astype(F32) * jax.nn.sigmoid(gln_ref[0].astype(F32)) * next_ok
    for cb in range(n_lane_blocks):
        sl = slice(cb * 128, (cb + 1) * 128)
        xs_scr[cb, 0:h, :] = x_prev[:, sl]
        xs_scr[cb, h:h + ROW_TILE, :] = x_main[:, sl]
        xs_scr[cb, h + ROW_TILE:, :] = x_next[:, sl]
    rc = 64
    base = h - CONV_WIDTH // 2

    def lane_block(cb, carry):
        for r in range(ROW_TILE // rc):
            acc = jnp.zeros((rc, 128), F32)
            for tap in range(CONV_WIDTH):
                lo = base + r * rc + tap
                acc = acc + xs_scr[cb, lo:lo + rc, :] * w_ref[cb, tap:tap + 1, :]
            acc_scr[cb, r * rc:(r + 1) * rc, :] = acc
        return carry

    lax.fori_loop(0, n_lane_blocks, lane_block, 0)
    y = jnp.concatenate([acc_scr[cb] for cb in range(n_lane_blocks)], axis=1) + cb_ref[...]
    mu = jnp.mean(y, axis=-1, keepdims=True)
    yc = y - mu
    var = jnp.mean(yc * yc, axis=-1, keepdims=True)
    z = yc * lax.rsqrt(var + EPS) * lw_ref[...] + lb_ref[...]
    y_ref[0] = (_silu(z) * _silu(gate_ref[0].astype(F32))).astype(y_ref.dtype)


def _conv(p3, conv_w, conv_b, ln_w, ln_b, ctx_out):
    bsz, t, _ = p3.shape
    n_lat = t // ROW_TILE - 1
    nt = n_lat + (1 if ctx_out else 0)
    hpt = ROW_TILE // CONV_HALO
    n_halo = t // CONV_HALO
    n_lane_blocks = GROUP_WIDTH // 128
    wpad = jnp.pad(conv_w, ((0, 32 - CONV_WIDTH), (0, 0))).reshape(32, n_lane_blocks, 128).transpose(1, 0, 2)

    def main(off):
        return pl.BlockSpec((1, ROW_TILE, GROUP_WIDTH), lambda b, i: (b, i, off // GROUP_WIDTH))

    def prev(off):
        return pl.BlockSpec((1, CONV_HALO, GROUP_WIDTH),
                            lambda b, i: (b, jnp.maximum(i * hpt - 1, 0), off // GROUP_WIDTH))

    def nxt(off):
        return pl.BlockSpec((1, CONV_HALO, GROUP_WIDTH),
                            lambda b, i: (b, jnp.minimum((i + 1) * hpt, n_halo - 1), off // GROUP_WIDTH))

    vec = pl.BlockSpec((1, GROUP_WIDTH), lambda b, i: (0, 0))
    return pl.pallas_call(
        functools.partial(_conv_kernel, n_lat=n_lat),
        grid=(bsz, nt),
        in_specs=[main(C_U), prev(C_U), nxt(C_U), main(C_GLU), prev(C_GLU), nxt(C_GLU), main(C_G),
                  pl.BlockSpec((n_lane_blocks, 32, 128), lambda b, i: (0, 0, 0)), vec, vec, vec],
        out_specs=pl.BlockSpec((1, ROW_TILE, GROUP_WIDTH), lambda b, i: (b, i, 0)),
        out_shape=jax.ShapeDtypeStruct((bsz, nt * ROW_TILE, GROUP_WIDTH), BF16),
        scratch_shapes=[pltpu.VMEM((n_lane_blocks, ROW_TILE + 2 * CONV_HALO, 128), F32),
                        pltpu.VMEM((n_lane_blocks, ROW_TILE, 128), F32)],
        compiler_params=_cparams(("parallel", "parallel")),
        name="conformer_conv",
    )(p3, p3, p3, p3, p3, p3, p3, wpad, conv_b.reshape(1, -1), ln_w.reshape(1, -1), ln_b.reshape(1, -1))


def _out_kernel(ya_ref, yb_ref, yc_ref, yd_ref, w_ref, x_ref, gate_ref, o_ref, wb_scr, *, ctx_rows):
    b = pl.program_id(1)
    gw = GROUP_WIDTH

    @pl.when(jnp.logical_and(b == 0, pl.program_id(2) == 0))
    def _():
        wb_scr[...] = w_ref[0].astype(BF16)

    acc = _dot(ya_ref[0], wb_scr[0:gw, :])
    acc = acc + _dot(yb_ref[0], wb_scr[gw:2 * gw, :])
    acc = acc + _dot(yc_ref[0], wb_scr[2 * gw:3 * gw, :])
    acc = acc + _dot(yd_ref[0], wb_scr[3 * gw:4 * gw, :])

    row = 2 if ctx_rows else b
    o_ref[0] = x_ref[0] + gate_ref[pl.ds(row, 1), :] * acc


def _out_proj(ys, w_out, layer, x, mod, ctx_rows):
    bsz, rows, d = x.shape
    tm = ROW_TILE if ctx_rows else _pick_tile(rows, (1024, 512, 256))
    y_off = (ys[0].shape[1] - rows) // tm if ctx_rows else 0
    tn = 512
    y_spec = pl.BlockSpec((1, tm, GROUP_WIDTH), lambda j, b, i: (b, i + y_off, 0))
    return pl.pallas_call(
        functools.partial(_out_kernel, ctx_rows=ctx_rows),
        grid=(d // tn, bsz, rows // tm),
        in_specs=[y_spec, y_spec, y_spec, y_spec,
                  pl.BlockSpec((1, 4 * GROUP_WIDTH, tn), lambda j, b, i: (layer, 0, j)),
                  pl.BlockSpec((1, tm, tn), lambda j, b, i: (b, i, j)),
                  pl.BlockSpec((8, tn), lambda j, b, i: (0, 2 * d // tn + j))],
        out_specs=pl.BlockSpec((1, tm, tn), lambda j, b, i: (b, i, j)),
        out_shape=jax.ShapeDtypeStruct((bsz, rows, d), F32),
        scratch_shapes=[pltpu.VMEM((4 * GROUP_WIDTH, tn), BF16)],
        compiler_params=_cparams(("parallel", "arbitrary", "arbitrary")),
        name="out_proj",
    )(*ys, w_out, x, mod)


D_PACK_ROWS = 64


def _pack_d_kernel(w_ref, o_ref, *, n_src_blocks):
    @pl.when(pl.program_id(0) < n_src_blocks)
    def _():
        o_ref[...] = w_ref[0].astype(BF16)

    @pl.when(pl.program_id(0) >= n_src_blocks)
    def _():
        o_ref[...] = jnp.zeros(o_ref.shape, o_ref.dtype)


def _group_d_weight(w_in_t, layer):
    k = w_in_t.shape[2]
    r = D_PACK_ROWS
    base = ABC_WIDTH // r
    src_cq, src_ckv, src_kr, src_g = base, base + 768 // r, base + 1280 // r, base + 1344 // r
    n_src_blocks = (D_KR + MLA_ROPE_DIM) // r

    def src(j):
        return jnp.where(j < D_CKV // r, src_g + j,
                         jnp.where(j < D_CQ // r, src_ckv + j - D_CKV // r,
                                   jnp.where(j < D_KR // r, src_cq + j - D_CQ // r, src_kr)))

    return pl.pallas_call(
        functools.partial(_pack_d_kernel, n_src_blocks=n_src_blocks),
        grid=(D_WIDTH // r,),
        in_specs=[pl.BlockSpec((1, r, k), lambda j: (layer, src(j), 0))],
        out_specs=pl.BlockSpec((r, k), lambda j: (j, 0)),
        out_shape=jax.ShapeDtypeStruct((D_WIDTH, k), BF16),
        compiler_params=_cparams(("parallel",)),
        name="pack_group_d",
    )(w_in_t)


def kernel(x, c, ctx, c_ctx, w_mod, b_mod, norm_w, w_in, w_out, att_q_norm, att_k_norm, hgrn_lb_logits, hgrn_o_norm,
           conv_w, conv_b, conv_ln_w, conv_ln_b, mla_q_norm, mla_kv_norm, mla_w_uq, mla_w_ukv, mla_qk_q_norm,
           mla_qk_k_norm):
    bsz, seq, d = x.shape
    ctx_len = ctx.shape[1]
    depth = w_mod.shape[0]
    t = ctx_len + seq
    assert ctx_len == ROW_TILE and seq % ROW_TILE == 0 and bsz <= 2
    n_tiles = t // ROW_TILE

    lb_all = jnp.cumsum(jax.nn.softmax(hgrn_lb_logits.astype(F32), axis=0), axis=0)
    lb_all = lb_all - lb_all[0:1]

    c8 = jnp.zeros((8, d), F32).at[0:bsz].set(c).at[2].set(c_ctx)
    mod_all = _modulation(c8, w_mod, b_mod)

    gqa_tabs = _rope_tables(seq, ctx_len, HEAD_DIM)
    mla_tabs = _rope_tables(seq, ctx_len, MLA_ROPE_DIM)

    w_in_t = jnp.swapaxes(w_in, 1, 2)

    xc, xl = ctx, x
    for l in range(depth):
        ctx_out = l < depth - 1
        mod = mod_all[l]
        h = _norm_modulate(xc, xl, mod, norm_w[l], n_tiles)
        h2 = h.reshape(bsz * t, d)
        p3 = _in_proj_abc(h2, w_in_t, l).reshape(bsz, t, ABC_WIDTH)
        pd3 = _matmul_nt(h2, _group_d_weight(w_in_t, l), BF16).reshape(bsz, t, D_WIDTH)

        qa, ka, va = _gqa_prep(p3, gqa_tabs, att_q_norm[l], att_k_norm[l])
        ya = _attention(qa, ka, va, p3, A_G, N_HEADS // ATT_KV_HEADS, HEAD_DIM, True, ctx_out)

        yb = _hgrn(p3, 1.0 - lb_all[l], jnp.log(lb_all[l]), hgrn_o_norm[l], ctx_len)

        yc = _conv(p3, conv_w[l], conv_b[l], conv_ln_w[l], conv_ln_b[l], ctx_out)

        wq = jnp.pad(mla_w_uq[l].reshape(MLA_Q_RANK, N_HEADS, MLA_QK_DIM),
                     ((0, 0), (0, 0), (0, MLA_PAD_DIM - MLA_QK_DIM))).reshape(MLA_Q_RANK, -1).astype(BF16)
        wkv = mla_w_ukv[l].reshape(MLA_KV_RANK, N_HEADS, 2 * HEAD_DIM)
        wk = wkv[:, :, :HEAD_DIM].reshape(MLA_KV_RANK, -1).astype(BF16)
        wv = wkv[:, :, HEAD_DIM:].reshape(MLA_KV_RANK, -1).astype(BF16)
        qn = jnp.pad(mla_qk_q_norm[l], (0, MLA_PAD_DIM - MLA_QK_DIM))
        kn = jnp.pad(mla_qk_k_norm[l], (0, MLA_PAD_DIM - MLA_QK_DIM))
        qd, kd, vd = _mla_up(pd3, mla_tabs, mla_q_norm[l], mla_kv_norm[l], wq, wk, wv, qn, kn)
        yd = _attention(qd, kd, vd, pd3, D_G, 4, MLA_PAD_DIM, False, ctx_out)

        ys = (ya, yb, yc, yd)
        xl_new = _out_proj(ys, w_out, l, xl, mod, False)
        if ctx_out:
            xc = _out_proj(ys, w_out, l, xc, mod, True)
        xl = xl_new
    return xl
```

```python
import functools

import numpy as np
import jax
import jax.numpy as jnp
from jax import lax
from jax.experimental import pallas as pl
from jax.experimental.pallas import tpu as pltpu

F32 = jnp.float32
BF16 = jnp.bfloat16

EPS = 1e-6
ROPE_THETA = 10000.0
GRID_W = 64
HEAD_DIM = 128
GROUP_WIDTH = 1024
N_HEADS = 8
ATT_KV_HEADS = 2
MLA_Q_RANK = 768
MLA_KV_RANK = 512
MLA_ROPE_DIM = 64
MLA_QK_DIM = 192
MLA_PAD_DIM = 256
CONV_WIDTH = 31
CONV_HALO = 16
HGRN_CHUNK = 64
HGRN_LEVELS = 6
HGRN_HEADS_PER_STEP = 2
ROW_TILE = 256
NORM_SUB_ROWS = 16

IN_TILE = 512
C_U, C_GLU, C_G = 0, 1024, 2048
A_Q, A_K, A_V, A_G = 3072, 4096, 4352, 4608
B_Q, B_I, B_FF, B_FB, B_G = 5632, 6656, 7680, 8704, 9728
ABC_WIDTH = 10752
D_G, D_CKV, D_CQ, D_KR = 0, 1024, 1536, 2304
D_WIDTH = 2560

VMEM_LIMIT = 56 * 1024 * 1024


def _cparams(sem):
    return pltpu.CompilerParams(dimension_semantics=sem, vmem_limit_bytes=VMEM_LIMIT)


def _silu(x):
    return x * jax.nn.sigmoid(x)


def _dot(a, b):
    return jnp.dot(a, b, preferred_element_type=F32)


def _dot_nt(a, b):
    return lax.dot_general(a, b, (((1,), (1,)), ((), ())), preferred_element_type=F32)


def _mod_kernel(c_ref, w_ref, b_ref, o_ref):
    a = _silu(c_ref[...]).astype(BF16)
    o_ref[0] = _dot(a, w_ref[0].astype(BF16)) + b_ref[0]


def _modulation(c8, w_mod, b_mod):
    depth, d, n = w_mod.shape
    tn = 512
    return pl.pallas_call(
        _mod_kernel,
        grid=(depth, n // tn),
        in_specs=[pl.BlockSpec((8, d), lambda l, j: (0, 0)),
                  pl.BlockSpec((1, d, tn), lambda l, j: (l, 0, j)),
                  pl.BlockSpec((1, 1, tn), lambda l, j: (l, 0, j))],
        out_specs=pl.BlockSpec((1, 8, tn), lambda l, j: (l, 0, j)),
        out_shape=jax.ShapeDtypeStruct((depth, 8, n), F32),
        compiler_params=_cparams(("parallel", "parallel")),
        name="modulation",
    )(c8, w_mod, b_mod.reshape(depth, 1, n))


def _norm_kernel(xc_ref, xl_ref, mod_ref, nw_ref, o_ref, *, d, n_lat):
    b = pl.program_id(0)
    t = pl.program_id(1)

    def body(x_ref, row):
        m = mod_ref[pl.ds(row, 1), :]
        shift = m[:, :d]
        gain = nw_ref[...] * (1.0 + m[:, d:2 * d])

        def sub_tile(r, carry):
            rows = pl.ds(pl.multiple_of(r * NORM_SUB_ROWS, NORM_SUB_ROWS), NORM_SUB_ROWS)
            x = x_ref[0, rows, :]
            ms = jnp.mean(x * x, axis=-1, keepdims=True)
            o_ref[0, rows, :] = (x * lax.rsqrt(ms + EPS) * gain + shift).astype(o_ref.dtype)
            return carry

        lax.fori_loop(0, ROW_TILE // NORM_SUB_ROWS, sub_tile, 0, unroll=4)

    @pl.when(t == n_lat)
    def _():
        body(xc_ref, 2)

    @pl.when(t < n_lat)
    def _():
        body(xl_ref, b)


def _norm_modulate(xc, xl, mod, nw, n_tiles):
    bsz, _, d = xc.shape
    n_lat = n_tiles - 1
    return pl.pallas_call(
        functools.partial(_norm_kernel, d=d, n_lat=n_lat),
        grid=(bsz, n_tiles),
        in_specs=[pl.BlockSpec((1, ROW_TILE, d), lambda b, t: (b, 0, 0)),
                  pl.BlockSpec((1, ROW_TILE, d), lambda b, t: (b, jnp.minimum(t, n_lat - 1), 0)),
                  pl.BlockSpec((8, 3 * d), lambda b, t: (0, 0)),
                  pl.BlockSpec((1, d), lambda b, t: (0, 0))],
        out_specs=pl.BlockSpec((1, ROW_TILE, d), lambda b, t: (b, t, 0)),
        out_shape=jax.ShapeDtypeStruct((bsz, n_tiles * ROW_TILE, d), BF16),
        compiler_params=_cparams(("parallel", "parallel")),
        name="norm_modulate",
    )(xc, xl, mod, nw.reshape(1, d))


def _mm_kernel(a_ref, w_ref, o_ref):
    o_ref[...] = _dot_nt(a_ref[...], w_ref[...]).astype(o_ref.dtype)


def _pick_tile(n, cands):
    for c in cands:
        if n % c == 0:
            return c
    raise ValueError(f"no tile for {n}")


def _matmul_nt(a, wt, out_dtype):
    m, k = a.shape
    n, _ = wt.shape
    tm = _pick_tile(m, (1088, 1024, 768, 512, 256))
    tn = _pick_tile(n, (512, 256, 128))
    return pl.pallas_call(
        _mm_kernel,
        grid=(m // tm, n // tn),
        in_specs=[pl.BlockSpec((tm, k), lambda i, j: (i, 0)),
                  pl.BlockSpec((tn, k), lambda i, j: (j, 0))],
        out_specs=pl.BlockSpec((tm, tn), lambda i, j: (i, j)),
        out_shape=jax.ShapeDtypeStruct((m, n), out_dtype),
        compiler_params=_cparams(("parallel", "parallel")),
        name="in_proj_d",
    )(a, wt)


def _mm_wcast_kernel(a_ref, w_ref, o_ref, wb_scr):
    @pl.when(pl.program_id(1) == 0)
    def _():
        wb_scr[...] = w_ref[0].astype(BF16)

    o_ref[...] = _dot_nt(a_ref[...], wb_scr[...]).astype(o_ref.dtype)


def _in_proj_abc(a, w_in_t, layer):
    m, k = a.shape
    tm = _pick_tile(m, (1088, 1024, 768, 512, 256))
    n_tiles = ABC_WIDTH // IN_TILE
    n_c = (C_G + GROUP_WIDTH) // IN_TILE
    return pl.pallas_call(
        _mm_wcast_kernel,
        grid=(n_tiles, m // tm),
        in_specs=[pl.BlockSpec((tm, k), lambda j, i: (i, 0)),
                  pl.BlockSpec((1, IN_TILE, k),
                               lambda j, i: (layer, jnp.where(j < n_c, j + n_tiles - n_c, j - n_c), 0))],
        out_specs=pl.BlockSpec((tm, IN_TILE), lambda j, i: (i, j)),
        out_shape=jax.ShapeDtypeStruct((m, ABC_WIDTH), BF16),
        scratch_shapes=[pltpu.VMEM((IN_TILE, k), BF16)],
        compiler_params=_cparams(("parallel", "arbitrary")),
        name="in_proj_abc",
    )(a, w_in_t)


def _rope_tables(seq, ctx_len, rope_dim):
    quarter = rope_dim // 4
    inv = ROPE_THETA ** (-jnp.arange(quarter, dtype=F32) / quarter)
    pos = jnp.arange(seq, dtype=jnp.int32)
    ar = (pos // GRID_W).astype(F32)[:, None] * inv
    ac = (pos % GRID_W).astype(F32)[:, None] * inv
    cos = jnp.concatenate([jnp.cos(ar), jnp.cos(ar), jnp.cos(ac), jnp.cos(ac)], axis=-1)
    sin = jnp.concatenate([jnp.sin(ar), jnp.sin(ar), jnp.sin(ac), jnp.sin(ac)], axis=-1)

    def full(tab, fill):
        tab = jnp.pad(tab, ((0, 0), (0, 128 - rope_dim)), constant_values=fill)
        return jnp.concatenate([tab, jnp.full((ctx_len, 128), fill, F32)], axis=0)

    swap = np.zeros((128, 128), np.float32)
    for i in range(rope_dim):
        if (i // quarter) % 2 == 0:
            swap[i + quarter, i] = -1.0
        else:
            swap[i - quarter, i] = 1.0
    return full(cos, 1.0), full(sin, 0.0), jnp.asarray(swap, BF16)


def _rope(x, cos, sin, swap):
    return x * cos + _dot(x.astype(BF16), swap) * sin


def _gqa_prep_kernel(q_ref, k_ref, v_ref, cos_ref, sin_ref, swap_ref, qn_ref, kn_ref, qo_ref, ko_ref, vo_ref):
    cos, sin, swap = cos_ref[...], sin_ref[...], swap_ref[...]

    def head(x, w, scale):
        ms = jnp.mean(x * x, axis=-1, keepdims=True)
        y = x * lax.rsqrt(ms + EPS) * w
        return (_rope(y, cos, sin, swap) * scale).astype(BF16)

    for h in range(N_HEADS):
        sl = slice(h * HEAD_DIM, (h + 1) * HEAD_DIM)
        qo_ref[0, :, sl] = head(q_ref[0, :, sl].astype(F32), qn_ref[...], HEAD_DIM ** -0.5)
    for h in range(ATT_KV_HEADS):
        sl = slice(h * HEAD_DIM, (h + 1) * HEAD_DIM)
        ko_ref[0, :, sl] = head(k_ref[0, :, sl].astype(F32), kn_ref[...], 1.0)
        _store_value_ext(vo_ref, h, v_ref[0, :, sl])


def _store_value_ext(vo_ref, h, v):
    vo_ref[0, :, 2 * h * HEAD_DIM:(2 * h + 1) * HEAD_DIM] = v
    vo_ref[0, :, (2 * h + 1) * HEAD_DIM:(2 * h + 2) * HEAD_DIM] = jnp.ones(v.shape, v.dtype)


def _gqa_prep(p3, tabs, qn, kn):
    bsz, t, _ = p3.shape
    nt = t // ROW_TILE
    kvw = ATT_KV_HEADS * HEAD_DIM
    tab_spec = pl.BlockSpec((ROW_TILE, 128), lambda b, i: (i, 0))
    swap_spec = pl.BlockSpec((128, 128), lambda b, i: (0, 0))
    vec_spec = pl.BlockSpec((1, 128), lambda b, i: (0, 0))
    return pl.pallas_call(
        _gqa_prep_kernel,
        grid=(bsz, nt),
        in_specs=[pl.BlockSpec((1, ROW_TILE, GROUP_WIDTH), lambda b, i: (b, i, A_Q // GROUP_WIDTH)),
                  pl.BlockSpec((1, ROW_TILE, kvw), lambda b, i: (b, i, A_K // kvw)),
                  pl.BlockSpec((1, ROW_TILE, kvw), lambda b, i: (b, i, A_V // kvw)),
                  tab_spec, tab_spec, swap_spec, vec_spec, vec_spec],
        out_specs=[pl.BlockSpec((1, ROW_TILE, GROUP_WIDTH), lambda b, i: (b, i, 0)),
                   pl.BlockSpec((1, ROW_TILE, kvw), lambda b, i: (b, i, 0)),
                   pl.BlockSpec((1, ROW_TILE, 2 * kvw), lambda b, i: (b, i, 0))],
        out_shape=[jax.ShapeDtypeStruct((bsz, t, GROUP_WIDTH), BF16),
                   jax.ShapeDtypeStruct((bsz, t, kvw), BF16),
                   jax.ShapeDtypeStruct((bsz, t, 2 * kvw), BF16)],
        compiler_params=_cparams(("parallel", "parallel")),
        name="gqa_prep",
    )(p3, p3, p3, *tabs, qn.reshape(1, 128), kn.reshape(1, 128))


def _attn_kernel(q_ref, k_ref, v_ref, g_ref, o_ref, *, n_heads, dq, shared_kv, n_lat):
    i = pl.program_id(2)
    vw = 2 * HEAD_DIM

    def attend(lo):
        def kv(j):
            if shared_kv:
                return k_ref[0, lo:, :], v_ref[0, lo:, :]
            return k_ref[0, lo:, j * dq:(j + 1) * dq], v_ref[0, lo:, j * vw:(j + 1) * vw]

        s = [None] * n_heads
        p = [None] * n_heads
        for j in range(n_heads + 2):
            if j < n_heads:
                s[j] = _dot_nt(q_ref[0, :, j * dq:(j + 1) * dq], kv(j)[0])
            if 1 <= j <= n_heads:
                m = jnp.max(s[j - 1], axis=-1, keepdims=True)
                p[j - 1] = jnp.exp((s[j - 1] - m).astype(BF16))
                s[j - 1] = None
            if j >= 2:
                oe = _dot(p[j - 2], kv(j - 2)[1])
                p[j - 2] = None
                o = oe[:, :HEAD_DIM] / oe[:, HEAD_DIM:]
                sl = slice((j - 2) * HEAD_DIM, (j - 1) * HEAD_DIM)
                o_ref[0, :, sl] = (o * _silu(g_ref[0, :, sl].astype(F32))).astype(o_ref.dtype)

    @pl.when(i < n_lat)
    def _():
        attend(0)

    @pl.when(i == n_lat)
    def _():
        attend(n_lat * ROW_TILE)


def _attention(q, k, v_ext, p3, gate_off, n_heads, dq, shared_kv, ctx_out):
    bsz, t, _ = q.shape
    n_lat = t // ROW_TILE - 1
    nt = n_lat + (1 if ctx_out else 0)
    n_kv = 1 if shared_kv else n_heads
    gw = n_heads * HEAD_DIM
    n_steps = q.shape[-1] // (n_heads * dq)
    return pl.pallas_call(
        functools.partial(_attn_kernel, n_heads=n_heads, dq=dq, shared_kv=shared_kv, n_lat=n_lat),
        grid=(bsz, n_steps, nt),
        in_specs=[pl.BlockSpec((1, ROW_TILE, n_heads * dq), lambda b, h, i: (b, i, h)),
                  pl.BlockSpec((1, t, n_kv * dq), lambda b, h, i: (b, 0, h)),
                  pl.BlockSpec((1, t, n_kv * 2 * HEAD_DIM), lambda b, h, i: (b, 0, h)),
                  pl.BlockSpec((1, ROW_TILE, gw), lambda b, h, i: (b, i, gate_off // gw + h))],
        out_specs=pl.BlockSpec((1, ROW_TILE, gw), lambda b, h, i: (b, i, h)),
        out_shape=jax.ShapeDtypeStruct((bsz, nt * ROW_TILE, n_steps * gw), BF16),
        compiler_params=_cparams(("parallel", "parallel", "parallel")),
        name="attention",
    )(q, k, v_ext, p3)


def _mla_up_kernel(cq_ref, ckv_ref, kr_ref, cos_ref, sin_ref, swap_ref, cqn_ref, ckvn_ref, wq_ref, wk_ref, wv_ref,
                   qn_ref, kn_ref, qo_ref, ko_ref, vo_ref):
    cos, sin, swap = cos_ref[...], sin_ref[...], swap_ref[...]

    def rms(x, w):
        return x * lax.rsqrt(jnp.mean(x * x, axis=-1, keepdims=True) + EPS) * w

    cq = rms(cq_ref[0].astype(F32), cqn_ref[...]).astype(BF16)
    ckv = rms(ckv_ref[0].astype(F32), ckvn_ref[...]).astype(BF16)
    qf = _dot(cq, wq_ref[...])
    kf = _dot(ckv, wk_ref[...])
    vf = _dot(ckv, wv_ref[...]).astype(BF16)
    kr = kr_ref[0].astype(F32)
    kr_ss = jnp.sum(kr * kr, axis=-1, keepdims=True)
    qw1, qw2 = qn_ref[:, :128], qn_ref[:, 128:]
    kw1, kw2 = kn_ref[:, :128], kn_ref[:, 128:]
    scale = MLA_QK_DIM ** -0.5
    for h in range(N_HEADS):
        lo = h * MLA_PAD_DIM
        q1 = qf[:, lo:lo + 128]
        q2 = qf[:, lo + 128:lo + 256]
        ss = jnp.sum(q1 * q1 + q2 * q2, axis=-1, keepdims=True)
        r = lax.rsqrt(ss / MLA_QK_DIM + EPS) * scale
        qo_ref[0, :, lo:lo + 128] = (q1 * r * qw1).astype(BF16)
        qo_ref[0, :, lo + 128:lo + 256] = _rope(q2 * r * qw2, cos, sin, swap).astype(BF16)
        k1 = kf[:, h * 128:(h + 1) * 128]
        r = lax.rsqrt((jnp.sum(k1 * k1, axis=-1, keepdims=True) + kr_ss) / MLA_QK_DIM + EPS)
        ko_ref[0, :, lo:lo + 128] = (k1 * r * kw1).astype(BF16)
        ko_ref[0, :, lo + 128:lo + 256] = _rope(kr * r * kw2, cos, sin, swap).astype(BF16)
        _store_value_ext(vo_ref, h, vf[:, h * 128:(h + 1) * 128])


def _mla_up(p3, tabs, cqn, ckvn, wq, wk, wv, qn, kn):
    bsz, t, _ = p3.shape
    nt = t // ROW_TILE
    hw = N_HEADS * MLA_PAD_DIM
    tab_spec = pl.BlockSpec((ROW_TILE, 128), lambda b, i: (i, 0))

    def whole(a):
        return pl.BlockSpec(a.shape, lambda b, i: (0,) * a.ndim)

    args = (cqn.reshape(1, -1), ckvn.reshape(1, -1), wq, wk, wv, qn.reshape(1, -1), kn.reshape(1, -1))
    return pl.pallas_call(
        _mla_up_kernel,
        grid=(bsz, nt),
        in_specs=[pl.BlockSpec((1, ROW_TILE, MLA_Q_RANK), lambda b, i: (b, i, D_CQ // MLA_Q_RANK)),
                  pl.BlockSpec((1, ROW_TILE, MLA_KV_RANK), lambda b, i: (b, i, D_CKV // MLA_KV_RANK)),
                  pl.BlockSpec((1, ROW_TILE, 128), lambda b, i: (b, i, D_KR // 128)),
                  tab_spec, tab_spec, pl.BlockSpec((128, 128), lambda b, i: (0, 0))] + [whole(a) for a in args],
        out_specs=[pl.BlockSpec((1, ROW_TILE, hw), lambda b, i: (b, i, 0)),
                   pl.BlockSpec((1, ROW_TILE, hw), lambda b, i: (b, i, 0)),
                   pl.BlockSpec((1, ROW_TILE, 2 * GROUP_WIDTH), lambda b, i: (b, i, 0))],
        out_shape=[jax.ShapeDtypeStruct((bsz, t, hw), BF16),
                   jax.ShapeDtypeStruct((bsz, t, hw), BF16),
                   jax.ShapeDtypeStruct((bsz, t, 2 * GROUP_WIDTH), BF16)],
        compiler_params=_cparams(("parallel", "parallel")),
        name="mla_up",
    )(p3, p3, p3, *tabs, *args)


def _hgrn_consts():
    c = HGRN_CHUNK
    t = np.arange(c)[:, None]
    s = np.arange(c)[None, :]
    cum = np.stack([s <= t, s >= t]).astype(np.float32)
    masks = np.zeros((2, HGRN_LEVELS + 1, c, c), np.float32)
    masks[:, 0] = np.eye(c)
    for lvl in range(1, HGRN_LEVELS + 1):
        m = c >> lvl
        same = (t // (2 * m)) == (s // (2 * m))
        t_up, s_up = (t // m) % 2 == 1, (s // m) % 2 == 1
        masks[0, lvl] = same & t_up & ~s_up
        masks[1, lvl] = same & ~t_up & s_up
    return jnp.asarray(cum, BF16), jnp.asarray(masks)


def _hgrn_kernel(q_ref, i_ref, ff_ref, fb_ref, gate_ref, oml_ref, llb_ref, on_ref, cum_ref, mask_ref, y_ref,
                 *scr, n_ctx_chunks, n_chunks):
    c = HGRN_CHUNK
    n_chain = 2 * HGRN_HEADS_PER_STEP
    g_scrs, o_scrs = scr[:n_chain], scr[n_chain:]
    sub = lax.broadcasted_iota(jnp.int32, (8, 128), 0)

    def level_ref(g_scr, d, lvl):
        m = c >> lvl
        blk = 2 * m
        def row(r):
            return g_scr[r:r + 1, :]
        idx = [b * blk + m - 1 + d for b in range(c // blk)]
        if blk >= 8:
            return jnp.concatenate([jnp.broadcast_to(row(r), (blk, 128)) for r in idx], axis=0)
        per = 8 // blk
        tiles = []
        for j in range(c // 8):
            tile = jnp.broadcast_to(row(idx[j * per + per - 1]), (8, 128))
            for u in range(per - 2, -1, -1):
                tile = jnp.where(sub < (u + 1) * blk, jnp.broadcast_to(row(idx[j * per + u]), (8, 128)), tile)
            tiles.append(tile)
        return jnp.concatenate(tiles, axis=0)

    chains = [(hh, d) for hh in range(HGRN_HEADS_PER_STEP) for d in range(2)]

    def chunk_rows(n):
        n_lat_chunks = n_chunks - n_ctx_chunks
        cf = jnp.where(n < n_ctx_chunks, n_lat_chunks + n, n - n_ctx_chunks)
        cb = n_chunks - 1 - n
        return [pl.ds(pl.multiple_of(cf * c, c), c), pl.ds(pl.multiple_of(cb * c, c), c)]

    def gate_decay_parts(n):
        rows_d = chunk_rows(n)
        out = []
        for hh, d in chains:
            cols = slice(hh * HEAD_DIM, (hh + 1) * HEAD_DIM)
            f = (ff_ref if d == 0 else fb_ref)[0, rows_d[d], cols].astype(F32)
            a = llb_ref[d:d + 1, cols] - f
            lf = (jnp.minimum(f, 0.0) - jnp.log(1.0 + jnp.exp(-jnp.abs(f)))
                  + jnp.maximum(a, 0.0) + jnp.log(1.0 + jnp.exp(-jnp.abs(a))))
            hi = lf.astype(BF16)
            r1 = lf - hi.astype(F32)
            mid = r1.astype(BF16)
            lo = (r1 - mid.astype(F32)).astype(BF16)
            cum = cum_ref[d]
            out.append((_dot(cum, hi), _dot(cum, mid), _dot(cum, lo)))
        return out

    def gate_qk(n):
        rows_d = chunk_rows(n)
        out = []
        for hh, d in chains:
            cols = slice(hh * HEAD_DIM, (hh + 1) * HEAD_DIM)
            qv = q_ref[0, rows_d[d], cols].astype(F32)
            qb = (_silu(qv) * (HEAD_DIM ** -0.5)).astype(BF16)
            f = (ff_ref if d == 0 else fb_ref)[0, rows_d[d], cols].astype(F32)
            kb = (oml_ref[d:d + 1, cols] / (1.0 + jnp.exp(f))).astype(BF16)
            out.append((qb, kb))
        return out

    def gate_join(qk, parts):
        return [(qb, kb, p0 + p1 + p2) for (qb, kb), (p0, p1, p2) in zip(qk, parts)]

    def body(n, carry):
        states, cur = carry
        n_next = jnp.minimum(n + 1, n_chunks - 1)
        rows_d = chunk_rows(n)
        vb, gtot, o, scores, o_intra, st_upd = {}, {}, {}, {}, {}, {}
        new = [None] * n_chain
        for ci, (hh, d) in enumerate(chains):
            g_scrs[ci][...] = cur[ci][2]

        def scores_of(group):
            for ci in group:
                hh, d = chains[ci]
                qb, kb, g = cur[ci]
                vb[ci] = i_ref[0, rows_d[d], hh * HEAD_DIM:(hh + 1) * HEAD_DIM]
                end = (c - 1) * (1 - d)
                gtot[ci] = g_scrs[ci][end:end + 1, :]
                o[ci] = _dot_nt(qb * jnp.exp(g).astype(BF16), states[ci].astype(BF16))
                scores[ci] = jnp.where(mask_ref[d, 0] > 0.5, _dot_nt(qb, kb), 0.0)
            for lvl in range(1, HGRN_LEVELS + 1):
                for ci in group:
                    d = chains[ci][1]
                    qb, kb, g = cur[ci]
                    e = jnp.exp(-jnp.abs(g - level_ref(g_scrs[ci], d, lvl))).astype(BF16)
                    s_l = _dot_nt(qb * e, kb * e)
                    scores[ci] = jnp.where(mask_ref[d, lvl] > 0.5, s_l, scores[ci])

        def issue_last(group):
            for ci in group:
                qb, kb, g = cur[ci]
                o_intra[ci] = _dot(scores[ci].astype(BF16), vb[ci])
                kend = kb * jnp.exp(gtot[ci] - g).astype(BF16)
                vt = vb[ci].astype(F32).T.astype(BF16)
                st_upd[ci] = _dot(vt, kend)

        def finish(group):
            for ci in group:
                o_scrs[ci][rows_d[chains[ci][1]], :] = o[ci] + o_intra[ci]
                new[ci] = states[ci] * jnp.exp(gtot[ci]) + st_upd[ci]

        every = range(n_chain)
        scores_of(every)
        nxt_parts = gate_decay_parts(n_next)
        issue_last(every)
        nxt_qk = gate_qk(n_next)
        finish(every)
        return tuple(new), gate_join(nxt_qk, nxt_parts)

    zero = jnp.zeros((HEAD_DIM, HEAD_DIM), F32)
    lax.fori_loop(0, n_chunks, body, ((zero,) * n_chain, gate_join(gate_qk(0), gate_decay_parts(0))), unroll=4)

    n_row_tiles = (n_chunks * c) // ROW_TILE

    def readout(i, carry):
        rows = pl.ds(pl.multiple_of(i * ROW_TILE, ROW_TILE), ROW_TILE)
        for hh in range(HGRN_HEADS_PER_STEP):
            cols = slice(hh * HEAD_DIM, (hh + 1) * HEAD_DIM)
            o = o_scrs[2 * hh][rows, :] + o_scrs[2 * hh + 1][rows, :]
            o = o * lax.rsqrt(jnp.mean(o * o, axis=-1, keepdims=True) + EPS) * on_ref[:, cols]
            y_ref[0, rows, cols] = (o * _silu(gate_ref[0, rows, cols].astype(F32))).astype(y_ref.dtype)
        return carry

    lax.fori_loop(0, n_row_tiles, readout, 0)


def _hgrn(p3, one_minus_lb, log_lb, o_norm, ctx_len):
    bsz, t, _ = p3.shape
    cum, masks = _hgrn_consts()
    bw = HGRN_HEADS_PER_STEP * HEAD_DIM
    n_chain = 2 * HGRN_HEADS_PER_STEP

    def col(off):
        return pl.BlockSpec((1, t, bw), lambda b, h: (b, 0, off // bw + h))

    return pl.pallas_call(
        functools.partial(_hgrn_kernel, n_ctx_chunks=ctx_len // HGRN_CHUNK, n_chunks=t // HGRN_CHUNK),
        grid=(bsz, GROUP_WIDTH // bw),
        in_specs=[col(B_Q), col(B_I), col(B_FF), col(B_FB), col(B_G),
                  pl.BlockSpec((2, bw), lambda b, h: (0, h)),
                  pl.BlockSpec((2, bw), lambda b, h: (0, h)),
                  pl.BlockSpec((1, bw), lambda b, h: (0, h)),
                  pl.BlockSpec(cum.shape, lambda b, h: (0, 0, 0)),
                  pl.BlockSpec(masks.shape, lambda b, h: (0, 0, 0, 0))],
        out_specs=pl.BlockSpec((1, t, bw), lambda b, h: (b, 0, h)),
        out_shape=jax.ShapeDtypeStruct((bsz, t, GROUP_WIDTH), BF16),
        scratch_shapes=([pltpu.VMEM((HGRN_CHUNK, HEAD_DIM), F32)] * n_chain
                        + [pltpu.VMEM((t, HEAD_DIM), F32)] * n_chain),
        compiler_params=_cparams(("parallel", "parallel")),
        name="hgrn2",
    )(p3, p3, p3, p3, p3, one_minus_lb, log_lb, o_norm.reshape(1, -1), cum, masks)


def _conv_kernel(u_ref, up_ref, un_ref, gl_ref, glp_ref, gln_ref, gate_ref, w_ref, cb_ref, lw_ref, lb_ref, y_ref,
                 xs_scr, acc_scr, *, n_lat):
    t = pl.program_id(1)
    h = CONV_HALO
    prev_ok = jnp.logical_and(t >= 1, t < n_lat).astype(F32)
    next_ok = (t < n_lat - 1).astype(F32)
    n_lane_blocks = GROUP_WIDTH // 128
    x_prev = up_ref[0].astype(F32) * jax.nn.sigmoid(glp_ref[0].astype(F32)) * prev_ok
    x_main = u_ref[0].astype(F32) * jax.nn.sigmoid(gl_ref[0].astype(F32))
    x_next = un_ref[0].astype(F32) * jax.nn.sigmoid(gln_ref[0].astype(F32)) * next_ok
    for cb in range(n_lane_blocks):
        sl = slice(cb * 128, (cb + 1) * 128)
        xs_scr[cb, 0:h, :] = x_prev[:, sl]
        xs_scr[cb, h:h + ROW_TILE, :] = x_main[:, sl]
        xs_scr[cb, h + ROW_TILE:, :] = x_next[:, sl]
    rc = 64
    base = h - CONV_WIDTH // 2

    def lane_block(cb, carry):
        for r in range(ROW_TILE // rc):
            acc = jnp.zeros((rc, 128), F32)
            for tap in range(CONV_WIDTH):
                lo = base + r * rc + tap
                acc = acc + xs_scr[cb, lo:lo + rc, :] * w_ref[cb, tap:tap + 1, :]
            acc_scr[cb, r * rc:(r + 1) * rc, :] = acc
        return carry

    lax.fori_loop(0, n_lane_blocks, lane_block, 0)
    y = jnp.concatenate([acc_scr[cb] for cb in range(n_lane_blocks)], axis=1) + cb_ref[...]
    mu = jnp.mean(y, axis=-1, keepdims=True)
    yc = y - mu
    var = jnp.mean(yc * yc, axis=-1, keepdims=True)
    z = yc * lax.rsqrt(var + EPS) * lw_ref[...] + lb_ref[...]
    y_ref[0] = (_silu(z) * _silu(gate_ref[0].astype(F32))).astype(y_ref.dtype)


def _conv(p3, conv_w, conv_b, ln_w, ln_b, ctx_out):
    bsz, t, _ = p3.shape
    n_lat = t // ROW_TILE - 1
    nt = n_lat + (1 if ctx_out else 0)
    hpt = ROW_TILE // CONV_HALO
    n_halo = t // CONV_HALO
    n_lane_blocks = GROUP_WIDTH // 128
    wpad = jnp.pad(conv_w, ((0, 32 - CONV_WIDTH), (0, 0))).reshape(32, n_lane_blocks, 128).transpose(1, 0, 2)

    def main(off):
        return pl.BlockSpec((1, ROW_TILE, GROUP_WIDTH), lambda b, i: (b, i, off // GROUP_WIDTH))

    def prev(off):
        return pl.BlockSpec((1, CONV_HALO, GROUP_WIDTH),
                            lambda b, i: (b, jnp.maximum(i * hpt - 1, 0), off // GROUP_WIDTH))

    def nxt(off):
        return pl.BlockSpec((1, CONV_HALO, GROUP_WIDTH),
                            lambda b, i: (b, jnp.minimum((i + 1) * hpt, n_halo - 1), off // GROUP_WIDTH))

    vec = pl.BlockSpec((1, GROUP_WIDTH), lambda b, i: (0, 0))
    return pl.pallas_call(
        functools.partial(_conv_kernel, n_lat=n_lat),
        grid=(bsz, nt),
        in_specs=[main(C_U), prev(C_U), nxt(C_U), main(C_GLU), prev(C_GLU), nxt(C_GLU), main(C_G),
                  pl.BlockSpec((n_lane_blocks, 32, 128), lambda b, i: (0, 0, 0)), vec, vec, vec],
        out_specs=pl.BlockSpec((1, ROW_TILE, GROUP_WIDTH), lambda b, i: (b, i, 0)),
        out_shape=jax.ShapeDtypeStruct((bsz, nt * ROW_TILE, GROUP_WIDTH), BF16),
        scratch_shapes=[pltpu.VMEM((n_lane_blocks, ROW_TILE + 2 * CONV_HALO, 128), F32),
                        pltpu.VMEM((n_lane_blocks, ROW_TILE, 128), F32)],
        compiler_params=_cparams(("parallel", "parallel")),
        name="conformer_conv",
    )(p3, p3, p3, p3, p3, p3, p3, wpad, conv_b.reshape(1, -1), ln_w.reshape(1, -1), ln_b.reshape(1, -1))


def _out_kernel(ya_ref, yb_ref, yc_ref, yd_ref, w_ref, x_ref, gate_ref, o_ref, wb_scr, *, ctx_rows):
    b = pl.program_id(1)
    gw = GROUP_WIDTH

    @pl.when(jnp.logical_and(b == 0, pl.program_id(2) == 0))
    def _():
        wb_scr[...] = w_ref[0].astype(BF16)

    acc = _dot(ya_ref[0], wb_scr[0:gw, :])
    acc = acc + _dot(yb_ref[0], wb_scr[gw:2 * gw, :])
    acc = acc + _dot(yc_ref[0], wb_scr[2 * gw:3 * gw, :])
    acc = acc + _dot(yd_ref[0], wb_scr[3 * gw:4 * gw, :])

    row = 2 if ctx_rows else b
    o_ref[0] = x_ref[0] + gate_ref[pl.ds(row, 1), :] * acc


def _out_proj(ys, w_out, layer, x, mod, ctx_rows):
    bsz, rows, d = x.shape
    tm = ROW_TILE if ctx_rows else _pick_tile(rows, (1024, 512, 256))
    y_off = (ys[0].shape[1] - rows) // tm if ctx_rows else 0
    tn = 512
    y_spec = pl.BlockSpec((1, tm, GROUP_WIDTH), lambda j, b, i: (b, i + y_off, 0))
    return pl.pallas_call(
        functools.partial(_out_kernel, ctx_rows=ctx_rows),
        grid=(d // tn, bsz, rows // tm),
        in_specs=[y_spec, y_spec, y_spec, y_spec,
                  pl.BlockSpec((1, 4 * GROUP_WIDTH, tn), lambda j, b, i: (layer, 0, j)),
                  pl.BlockSpec((1, tm, tn), lambda j, b, i: (b, i, j)),
                  pl.BlockSpec((8, tn), lambda j, b, i: (0, 2 * d // tn + j))],
        out_specs=pl.BlockSpec((1, tm, tn), lambda j, b, i: (b, i, j)),
        out_shape=jax.ShapeDtypeStruct((bsz, rows, d), F32),
        scratch_shapes=[pltpu.VMEM((4 * GROUP_WIDTH, tn), BF16)],
        compiler_params=_cparams(("parallel", "arbitrary", "arbitrary")),
        name="out_proj",
    )(*ys, w_out, x, mod)


D_PACK_ROWS = 64


def _pack_d_kernel(w_ref, o_ref, *, n_src_blocks):
    @pl.when(pl.program_id(0) < n_src_blocks)
    def _():
        o_ref[...] = w_ref[0].astype(BF16)

    @pl.when(pl.program_id(0) >= n_src_blocks)
    def _():
        o_ref[...] = jnp.zeros(o_ref.shape, o_ref.dtype)


def _group_d_weight(w_in_t, layer):
    k = w_in_t.shape[2]
    r = D_PACK_ROWS
    base = ABC_WIDTH // r
    src_cq, src_ckv, src_kr, src_g = base, base + 768 // r, base + 1280 // r, base + 1344 // r
    n_src_blocks = (D_KR + MLA_ROPE_DIM) // r

    def src(j):
        return jnp.where(j < D_CKV // r, src_g + j,
                         jnp.where(j < D_CQ // r, src_ckv + j - D_CKV // r,
                                   jnp.where(j < D_KR // r, src_cq + j - D_CQ // r, src_kr)))

    return pl.pallas_call(
        functools.partial(_pack_d_kernel, n_src_blocks=n_src_blocks),
        grid=(D_WIDTH // r,),
        in_specs=[pl.BlockSpec((1, r, k), lambda j: (layer, src(j), 0))],
        out_specs=pl.BlockSpec((r, k), lambda j: (j, 0)),
        out_shape=jax.ShapeDtypeStruct((D_WIDTH, k), BF16),
        compiler_params=_cparams(("parallel",)),
        name="pack_group_d",
    )(w_in_t)


def kernel(x, c, ctx, c_ctx, w_mod, b_mod, norm_w, w_in, w_out, att_q_norm, att_k_norm, hgrn_lb_logits, hgrn_o_norm,
           conv_w, conv_b, conv_ln_w, conv_ln_b, mla_q_norm, mla_kv_norm, mla_w_uq, mla_w_ukv, mla_qk_q_norm,
           mla_qk_k_norm):
    bsz, seq, d = x.shape
    ctx_len = ctx.shape[1]
    depth = w_mod.shape[0]
    t = ctx_len + seq
    assert ctx_len == ROW_TILE and seq % ROW_TILE == 0 and bsz <= 2
    n_tiles = t // ROW_TILE

    lb_all = jnp.cumsum(jax.nn.softmax(hgrn_lb_logits.astype(F32), axis=0), axis=0)
    lb_all = lb_all - lb_all[0:1]

    c8 = jnp.zeros((8, d), F32).at[0:bsz].set(c).at[2].set(c_ctx)
    mod_all = _modulation(c8, w_mod, b_mod)

    gqa_tabs = _rope_tables(seq, ctx_len, HEAD_DIM)
    mla_tabs = _rope_tables(seq, ctx_len, MLA_ROPE_DIM)

    w_in_t = jnp.swapaxes(w_in, 1, 2)

    xc, xl = ctx, x
    for l in range(depth):
        ctx_out = l < depth - 1
        mod = mod_all[l]
        h = _norm_modulate(xc, xl, mod, norm_w[l], n_tiles)
        h2 = h.reshape(bsz * t, d)
        p3 = _in_proj_abc(h2, w_in_t, l).reshape(bsz, t, ABC_WIDTH)
        pd3 = _matmul_nt(h2, _group_d_weight(w_in_t, l), BF16).reshape(bsz, t, D_WIDTH)

        qa, ka, va = _gqa_prep(p3, gqa_tabs, att_q_norm[l], att_k_norm[l])
        ya = _attention(qa, ka, va, p3, A_G, N_HEADS // ATT_KV_HEADS, HEAD_DIM, True, ctx_out)

        yb = _hgrn(p3, 1.0 - lb_all[l], jnp.log(lb_all[l]), hgrn_o_norm[l], ctx_len)

        yc = _conv(p3, conv_w[l], conv_b[l], conv_ln_w[l], conv_ln_b[l], ctx_out)

        wq = jnp.pad(mla_w_uq[l].reshape(MLA_Q_RANK, N_HEADS, MLA_QK_DIM),
                     ((0, 0), (0, 0), (0, MLA_PAD_DIM - MLA_QK_DIM))).reshape(MLA_Q_RANK, -1).astype(BF16)
        wkv = mla_w_ukv[l].reshape(MLA_KV_RANK, N_HEADS, 2 * HEAD_DIM)
        wk = wkv[:, :, :HEAD_DIM].reshape(MLA_KV_RANK, -1).astype(BF16)
        wv = wkv[:, :, HEAD_DIM:].reshape(MLA_KV_RANK, -1).astype(BF16)
        qn = jnp.pad(mla_qk_q_norm[l], (0, MLA_PAD_DIM - MLA_QK_DIM))
        kn = jnp.pad(mla_qk_k_norm[l], (0, MLA_PAD_DIM - MLA_QK_DIM))
        qd, kd, vd = _mla_up(pd3, mla_tabs, mla_q_norm[l], mla_kv_norm[l], wq, wk, wv, qn, kn)
        yd = _attention(qd, kd, vd, pd3, D_G, 4, MLA_PAD_DIM, False, ctx_out)

        ys = (ya, yb, yc, yd)
        xl_new = _out_proj(ys, w_out, l, xl, mod, False)
        if ctx_out:
            xc = _out_proj(ys, w_out, l, xc, mod, True)
        xl = xl_new
    return xl
```

```python
import functools

import numpy as np
import jax
import jax.numpy as jnp
from jax import lax
from jax.experimental import pallas as pl
from jax.experimental.pallas import tpu as pltpu

F32 = jnp.float32
BF16 = jnp.bfloat16

EPS = 1e-6
ROPE_THETA = 10000.0
GRID_W = 64
HEAD_DIM = 128
GROUP_WIDTH = 1024
N_HEADS = 8
ATT_KV_HEADS = 2
MLA_Q_RANK = 768
MLA_KV_RANK = 512
MLA_ROPE_DIM = 64
MLA_QK_DIM = 192
MLA_PAD_DIM = 256
CONV_WIDTH = 31
CONV_HALO = 16
HGRN_CHUNK = 64
HGRN_LEVELS = 6
HGRN_HEADS_PER_STEP = 2
ROW_TILE = 256
NORM_SUB_ROWS = 16

IN_TILE = 512
C_U, C_GLU, C_G = 0, 1024, 2048
A_Q, A_K, A_V, A_G = 3072, 4096, 4352, 4608
B_Q, B_I, B_FF, B_FB, B_G = 5632, 6656, 7680, 8704, 9728
ABC_WIDTH = 10752
D_G, D_CKV, D_CQ, D_KR = 0, 1024, 1536, 2304
D_WIDTH = 2560

VMEM_LIMIT = 56 * 1024 * 1024


def _cparams(sem):
    return pltpu.CompilerParams(dimension_semantics=sem, vmem_limit_bytes=VMEM_LIMIT)


def _silu(x):
    return x * jax.nn.sigmoid(x)


def _dot(a, b):
    return jnp.dot(a, b, preferred_element_type=F32)


def _dot_nt(a, b):
    return lax.dot_general(a, b, (((1,), (1,)), ((), ())), preferred_element_type=F32)


def _mod_kernel(c_ref, w_ref, b_ref, o_ref):
    a = _silu(c_ref[...]).astype(BF16)
    o_ref[0] = _dot(a, w_ref[0].astype(BF16)) + b_ref[0]


def _modulation(c8, w_mod, b_mod):
    depth, d, n = w_mod.shape
    tn = 512
    return pl.pallas_call(
        _mod_kernel,
        grid=(depth, n // tn),
        in_specs=[pl.BlockSpec((8, d), lambda l, j: (0, 0)),
                  pl.BlockSpec((1, d, tn), lambda l, j: (l, 0, j)),
                  pl.BlockSpec((1, 1, tn), lambda l, j: (l, 0, j))],
        out_specs=pl.BlockSpec((1, 8, tn), lambda l, j: (l, 0, j)),
        out_shape=jax.ShapeDtypeStruct((depth, 8, n), F32),
        compiler_params=_cparams(("parallel", "parallel")),
        name="modulation",
    )(c8, w_mod, b_mod.reshape(depth, 1, n))


def _norm_kernel(xc_ref, xl_ref, mod_ref, nw_ref, o_ref, *, d, n_lat):
    b = pl.program_id(0)
    t = pl.program_id(1)

    def body(x_ref, row):
        m = mod_ref[pl.ds(row, 1), :]
        shift = m[:, :d]
        gain = nw_ref[...] * (1.0 + m[:, d:2 * d])

        def sub_tile(r, carry):
            rows = pl.ds(pl.multiple_of(r * NORM_SUB_ROWS, NORM_SUB_ROWS), NORM_SUB_ROWS)
            x = x_ref[0, rows, :]
            ms = jnp.mean(x * x, axis=-1, keepdims=True)
            o_ref[0, rows, :] = (x * lax.rsqrt(ms + EPS) * gain + shift).astype(o_ref.dtype)
            return carry

        lax.fori_loop(0, ROW_TILE // NORM_SUB_ROWS, sub_tile, 0, unroll=4)

    @pl.when(t == n_lat)
    def _():
        body(xc_ref, 2)

    @pl.when(t < n_lat)
    def _():
        body(xl_ref, b)


def _norm_modulate(xc, xl, mod, nw, n_tiles):
    bsz, _, d = xc.shape
    n_lat = n_tiles - 1
    return pl.pallas_call(
        functools.partial(_norm_kernel, d=d, n_lat=n_lat),
        grid=(bsz, n_tiles),
        in_specs=[pl.BlockSpec((1, ROW_TILE, d), lambda b, t: (b, 0, 0)),
                  pl.BlockSpec((1, ROW_TILE, d), lambda b, t: (b, jnp.minimum(t, n_lat - 1), 0)),
                  pl.BlockSpec((8, 3 * d), lambda b, t: (0, 0)),
                  pl.BlockSpec((1, d), lambda b, t: (0, 0))],
        out_specs=pl.BlockSpec((1, ROW_TILE, d), lambda b, t: (b, t, 0)),
        out_shape=jax.ShapeDtypeStruct((bsz, n_tiles * ROW_TILE, d), BF16),
        compiler_params=_cparams(("parallel", "parallel")),
        name="norm_modulate",
    )(xc, xl, mod, nw.reshape(1, d))


def _mm_kernel(a_ref, w_ref, o_ref):
    o_ref[...] = _dot_nt(a_ref[...], w_ref[...]).astype(o_ref.dtype)


def _pick_tile(n, cands):
    for c in cands:
        if n % c == 0:
            return c
    raise ValueError(f"no tile for {n}")


def _matmul_nt(a, wt, out_dtype):
    m, k = a.shape
    n, _ = wt.shape
    tm = _pick_tile(m, (1088, 1024, 768, 512, 256))
    tn = _pick_tile(n, (512, 256, 128))
    return pl.pallas_call(
        _mm_kernel,
        grid=(m // tm, n // tn),
        in_specs=[pl.BlockSpec((tm, k), lambda i, j: (i, 0)),
                  pl.BlockSpec((tn, k), lambda i, j: (j, 0))],
        out_specs=pl.BlockSpec((tm, tn), lambda i, j: (i, j)),
        out_shape=jax.ShapeDtypeStruct((m, n), out_dtype),
        compiler_params=_cparams(("parallel", "parallel")),
        name="in_proj_d",
    )(a, wt)


def _mm_wcast_kernel(a_ref, w_ref, o_ref, wb_scr):
    @pl.when(pl.program_id(1) == 0)
    def _():
        wb_scr[...] = w_ref[0].astype(BF16)

    o_ref[...] = _dot_nt(a_ref[...], wb_scr[...]).astype(o_ref.dtype)


def _in_proj_abc(a, w_in_t, layer):
    m, k = a.shape
    tm = _pick_tile(m, (1088, 1024, 768, 512, 256))
    n_tiles = ABC_WIDTH // IN_TILE
    n_c = (C_G + GROUP_WIDTH) // IN_TILE
    return pl.pallas_call(
        _mm_wcast_kernel,
        grid=(n_tiles, m // tm),
        in_specs=[pl.BlockSpec((tm, k), lambda j, i: (i, 0)),
                  pl.BlockSpec((1, IN_TILE, k),
                               lambda j, i: (layer, jnp.where(j < n_c, j + n_tiles - n_c, j - n_c), 0))],
        out_specs=pl.BlockSpec((tm, IN_TILE), lambda j, i: (i, j)),
        out_shape=jax.ShapeDtypeStruct((m, ABC_WIDTH), BF16),
        scratch_shapes=[pltpu.VMEM((IN_TILE, k), BF16)],
        compiler_params=_cparams(("parallel", "arbitrary")),
        name="in_proj_abc",
    )(a, w_in_t)


def _rope_tables(seq, ctx_len, rope_dim):
    quarter = rope_dim // 4
    inv = (np.float32(ROPE_THETA) ** (-np.arange(quarter, dtype=np.float32) / np.float32(quarter))).astype(np.float32)
    pos = np.arange(seq, dtype=np.int32)
    ar = (pos // GRID_W).astype(np.float32)[:, None] * inv
    ac = (pos % GRID_W).astype(np.float32)[:, None] * inv
    cos = np.concatenate([np.cos(ar), np.cos(ar), np.cos(ac), np.cos(ac)], axis=-1)
    sin = np.concatenate([np.sin(ar), np.sin(ar), np.sin(ac), np.sin(ac)], axis=-1)

    def full(tab, fill):
        tab = np.pad(tab, ((0, 0), (0, 128 - rope_dim)), constant_values=fill)
        return jnp.asarray(np.concatenate([tab, np.full((ctx_len, 128), fill, np.float32)], axis=0), F32)

    swap = np.zeros((128, 128), np.float32)
    for i in range(rope_dim):
        if (i // quarter) % 2 == 0:
            swap[i + quarter, i] = -1.0
        else:
            swap[i - quarter, i] = 1.0
    return full(cos, 1.0), full(sin, 0.0), jnp.asarray(swap, BF16)


def _rope(x, cos, sin, swap):
    return x * cos + _dot(x.astype(BF16), swap) * sin


def _gqa_prep_kernel(q_ref, k_ref, v_ref, cos_ref, sin_ref, swap_ref, qn_ref, kn_ref, qo_ref, ko_ref, vo_ref):
    cos, sin, swap = cos_ref[...], sin_ref[...], swap_ref[...]

    def head(x, w, scale):
        ms = jnp.mean(x * x, axis=-1, keepdims=True)
        y = x * lax.rsqrt(ms + EPS) * w
        return (_rope(y, cos, sin, swap) * scale).astype(BF16)

    for h in range(N_HEADS):
        sl = slice(h * HEAD_DIM, (h + 1) * HEAD_DIM)
        qo_ref[0, :, sl] = head(q_ref[0, :, sl].astype(F32), qn_ref[...], HEAD_DIM ** -0.5)
    for h in range(ATT_KV_HEADS):
        sl = slice(h * HEAD_DIM, (h + 1) * HEAD_DIM)
        ko_ref[0, :, sl] = head(k_ref[0, :, sl].astype(F32), kn_ref[...], 1.0)
        _store_value_ext(vo_ref, h, v_ref[0, :, sl])


def _store_value_ext(vo_ref, h, v):
    vo_ref[0, :, 2 * h * HEAD_DIM:(2 * h + 1) * HEAD_DIM] = v
    vo_ref[0, :, (2 * h + 1) * HEAD_DIM:(2 * h + 2) * HEAD_DIM] = jnp.ones(v.shape, v.dtype)


def _gqa_prep(p3, tabs, qn, kn):
    bsz, t, _ = p3.shape
    nt = t // ROW_TILE
    kvw = ATT_KV_HEADS * HEAD_DIM
    tab_spec = pl.BlockSpec((ROW_TILE, 128), lambda b, i: (i, 0))
    swap_spec = pl.BlockSpec((128, 128), lambda b, i: (0, 0))
    vec_spec = pl.BlockSpec((1, 128), lambda b, i: (0, 0))
    return pl.pallas_call(
        _gqa_prep_kernel,
        grid=(bsz, nt),
        in_specs=[pl.BlockSpec((1, ROW_TILE, GROUP_WIDTH), lambda b, i: (b, i, A_Q // GROUP_WIDTH)),
                  pl.BlockSpec((1, ROW_TILE, kvw), lambda b, i: (b, i, A_K // kvw)),
                  pl.BlockSpec((1, ROW_TILE, kvw), lambda b, i: (b, i, A_V // kvw)),
                  tab_spec, tab_spec, swap_spec, vec_spec, vec_spec],
        out_specs=[pl.BlockSpec((1, ROW_TILE, GROUP_WIDTH), lambda b, i: (b, i, 0)),
                   pl.BlockSpec((1, ROW_TILE, kvw), lambda b, i: (b, i, 0)),
                   pl.BlockSpec((1, ROW_TILE, 2 * kvw), lambda b, i: (b, i, 0))],
        out_shape=[jax.ShapeDtypeStruct((bsz, t, GROUP_WIDTH), BF16),
                   jax.ShapeDtypeStruct((bsz, t, kvw), BF16),
                   jax.ShapeDtypeStruct((bsz, t, 2 * kvw), BF16)],
        compiler_params=_cparams(("parallel", "parallel")),
        name="gqa_prep",
    )(p3, p3, p3, *tabs, qn.reshape(1, 128), kn.reshape(1, 128))


def _attn_kernel(q_ref, k_ref, v_ref, g_ref, o_ref, *, n_heads, dq, shared_kv, n_lat):
    i = pl.program_id(2)
    vw = 2 * HEAD_DIM

    def attend(lo):
        def kv(j):
            if shared_kv:
                return k_ref[0, lo:, :], v_ref[0, lo:, :]
            return k_ref[0, lo:, j * dq:(j + 1) * dq], v_ref[0, lo:, j * vw:(j + 1) * vw]

        s = [None] * n_heads
        p = [None] * n_heads
        for j in range(n_heads + 2):
            if j < n_heads:
                s[j] = _dot_nt(q_ref[0, :, j * dq:(j + 1) * dq], kv(j)[0])
            if 1 <= j <= n_heads:
                m = jnp.max(s[j - 1], axis=-1, keepdims=True)
                p[j - 1] = jnp.exp((s[j - 1] - m).astype(BF16))
                s[j - 1] = None
            if j >= 2:
                oe = _dot(p[j - 2], kv(j - 2)[1])
                p[j - 2] = None
                o = oe[:, :HEAD_DIM] / oe[:, HEAD_DIM:]
                sl = slice((j - 2) * HEAD_DIM, (j - 1) * HEAD_DIM)
                o_ref[0, :, sl] = (o * _silu(g_ref[0, :, sl].astype(F32))).astype(o_ref.dtype)

    @pl.when(i < n_lat)
    def _():
        attend(0)

    @pl.when(i == n_lat)
    def _():
        attend(n_lat * ROW_TILE)


def _attention(q, k, v_ext, p3, gate_off, n_heads, dq, shared_kv, ctx_out):
    bsz, t, _ = q.shape
    n_lat = t // ROW_TILE - 1
    nt = n_lat + (1 if ctx_out else 0)
    n_kv = 1 if shared_kv else n_heads
    gw = n_heads * HEAD_DIM
    n_steps = q.shape[-1] // (n_heads * dq)
    return pl.pallas_call(
        functools.partial(_attn_kernel, n_heads=n_heads, dq=dq, shared_kv=shared_kv, n_lat=n_lat),
        grid=(bsz, n_steps, nt),
        in_specs=[pl.BlockSpec((1, ROW_TILE, n_heads * dq), lambda b, h, i: (b, i, h)),
                  pl.BlockSpec((1, t, n_kv * dq), lambda b, h, i: (b, 0, h)),
                  pl.BlockSpec((1, t, n_kv * 2 * HEAD_DIM), lambda b, h, i: (b, 0, h)),
                  pl.BlockSpec((1, ROW_TILE, gw), lambda b, h, i: (b, i, gate_off // gw + h))],
        out_specs=pl.BlockSpec((1, ROW_TILE, gw), lambda b, h, i: (b, i, h)),
        out_shape=jax.ShapeDtypeStruct((bsz, nt * ROW_TILE, n_steps * gw), BF16),
        compiler_params=_cparams(("parallel", "parallel", "parallel")),
        name="attention",
    )(q, k, v_ext, p3)


def _mla_up_kernel(cq_ref, ckv_ref, kr_ref, cos_ref, sin_ref, swap_ref, cqn_ref, ckvn_ref, wq_ref, wk_ref, wv_ref,
                   qn_ref, kn_ref, qo_ref, ko_ref, vo_ref):
    cos, sin, swap = cos_ref[...], sin_ref[...], swap_ref[...]

    def rms(x, w):
        return x * lax.rsqrt(jnp.mean(x * x, axis=-1, keepdims=True) + EPS) * w

    cq = rms(cq_ref[0].astype(F32), cqn_ref[...]).astype(BF16)
    ckv = rms(ckv_ref[0].astype(F32), ckvn_ref[...]).astype(BF16)
    qf = _dot(cq, wq_ref[...])
    kf = _dot(ckv, wk_ref[...])
    vf = _dot(ckv, wv_ref[...]).astype(BF16)
    kr = kr_ref[0].astype(F32)
    kr_ss = jnp.sum(kr * kr, axis=-1, keepdims=True)
    qw1, qw2 = qn_ref[:, :128], qn_ref[:, 128:]
    kw1, kw2 = kn_ref[:, :128], kn_ref[:, 128:]
    kr_rot = _rope(kr * kw2, cos, sin, swap)
    scale = MLA_QK_DIM ** -0.5
    for h in range(N_HEADS):
        lo = h * MLA_PAD_DIM
        q1 = qf[:, lo:lo + 128]
        q2 = qf[:, lo + 128:lo + 256]
        ss = jnp.sum(q1 * q1 + q2 * q2, axis=-1, keepdims=True)
        r = lax.rsqrt(ss / MLA_QK_DIM + EPS) * scale
        qo_ref[0, :, lo:lo + 128] = (q1 * r * qw1).astype(BF16)
        qo_ref[0, :, lo + 128:lo + 256] = _rope(q2 * r * qw2, cos, sin, swap).astype(BF16)
        k1 = kf[:, h * 128:(h + 1) * 128]
        r = lax.rsqrt((jnp.sum(k1 * k1, axis=-1, keepdims=True) + kr_ss) / MLA_QK_DIM + EPS)
        ko_ref[0, :, lo:lo + 128] = (k1 * r * kw1).astype(BF16)
        ko_ref[0, :, lo + 128:lo + 256] = (kr_rot * r).astype(BF16)
        _store_value_ext(vo_ref, h, vf[:, h * 128:(h + 1) * 128])


def _mla_up(p3, tabs, cqn, ckvn, wq, wk, wv, qn, kn):
    bsz, t, _ = p3.shape
    nt = t // ROW_TILE
    hw = N_HEADS * MLA_PAD_DIM
    tab_spec = pl.BlockSpec((ROW_TILE, 128), lambda b, i: (i, 0))

    def whole(a):
        return pl.BlockSpec(a.shape, lambda b, i: (0,) * a.ndim)

    args = (cqn.reshape(1, -1), ckvn.reshape(1, -1), wq, wk, wv, qn.reshape(1, -1), kn.reshape(1, -1))
    return pl.pallas_call(
        _mla_up_kernel,
        grid=(bsz, nt),
        in_specs=[pl.BlockSpec((1, ROW_TILE, MLA_Q_RANK), lambda b, i: (b, i, D_CQ // MLA_Q_RANK)),
                  pl.BlockSpec((1, ROW_TILE, MLA_KV_RANK), lambda b, i: (b, i, D_CKV // MLA_KV_RANK)),
                  pl.BlockSpec((1, ROW_TILE, 128), lambda b, i: (b, i, D_KR // 128)),
                  tab_spec, tab_spec, pl.BlockSpec((128, 128), lambda b, i: (0, 0))] + [whole(a) for a in args],
        out_specs=[pl.BlockSpec((1, ROW_TILE, hw), lambda b, i: (b, i, 0)),
                   pl.BlockSpec((1, ROW_TILE, hw), lambda b, i: (b, i, 0)),
                   pl.BlockSpec((1, ROW_TILE, 2 * GROUP_WIDTH), lambda b, i: (b, i, 0))],
        out_shape=[jax.ShapeDtypeStruct((bsz, t, hw), BF16),
                   jax.ShapeDtypeStruct((bsz, t, hw), BF16),
                   jax.ShapeDtypeStruct((bsz, t, 2 * GROUP_WIDTH), BF16)],
        compiler_params=_cparams(("parallel", "parallel")),
        name="mla_up",
    )(p3, p3, p3, *tabs, *args)


def _hgrn_consts():
    c = HGRN_CHUNK
    t = np.arange(c)[:, None]
    s = np.arange(c)[None, :]
    cum = np.stack([s <= t, s >= t]).astype(np.float32)
    masks = np.zeros((2, HGRN_LEVELS + 1, c, c), np.float32)
    masks[:, 0] = np.eye(c)
    for lvl in range(1, HGRN_LEVELS + 1):
        m = c >> lvl
        same = (t // (2 * m)) == (s // (2 * m))
        t_up, s_up = (t // m) % 2 == 1, (s // m) % 2 == 1
        masks[0, lvl] = same & t_up & ~s_up
        masks[1, lvl] = same & ~t_up & s_up
    return jnp.asarray(cum, BF16), jnp.asarray(masks, BF16)


def _hgrn_kernel(q_ref, i_ref, ff_ref, fb_ref, gate_ref, oml_ref, llb_ref, on_ref, cum_ref, mask_ref, y_ref,
                 *scr, n_ctx_chunks, n_chunks):
    c = HGRN_CHUNK
    n_chain = 2 * HGRN_HEADS_PER_STEP
    g_scrs, o_scrs = scr[:n_chain], scr[n_chain:]
    sub = lax.broadcasted_iota(jnp.int32, (8, 128), 0)

    def level_ref(g_scr, d, lvl):
        m = c >> lvl
        blk = 2 * m
        def row(r):
            return g_scr[r:r + 1, :]
        idx = [b * blk + m - 1 + d for b in range(c // blk)]
        if blk >= 8:
            return jnp.concatenate([jnp.broadcast_to(row(r), (blk, 128)) for r in idx], axis=0)
        per = 8 // blk
        tiles = []
        for j in range(c // 8):
            tile = jnp.broadcast_to(row(idx[j * per + per - 1]), (8, 128))
            for u in range(per - 2, -1, -1):
                tile = jnp.where(sub < (u + 1) * blk, jnp.broadcast_to(row(idx[j * per + u]), (8, 128)), tile)
            tiles.append(tile)
        return jnp.concatenate(tiles, axis=0)

    chains = [(hh, d) for hh in range(HGRN_HEADS_PER_STEP) for d in range(2)]

    def chunk_rows(n):
        n_lat_chunks = n_chunks - n_ctx_chunks
        cf = jnp.where(n < n_ctx_chunks, n_lat_chunks + n, n - n_ctx_chunks)
        cb = n_chunks - 1 - n
        return [pl.ds(pl.multiple_of(cf * c, c), c), pl.ds(pl.multiple_of(cb * c, c), c)]

    def gate_decay_parts(n):
        rows_d = chunk_rows(n)
        out = []
        for hh, d in chains:
            cols = slice(hh * HEAD_DIM, (hh + 1) * HEAD_DIM)
            f = (ff_ref if d == 0 else fb_ref)[0, rows_d[d], cols].astype(F32)
            a = llb_ref[d:d + 1, cols] - f
            lf = (jnp.minimum(f, 0.0) - jnp.log(1.0 + jnp.exp(-jnp.abs(f)))
                  + jnp.maximum(a, 0.0) + jnp.log(1.0 + jnp.exp(-jnp.abs(a))))
            hi = lf.astype(BF16)
            r1 = lf - hi.astype(F32)
            mid = r1.astype(BF16)
            lo = (r1 - mid.astype(F32)).astype(BF16)
            cum = cum_ref[d]
            out.append((_dot(cum, hi), _dot(cum, mid), _dot(cum, lo)))
        return out

    def gate_qk(n):
        rows_d = chunk_rows(n)
        out = []
        for hh, d in chains:
            cols = slice(hh * HEAD_DIM, (hh + 1) * HEAD_DIM)
            qv = q_ref[0, rows_d[d], cols].astype(F32)
            qb = (_silu(qv) * (HEAD_DIM ** -0.5)).astype(BF16)
            f = (ff_ref if d == 0 else fb_ref)[0, rows_d[d], cols].astype(F32)
            kb = (oml_ref[d:d + 1, cols] / (1.0 + jnp.exp(f))).astype(BF16)
            out.append((qb, kb))
        return out

    def gate_join(qk, parts):
        return [(qb, kb, p0 + p1 + p2) for (qb, kb), (p0, p1, p2) in zip(qk, parts)]

    def body(n, carry):
        states, cur = carry
        n_next = jnp.minimum(n + 1, n_chunks - 1)
        rows_d = chunk_rows(n)
        vb, gtot, o, scores, o_intra, st_upd = {}, {}, {}, {}, {}, {}
        new = [None] * n_chain
        for ci, (hh, d) in enumerate(chains):
            g_scrs[ci][...] = cur[ci][2]

        def scores_of(group):
            for ci in group:
                hh, d = chains[ci]
                qb, kb, g = cur[ci]
                vb[ci] = i_ref[0, rows_d[d], hh * HEAD_DIM:(hh + 1) * HEAD_DIM]
                end = (c - 1) * (1 - d)
                gtot[ci] = g_scrs[ci][end:end + 1, :]
                o[ci] = _dot_nt(qb * jnp.exp(g).astype(BF16), states[ci].astype(BF16))
                scores[ci] = _dot_nt(qb, kb).astype(BF16) * mask_ref[d, 0]
            for lvl in range(1, HGRN_LEVELS + 1):
                for ci in group:
                    d = chains[ci][1]
                    qb, kb, g = cur[ci]
                    e = jnp.exp(-jnp.abs(g - level_ref(g_scrs[ci], d, lvl))).astype(BF16)
                    s_l = _dot_nt(qb * e, kb * e)
                    scores[ci] = scores[ci] + s_l.astype(BF16) * mask_ref[d, lvl]

        def issue_last(group):
            for ci in group:
                qb, kb, g = cur[ci]
                o_intra[ci] = _dot(scores[ci], vb[ci])
                kend = kb * jnp.exp(gtot[ci] - g).astype(BF16)
                vt = vb[ci].astype(F32).T.astype(BF16)
                st_upd[ci] = _dot(vt, kend)

        def finish(group):
            for ci in group:
                o_scrs[ci][rows_d[chains[ci][1]], :] = o[ci] + o_intra[ci]
                new[ci] = states[ci] * jnp.exp(gtot[ci]) + st_upd[ci]

        every = range(n_chain)
        scores_of(every)
        nxt_parts = gate_decay_parts(n_next)
        issue_last(every)
        nxt_qk = gate_qk(n_next)
        finish(every)
        return tuple(new), gate_join(nxt_qk, nxt_parts)

    zero = jnp.zeros((HEAD_DIM, HEAD_DIM), F32)
    lax.fori_loop(0, n_chunks, body, ((zero,) * n_chain, gate_join(gate_qk(0), gate_decay_parts(0))), unroll=4)

    n_row_tiles = (n_chunks * c) // ROW_TILE

    def readout(i, carry):
        rows = pl.ds(pl.multiple_of(i * ROW_TILE, ROW_TILE), ROW_TILE)
        for hh in range(HGRN_HEADS_PER_STEP):
            cols = slice(hh * HEAD_DIM, (hh + 1) * HEAD_DIM)
            o = o_scrs[2 * hh][rows, :] + o_scrs[2 * hh + 1][rows, :]
            o = o * lax.rsqrt(jnp.mean(o * o, axis=-1, keepdims=True) + EPS) * on_ref[:, cols]
            y_ref[0, rows, cols] = (o * _silu(gate_ref[0, rows, cols].astype(F32))).astype(y_ref.dtype)
        return carry

    lax.fori_loop(0, n_row_tiles, readout, 0)


def _hgrn(p3, one_minus_lb, log_lb, o_norm, ctx_len):
    bsz, t, _ = p3.shape
    cum, masks = _hgrn_consts()
    bw = HGRN_HEADS_PER_STEP * HEAD_DIM
    n_chain = 2 * HGRN_HEADS_PER_STEP

    def col(off):
        return pl.BlockSpec((1, t, bw), lambda b, h: (b, 0, off // bw + h))

    return pl.pallas_call(
        functools.partial(_hgrn_kernel, n_ctx_chunks=ctx_len // HGRN_CHUNK, n_chunks=t // HGRN_CHUNK),
        grid=(bsz, GROUP_WIDTH // bw),
        in_specs=[col(B_Q), col(B_I), col(B_FF), col(B_FB), col(B_G),
                  pl.BlockSpec((2, bw), lambda b, h: (0, h)),
                  pl.BlockSpec((2, bw), lambda b, h: (0, h)),
                  pl.BlockSpec((1, bw), lambda b, h: (0, h)),
                  pl.BlockSpec(cum.shape, lambda b, h: (0, 0, 0)),
                  pl.BlockSpec(masks.shape, lambda b, h: (0, 0, 0, 0))],
        out_specs=pl.BlockSpec((1, t, bw), lambda b, h: (b, 0, h)),
        out_shape=jax.ShapeDtypeStruct((bsz, t, GROUP_WIDTH), BF16),
        scratch_shapes=([pltpu.VMEM((HGRN_CHUNK, HEAD_DIM), F32)] * n_chain
                        + [pltpu.VMEM((t, HEAD_DIM), F32)] * n_chain),
        compiler_params=_cparams(("parallel", "parallel")),
        name="hgrn2",
    )(p3, p3, p3, p3, p3, one_minus_lb, log_lb, o_norm.reshape(1, -1), cum, masks)


def _conv_kernel(u_ref, up_ref, un_ref, gl_ref, glp_ref, gln_ref, gate_ref, w_ref, cb_ref, lw_ref, lb_ref, y_ref,
                 xs_scr, acc_scr, *, n_lat):
    t = pl.program_id(1)
    h = CONV_HALO
    prev_ok = jnp.logical_and(t >= 1, t < n_lat).astype(F32)
    next_ok = (t < n_lat - 1).astype(F32)
    n_lane_blocks = GROUP_WIDTH // 128
    x_prev = up_ref[0].astype(F32) * jax.nn.sigmoid(glp_ref[0].astype(F32)) * prev_ok
    x_main = u_ref[0].astype(F32) * jax.nn.sigmoid(gl_ref[0].astype(F32))
    x_next = un_ref[0].astype(F32) * jax.nn.sigmoid(gln_ref[0].astype(F32)) * next_ok
    for cb in range(n_lane_blocks):
        sl = slice(cb * 128, (cb + 1) * 128)
        xs_scr[cb, 0:h, :] = x_prev[:, sl]
        xs_scr[cb, h:h + ROW_TILE, :] = x_main[:, sl]
        xs_scr[cb, h + ROW_TILE:, :] = x_next[:, sl]
    rc = 64
    base = h - CONV_WIDTH // 2

    def lane_block(cb, carry):
        for r in range(ROW_TILE // rc):
            acc = jnp.zeros((rc, 128), F32)
            for tap in range(CONV_WIDTH):
                lo = base + r * rc + tap
                acc = acc + xs_scr[cb, lo:lo + rc, :] * w_ref[cb, tap:tap + 1, :]
            acc_scr[cb, r * rc:(r + 1) * rc, :] = acc
        return carry

    lax.fori_loop(0, n_lane_blocks, lane_block, 0)
    y = jnp.concatenate([acc_scr[cb] for cb in range(n_lane_blocks)], axis=1) + cb_ref[...]
    mu = jnp.mean(y, axis=-1, keepdims=True)
    yc = y - mu
    var = jnp.mean(yc * yc, axis=-1, keepdims=True)
    z = yc * lax.rsqrt(var + EPS) * lw_ref[...] + lb_ref[...]
    y_ref[0] = (_silu(z) * _silu(gate_ref[0].astype(F32))).astype(y_ref.dtype)


def _conv(p3, conv_w, conv_b, ln_w, ln_b, ctx_out):
    bsz, t, _ = p3.shape
    n_lat = t // ROW_TILE - 1
    nt = n_lat + (1 if ctx_out else 0)
    hpt = ROW_TILE // CONV_HALO
    n_halo = t // CONV_HALO
    n_lane_blocks = GROUP_WIDTH // 128
    wpad = jnp.pad(conv_w, ((0, 32 - CONV_WIDTH), (0, 0))).reshape(32, n_lane_blocks, 128).transpose(1, 0, 2)

    def main(off):
        return pl.BlockSpec((1, ROW_TILE, GROUP_WIDTH), lambda b, i: (b, i, off // GROUP_WIDTH))

    def prev(off):
        return pl.BlockSpec((1, CONV_HALO, GROUP_WIDTH),
                            lambda b, i: (b, jnp.maximum(i * hpt - 1, 0), off // GROUP_WIDTH))

    def nxt(off):
        return pl.BlockSpec((1, CONV_HALO, GROUP_WIDTH),
                            lambda b, i: (b, jnp.minimum((i + 1) * hpt, n_halo - 1), off // GROUP_WIDTH))

    vec = pl.BlockSpec((1, GROUP_WIDTH), lambda b, i: (0, 0))
    return pl.pallas_call(
        functools.partial(_conv_kernel, n_lat=n_lat),
        grid=(bsz, nt),
        in_specs=[main(C_U), prev(C_U), nxt(C_U), main(C_GLU), prev(C_GLU), nxt(C_GLU), main(C_G),
                  pl.BlockSpec((n_lane_blocks, 32, 128), lambda b, i: (0, 0, 0)), vec, vec, vec],
        out_specs=pl.BlockSpec((1, ROW_TILE, GROUP_WIDTH), lambda b, i: (b, i, 0)),
        out_shape=jax.ShapeDtypeStruct((bsz, nt * ROW_TILE, GROUP_WIDTH), BF16),
        scratch_shapes=[pltpu.VMEM((n_lane_blocks, ROW_TILE + 2 * CONV_HALO, 128), F32),
                        pltpu.VMEM((n_lane_blocks, ROW_TILE, 128), F32)],
        compiler_params=_cparams(("parallel", "parallel")),
        name="conformer_conv",
    )(p3, p3, p3, p3, p3, p3, p3, wpad, conv_b.reshape(1, -1), ln_w.reshape(1, -1), ln_b.reshape(1, -1))


def _out_kernel(ya_ref, yb_ref, yc_ref, yd_ref, w_ref, x_ref, gate_ref, o_ref, wb_scr, *, ctx_rows):
    b = pl.program_id(1)
    gw = GROUP_WIDTH

    @pl.when(jnp.logical_and(b == 0, pl.program_id(2) == 0))
    def _():
        wb_scr[...] = w_ref[0].astype(BF16)

    acc = _dot(ya_ref[0], wb_scr[0:gw, :])
    acc = acc + _dot(yb_ref[0], wb_scr[gw:2 * gw, :])
    acc = acc + _dot(yc_ref[0], wb_scr[2 * gw:3 * gw, :])
    acc = acc + _dot(yd_ref[0], wb_scr[3 * gw:4 * gw, :])

    row = 2 if ctx_rows else b
    o_ref[0] = x_ref[0] + gate_ref[pl.ds(row, 1), :] * acc


def _out_proj(ys, w_out, layer, x, mod, ctx_rows):
    bsz, rows, d = x.shape
    tm = ROW_TILE if ctx_rows else _pick_tile(rows, (1024, 512, 256))
    y_off = (ys[0].shape[1] - rows) // tm if ctx_rows else 0
    tn = 512
    y_spec = pl.BlockSpec((1, tm, GROUP_WIDTH), lambda j, b, i: (b, i + y_off, 0))
    return pl.pallas_call(
        functools.partial(_out_kernel, ctx_rows=ctx_rows),
        grid=(d // tn, bsz, rows // tm),
        in_specs=[y_spec, y_spec, y_spec, y_spec,
                  pl.BlockSpec((1, 4 * GROUP_WIDTH, tn), lambda j, b, i: (layer, 0, j)),
                  pl.BlockSpec((1, tm, tn), lambda j, b, i: (b, i, j)),
                  pl.BlockSpec((8, tn), lambda j, b, i: (0, 2 * d // tn + j))],
        out_specs=pl.BlockSpec((1, tm, tn), lambda j, b, i: (b, i, j)),
        out_shape=jax.ShapeDtypeStruct((bsz, rows, d), F32),
        scratch_shapes=[pltpu.VMEM((4 * GROUP_WIDTH, tn), BF16)],
        compiler_params=_cparams(("parallel", "arbitrary", "arbitrary")),
        name="out_proj",
    )(*ys, w_out, x, mod)


D_PACK_ROWS = 64


def _pack_d_kernel(w_ref, o_ref, *, n_src_blocks):
    @pl.when(pl.program_id(0) < n_src_blocks)
    def _():
        o_ref[...] = w_ref[0].astype(BF16)

    @pl.when(pl.program_id(0) >= n_src_blocks)
    def _():
        o_ref[...] = jnp.zeros(o_ref.shape, o_ref.dtype)


def _group_d_weight(w_in_t, layer):
    k = w_in_t.shape[2]
    r = D_PACK_ROWS
    base = ABC_WIDTH // r
    src_cq, src_ckv, src_kr, src_g = base, base + 768 // r, base + 1280 // r, base + 1344 // r
    n_src_blocks = (D_KR + MLA_ROPE_DIM) // r

    def src(j):
        return jnp.where(j < D_CKV // r, src_g + j,
                         jnp.where(j < D_CQ // r, src_ckv + j - D_CKV // r,
                                   jnp.where(j < D_KR // r, src_cq + j - D_CQ // r, src_kr)))

    return pl.pallas_call(
        functools.partial(_pack_d_kernel, n_src_blocks=n_src_blocks),
        grid=(D_WIDTH // r,),
        in_specs=[pl.BlockSpec((1, r, k), lambda j: (layer, src(j), 0))],
        out_specs=pl.BlockSpec((r, k), lambda j: (j, 0)),
        out_shape=jax.ShapeDtypeStruct((D_WIDTH, k), BF16),
        compiler_params=_cparams(("parallel",)),
        name="pack_group_d",
    )(w_in_t)


def kernel(x, c, ctx, c_ctx, w_mod, b_mod, norm_w, w_in, w_out, att_q_norm, att_k_norm, hgrn_lb_logits, hgrn_o_norm,
           conv_w, conv_b, conv_ln_w, conv_ln_b, mla_q_norm, mla_kv_norm, mla_w_uq, mla_w_ukv, mla_qk_q_norm,
           mla_qk_k_norm):
    bsz, seq, d = x.shape
    ctx_len = ctx.shape[1]
    depth = w_mod.shape[0]
    t = ctx_len + seq
    assert ctx_len == ROW_TILE and seq % ROW_TILE == 0 and bsz <= 2
    n_tiles = t // ROW_TILE

    lb_all = jnp.cumsum(jax.nn.softmax(hgrn_lb_logits.astype(F32), axis=0), axis=0)
    lb_all = lb_all - lb_all[0:1]

    c8 = jnp.zeros((8, d), F32).at[0:bsz].set(c).at[2].set(c_ctx)
    mod_all = _modulation(c8, w_mod, b_mod)

    gqa_tabs = _rope_tables(seq, ctx_len, HEAD_DIM)
    mla_tabs = _rope_tables(seq, ctx_len, MLA_ROPE_DIM)

    w_in_t = jnp.swapaxes(w_in, 1, 2)

    xc, xl = ctx, x
    for l in range(depth):
        ctx_out = l < depth - 1
        mod = mod_all[l]
        h = _norm_modulate(xc, xl, mod, norm_w[l], n_tiles)
        h2 = h.reshape(bsz * t, d)
        p3 = _in_proj_abc(h2, w_in_t, l).reshape(bsz, t, ABC_WIDTH)
        pd3 = _matmul_nt(h2, _group_d_weight(w_in_t, l), BF16).reshape(bsz, t, D_WIDTH)

        qa, ka, va = _gqa_prep(p3, gqa_tabs, att_q_norm[l], att_k_norm[l])
        ya = _attention(qa, ka, va, p3, A_G, N_HEADS // ATT_KV_HEADS, HEAD_DIM, True, ctx_out)

        yb = _hgrn(p3, 1.0 - lb_all[l], jnp.log(lb_all[l]), hgrn_o_norm[l], ctx_len)

        yc = _conv(p3, conv_w[l], conv_b[l], conv_ln_w[l], conv_ln_b[l], ctx_out)

        wq = jnp.pad(mla_w_uq[l].reshape(MLA_Q_RANK, N_HEADS, MLA_QK_DIM),
                     ((0, 0), (0, 0), (0, MLA_PAD_DIM - MLA_QK_DIM))).reshape(MLA_Q_RANK, -1).astype(BF16)
        wkv = mla_w_ukv[l].reshape(MLA_KV_RANK, N_HEADS, 2 * HEAD_DIM)
        wk = wkv[:, :, :HEAD_DIM].reshape(MLA_KV_RANK, -1).astype(BF16)
        wv = wkv[:, :, HEAD_DIM:].reshape(MLA_KV_RANK, -1).astype(BF16)
        qn = jnp.pad(mla_qk_q_norm[l], (0, MLA_PAD_DIM - MLA_QK_DIM))
        kn = jnp.pad(mla_qk_k_norm[l], (0, MLA_PAD_DIM - MLA_QK_DIM))
        qd, kd, vd = _mla_up(pd3, mla_tabs, mla_q_norm[l], mla_kv_norm[l], wq, wk, wv, qn, kn)
        yd = _attention(qd, kd, vd, pd3, D_G, 4, MLA_PAD_DIM, False, ctx_out)

        ys = (ya, yb, yc, yd)
        xl_new = _out_proj(ys, w_out, l, xl, mod, False)
        if ctx_out:
            xc = _out_proj(ys, w_out, l, xc, mod, True)
        xl = xl_new
    return xl
```

```python
import functools

import numpy as np
import jax
import jax.numpy as jnp
from jax import lax
from jax.experimental import pallas as pl
from jax.experimental.pallas import tpu as pltpu

F32 = jnp.float32
BF16 = jnp.bfloat16

EPS = 1e-6
ROPE_THETA = 10000.0
GRID_W = 64
HEAD_DIM = 128
GROUP_WIDTH = 1024
N_HEADS = 8
ATT_KV_HEADS = 2
MLA_Q_RANK = 768
MLA_KV_RANK = 512
MLA_ROPE_DIM = 64
MLA_QK_DIM = 192
MLA_PAD_DIM = 256
CONV_WIDTH = 31
CONV_HALO = 16
HGRN_CHUNK = 128
HGRN_LEVELS = 7
HGRN_HEADS_PER_STEP = 2
ROW_TILE = 256
NORM_SUB_ROWS = 16

IN_TILE = 512
C_U, C_GLU, C_G = 0, 1024, 2048
A_Q, A_K, A_V, A_G = 3072, 4096, 4352, 4608
B_Q, B_I, B_FF, B_FB, B_G = 5632, 6656, 7680, 8704, 9728
ABC_WIDTH = 10752
D_G, D_CKV, D_CQ, D_KR = 0, 1024, 1536, 2304
D_WIDTH = 2560

VMEM_LIMIT = 56 * 1024 * 1024


def _cparams(sem):
    return pltpu.CompilerParams(dimension_semantics=sem, vmem_limit_bytes=VMEM_LIMIT)


def _silu(x):
    return x * jax.nn.sigmoid(x)


def _dot(a, b):
    return jnp.dot(a, b, preferred_element_type=F32)


def _dot_nt(a, b):
    return lax.dot_general(a, b, (((1,), (1,)), ((), ())), preferred_element_type=F32)


def _mod_kernel(c_ref, w_ref, b_ref, o_ref):
    a = _silu(c_ref[...]).astype(BF16)
    o_ref[0] = _dot(a, w_ref[0].astype(BF16)) + b_ref[0]


def _modulation(c8, w_mod, b_mod):
    depth, d, n = w_mod.shape
    tn = 512
    return pl.pallas_call(
        _mod_kernel,
        grid=(depth, n // tn),
        in_specs=[pl.BlockSpec((8, d), lambda l, j: (0, 0)),
                  pl.BlockSpec((1, d, tn), lambda l, j: (l, 0, j)),
                  pl.BlockSpec((1, 1, tn), lambda l, j: (l, 0, j))],
        out_specs=pl.BlockSpec((1, 8, tn), lambda l, j: (l, 0, j)),
        out_shape=jax.ShapeDtypeStruct((depth, 8, n), F32),
        compiler_params=_cparams(("parallel", "parallel")),
        name="modulation",
    )(c8, w_mod, b_mod.reshape(depth, 1, n))


def _norm_kernel(xc_ref, xl_ref, mod_ref, nw_ref, o_ref, *, d, n_lat):
    b = pl.program_id(0)
    t = pl.program_id(1)

    def body(x_ref, row):
        m = mod_ref[pl.ds(row, 1), :]
        shift = m[:, :d]
        gain = nw_ref[...] * (1.0 + m[:, d:2 * d])

        def sub_tile(r, carry):
            rows = pl.ds(pl.multiple_of(r * NORM_SUB_ROWS, NORM_SUB_ROWS), NORM_SUB_ROWS)
            x = x_ref[0, rows, :]
            ms = jnp.mean(x * x, axis=-1, keepdims=True)
            o_ref[0, rows, :] = (x * lax.rsqrt(ms + EPS) * gain + shift).astype(o_ref.dtype)
            return carry

        lax.fori_loop(0, ROW_TILE // NORM_SUB_ROWS, sub_tile, 0, unroll=4)

    @pl.when(t == n_lat)
    def _():
        body(xc_ref, 2)

    @pl.when(t < n_lat)
    def _():
        body(xl_ref, b)


def _norm_modulate(xc, xl, mod, nw, n_tiles):
    bsz, _, d = xc.shape
    n_lat = n_tiles - 1
    return pl.pallas_call(
        functools.partial(_norm_kernel, d=d, n_lat=n_lat),
        grid=(bsz, n_tiles),
        in_specs=[pl.BlockSpec((1, ROW_TILE, d), lambda b, t: (b, 0, 0)),
                  pl.BlockSpec((1, ROW_TILE, d), lambda b, t: (b, jnp.minimum(t, n_lat - 1), 0)),
                  pl.BlockSpec((8, 3 * d), lambda b, t: (0, 0)),
                  pl.BlockSpec((1, d), lambda b, t: (0, 0))],
        out_specs=pl.BlockSpec((1, ROW_TILE, d), lambda b, t: (b, t, 0)),
        out_shape=jax.ShapeDtypeStruct((bsz, n_tiles * ROW_TILE, d), BF16),
        compiler_params=_cparams(("parallel", "parallel")),
        name="norm_modulate",
    )(xc, xl, mod, nw.reshape(1, d))


def _mm_kernel(a_ref, w_ref, o_ref):
    o_ref[...] = _dot_nt(a_ref[...], w_ref[...]).astype(o_ref.dtype)


def _pick_tile(n, cands):
    for c in cands:
        if n % c == 0:
            return c
    raise ValueError(f"no tile for {n}")


def _matmul_nt(a, wt, out_dtype):
    m, k = a.shape
    n, _ = wt.shape
    tm = _pick_tile(m, (1088, 1024, 768, 512, 256))
    tn = _pick_tile(n, (512, 256, 128))
    return pl.pallas_call(
        _mm_kernel,
        grid=(m // tm, n // tn),
        in_specs=[pl.BlockSpec((tm, k), lambda i, j: (i, 0)),
                  pl.BlockSpec((tn, k), lambda i, j: (j, 0))],
        out_specs=pl.BlockSpec((tm, tn), lambda i, j: (i, j)),
        out_shape=jax.ShapeDtypeStruct((m, n), out_dtype),
        compiler_params=_cparams(("parallel", "parallel")),
        name="in_proj_d",
    )(a, wt)


def _mm_wcast_kernel(a_ref, w_ref, o_ref, wb_scr):
    @pl.when(pl.program_id(1) == 0)
    def _():
        wb_scr[...] = w_ref[0].astype(BF16)

    o_ref[...] = _dot_nt(a_ref[...], wb_scr[...]).astype(o_ref.dtype)


def _in_proj_abc(a, w_in_t, layer):
    m, k = a.shape
    tm = _pick_tile(m, (1088, 1024, 768, 512, 256))
    n_tiles = ABC_WIDTH // IN_TILE
    n_c = (C_G + GROUP_WIDTH) // IN_TILE
    return pl.pallas_call(
        _mm_wcast_kernel,
        grid=(n_tiles, m // tm),
        in_specs=[pl.BlockSpec((tm, k), lambda j, i: (i, 0)),
                  pl.BlockSpec((1, IN_TILE, k),
                               lambda j, i: (layer, jnp.where(j < n_c, j + n_tiles - n_c, j - n_c), 0))],
        out_specs=pl.BlockSpec((tm, IN_TILE), lambda j, i: (i, j)),
        out_shape=jax.ShapeDtypeStruct((m, ABC_WIDTH), BF16),
        scratch_shapes=[pltpu.VMEM((IN_TILE, k), BF16)],
        compiler_params=_cparams(("parallel", "arbitrary")),
        name="in_proj_abc",
    )(a, w_in_t)


def _rope_tables(seq, ctx_len, rope_dim):
    quarter = rope_dim // 4
    inv = (np.float32(ROPE_THETA) ** (-np.arange(quarter, dtype=np.float32) / np.float32(quarter))).astype(np.float32)
    pos = np.arange(seq, dtype=np.int32)
    ar = (pos // GRID_W).astype(np.float32)[:, None] * inv
    ac = (pos % GRID_W).astype(np.float32)[:, None] * inv
    cos = np.concatenate([np.cos(ar), np.cos(ar), np.cos(ac), np.cos(ac)], axis=-1)
    sin = np.concatenate([np.sin(ar), np.sin(ar), np.sin(ac), np.sin(ac)], axis=-1)

    def full(tab, fill):
        tab = np.pad(tab, ((0, 0), (0, 128 - rope_dim)), constant_values=fill)
        return jnp.asarray(np.concatenate([tab, np.full((ctx_len, 128), fill, np.float32)], axis=0), F32)

    swap = np.zeros((128, 128), np.float32)
    for i in range(rope_dim):
        if (i // quarter) % 2 == 0:
            swap[i + quarter, i] = -1.0
        else:
            swap[i - quarter, i] = 1.0
    return full(cos, 1.0), full(sin, 0.0), jnp.asarray(swap, BF16)


def _rope(x, cos, sin, swap):
    return x * cos + _dot(x.astype(BF16), swap) * sin


def _gqa_prep_kernel(q_ref, k_ref, v_ref, cos_ref, sin_ref, swap_ref, qn_ref, kn_ref, qo_ref, ko_ref, vo_ref):
    cos, sin, swap = cos_ref[...], sin_ref[...], swap_ref[...]

    def head(x, w, scale):
        ms = jnp.mean(x * x, axis=-1, keepdims=True)
        y = x * lax.rsqrt(ms + EPS) * w
        return (_rope(y, cos, sin, swap) * scale).astype(BF16)

    for h in range(N_HEADS):
        sl = slice(h * HEAD_DIM, (h + 1) * HEAD_DIM)
        qo_ref[0, :, sl] = head(q_ref[0, :, sl].astype(F32), qn_ref[...], HEAD_DIM ** -0.5)
    for h in range(ATT_KV_HEADS):
        sl = slice(h * HEAD_DIM, (h + 1) * HEAD_DIM)
        ko_ref[0, :, sl] = head(k_ref[0, :, sl].astype(F32), kn_ref[...], 1.0)
        _store_value_ext(vo_ref, h, v_ref[0, :, sl])


def _store_value_ext(vo_ref, h, v):
    vo_ref[0, :, 2 * h * HEAD_DIM:(2 * h + 1) * HEAD_DIM] = v
    vo_ref[0, :, (2 * h + 1) * HEAD_DIM:(2 * h + 2) * HEAD_DIM] = jnp.ones(v.shape, v.dtype)


def _gqa_prep(p3, tabs, qn, kn):
    bsz, t, _ = p3.shape
    nt = t // ROW_TILE
    kvw = ATT_KV_HEADS * HEAD_DIM
    tab_spec = pl.BlockSpec((ROW_TILE, 128), lambda b, i: (i, 0))
    swap_spec = pl.BlockSpec((128, 128), lambda b, i: (0, 0))
    vec_spec = pl.BlockSpec((1, 128), lambda b, i: (0, 0))
    return pl.pallas_call(
        _gqa_prep_kernel,
        grid=(bsz, nt),
        in_specs=[pl.BlockSpec((1, ROW_TILE, GROUP_WIDTH), lambda b, i: (b, i, A_Q // GROUP_WIDTH)),
                  pl.BlockSpec((1, ROW_TILE, kvw), lambda b, i: (b, i, A_K // kvw)),
                  pl.BlockSpec((1, ROW_TILE, kvw), lambda b, i: (b, i, A_V // kvw)),
                  tab_spec, tab_spec, swap_spec, vec_spec, vec_spec],
        out_specs=[pl.BlockSpec((1, ROW_TILE, GROUP_WIDTH), lambda b, i: (b, i, 0)),
                   pl.BlockSpec((1, ROW_TILE, kvw), lambda b, i: (b, i, 0)),
                   pl.BlockSpec((1, ROW_TILE, 2 * kvw), lambda b, i: (b, i, 0))],
        out_shape=[jax.ShapeDtypeStruct((bsz, t, GROUP_WIDTH), BF16),
                   jax.ShapeDtypeStruct((bsz, t, kvw), BF16),
                   jax.ShapeDtypeStruct((bsz, t, 2 * kvw), BF16)],
        compiler_params=_cparams(("parallel", "parallel")),
        name="gqa_prep",
    )(p3, p3, p3, *tabs, qn.reshape(1, 128), kn.reshape(1, 128))


def _attn_kernel(q_ref, k_ref, v_ref, g_ref, o_ref, *, n_heads, dq, shared_kv, n_lat):
    i = pl.program_id(2)
    vw = 2 * HEAD_DIM

    def attend(lo):
        def kv(j):
            if shared_kv:
                return k_ref[0, lo:, :], v_ref[0, lo:, :]
            return k_ref[0, lo:, j * dq:(j + 1) * dq], v_ref[0, lo:, j * vw:(j + 1) * vw]

        s = [None] * n_heads
        p = [None] * n_heads
        for j in range(n_heads + 2):
            if j < n_heads:
                s[j] = _dot_nt(q_ref[0, :, j * dq:(j + 1) * dq], kv(j)[0])
            if 1 <= j <= n_heads:
                m = jnp.max(s[j - 1], axis=-1, keepdims=True)
                p[j - 1] = jnp.exp((s[j - 1] - m).astype(BF16))
                s[j - 1] = None
            if j >= 2:
                oe = _dot(p[j - 2], kv(j - 2)[1])
                p[j - 2] = None
                o = oe[:, :HEAD_DIM] / oe[:, HEAD_DIM:]
                sl = slice((j - 2) * HEAD_DIM, (j - 1) * HEAD_DIM)
                o_ref[0, :, sl] = (o * _silu(g_ref[0, :, sl].astype(F32))).astype(o_ref.dtype)

    @pl.when(i < n_lat)
    def _():
        attend(0)

    @pl.when(i == n_lat)
    def _():
        attend(n_lat * ROW_TILE)


def _attention(q, k, v_ext, p3, gate_off, n_heads, dq, shared_kv, ctx_out):
    bsz, t, _ = q.shape
    n_lat = t // ROW_TILE - 1
    nt = n_lat + (1 if ctx_out else 0)
    n_kv = 1 if shared_kv else n_heads
    gw = n_heads * HEAD_DIM
    n_steps = q.shape[-1] // (n_heads * dq)
    return pl.pallas_call(
        functools.partial(_attn_kernel, n_heads=n_heads, dq=dq, shared_kv=shared_kv, n_lat=n_lat),
        grid=(bsz, n_steps, nt),
        in_specs=[pl.BlockSpec((1, ROW_TILE, n_heads * dq), lambda b, h, i: (b, i, h)),
                  pl.BlockSpec((1, t, n_kv * dq), lambda b, h, i: (b, 0, h)),
                  pl.BlockSpec((1, t, n_kv * 2 * HEAD_DIM), lambda b, h, i: (b, 0, h)),
                  pl.BlockSpec((1, ROW_TILE, gw), lambda b, h, i: (b, i, gate_off // gw + h))],
        out_specs=pl.BlockSpec((1, ROW_TILE, gw), lambda b, h, i: (b, i, h)),
        out_shape=jax.ShapeDtypeStruct((bsz, nt * ROW_TILE, n_steps * gw), BF16),
        compiler_params=_cparams(("parallel", "parallel", "parallel")),
        name="attention",
    )(q, k, v_ext, p3)


def _mla_up_kernel(cq_ref, ckv_ref, kr_ref, cos_ref, sin_ref, swap_ref, cqn_ref, ckvn_ref, wq_ref, wk_ref, wv_ref,
                   qn_ref, kn_ref, qo_ref, ko_ref, vo_ref):
    cos, sin, swap = cos_ref[...], sin_ref[...], swap_ref[...]

    def rms(x, w):
        return x * lax.rsqrt(jnp.mean(x * x, axis=-1, keepdims=True) + EPS) * w

    cq = rms(cq_ref[0].astype(F32), cqn_ref[...]).astype(BF16)
    ckv = rms(ckv_ref[0].astype(F32), ckvn_ref[...]).astype(BF16)
    qf = _dot(cq, wq_ref[...])
    kf = _dot(ckv, wk_ref[...])
    vf = _dot(ckv, wv_ref[...]).astype(BF16)
    kr = kr_ref[0].astype(F32)
    kr_ss = jnp.sum(kr * kr, axis=-1, keepdims=True)
    qw1, qw2 = qn_ref[:, :128], qn_ref[:, 128:]
    kw1, kw2 = kn_ref[:, :128], kn_ref[:, 128:]
    kr_rot = _rope(kr * kw2, cos, sin, swap)
    scale = MLA_QK_DIM ** -0.5
    for h in range(N_HEADS):
        lo = h * MLA_PAD_DIM
        q1 = qf[:, lo:lo + 128]
        q2 = qf[:, lo + 128:lo + 256]
        ss = jnp.sum(q1 * q1 + q2 * q2, axis=-1, keepdims=True)
        r = lax.rsqrt(ss / MLA_QK_DIM + EPS) * scale
        qo_ref[0, :, lo:lo + 128] = (q1 * r * qw1).astype(BF16)
        qo_ref[0, :, lo + 128:lo + 256] = _rope(q2 * r * qw2, cos, sin, swap).astype(BF16)
        k1 = kf[:, h * 128:(h + 1) * 128]
        r = lax.rsqrt((jnp.sum(k1 * k1, axis=-1, keepdims=True) + kr_ss) / MLA_QK_DIM + EPS)
        ko_ref[0, :, lo:lo + 128] = (k1 * r * kw1).astype(BF16)
        ko_ref[0, :, lo + 128:lo + 256] = (kr_rot * r).astype(BF16)
        _store_value_ext(vo_ref, h, vf[:, h * 128:(h + 1) * 128])


def _mla_up(p3, tabs, cqn, ckvn, wq, wk, wv, qn, kn):
    bsz, t, _ = p3.shape
    nt = t // ROW_TILE
    hw = N_HEADS * MLA_PAD_DIM
    tab_spec = pl.BlockSpec((ROW_TILE, 128), lambda b, i: (i, 0))

    def whole(a):
        return pl.BlockSpec(a.shape, lambda b, i: (0,) * a.ndim)

    args = (cqn.reshape(1, -1), ckvn.reshape(1, -1), wq, wk, wv, qn.reshape(1, -1), kn.reshape(1, -1))
    return pl.pallas_call(
        _mla_up_kernel,
        grid=(bsz, nt),
        in_specs=[pl.BlockSpec((1, ROW_TILE, MLA_Q_RANK), lambda b, i: (b, i, D_CQ // MLA_Q_RANK)),
                  pl.BlockSpec((1, ROW_TILE, MLA_KV_RANK), lambda b, i: (b, i, D_CKV // MLA_KV_RANK)),
                  pl.BlockSpec((1, ROW_TILE, 128), lambda b, i: (b, i, D_KR // 128)),
                  tab_spec, tab_spec, pl.BlockSpec((128, 128), lambda b, i: (0, 0))] + [whole(a) for a in args],
        out_specs=[pl.BlockSpec((1, ROW_TILE, hw), lambda b, i: (b, i, 0)),
                   pl.BlockSpec((1, ROW_TILE, hw), lambda b, i: (b, i, 0)),
                   pl.BlockSpec((1, ROW_TILE, 2 * GROUP_WIDTH), lambda b, i: (b, i, 0))],
        out_shape=[jax.ShapeDtypeStruct((bsz, t, hw), BF16),
                   jax.ShapeDtypeStruct((bsz, t, hw), BF16),
                   jax.ShapeDtypeStruct((bsz, t, 2 * GROUP_WIDTH), BF16)],
        compiler_params=_cparams(("parallel", "parallel")),
        name="mla_up",
    )(p3, p3, p3, *tabs, *args)


def _hgrn_consts():
    c = HGRN_CHUNK
    t = np.arange(c)[:, None]
    s = np.arange(c)[None, :]
    cum = np.stack([s <= t, s >= t]).astype(np.float32)
    masks = np.zeros((2, HGRN_LEVELS + 1, c, c), np.float32)
    masks[:, 0] = np.eye(c)
    for lvl in range(1, HGRN_LEVELS + 1):
        m = c >> lvl
        same = (t // (2 * m)) == (s // (2 * m))
        t_up, s_up = (t // m) % 2 == 1, (s // m) % 2 == 1
        masks[0, lvl] = same & t_up & ~s_up
        masks[1, lvl] = same & ~t_up & s_up
    return jnp.asarray(cum, BF16), jnp.asarray(masks, BF16)


def _hgrn_kernel(q_ref, i_ref, ff_ref, fb_ref, gate_ref, oml_ref, llb_ref, on_ref, cum_ref, mask_ref, y_ref,
                 *scr, n_ctx_chunks, n_chunks):
    c = HGRN_CHUNK
    n_chain = 2 * HGRN_HEADS_PER_STEP
    g_scrs, o_scrs = scr[:n_chain], scr[n_chain:]
    sub = lax.broadcasted_iota(jnp.int32, (8, 128), 0)

    def level_ref(g_scr, d, lvl):
        m = c >> lvl
        blk = 2 * m
        def row(r):
            return g_scr[r:r + 1, :]
        idx = [b * blk + m - 1 + d for b in range(c // blk)]
        if blk >= 8:
            return jnp.concatenate([jnp.broadcast_to(row(r), (blk, 128)) for r in idx], axis=0)
        per = 8 // blk
        tiles = []
        for j in range(c // 8):
            tile = jnp.broadcast_to(row(idx[j * per + per - 1]), (8, 128))
            for u in range(per - 2, -1, -1):
                tile = jnp.where(sub < (u + 1) * blk, jnp.broadcast_to(row(idx[j * per + u]), (8, 128)), tile)
            tiles.append(tile)
        return jnp.concatenate(tiles, axis=0)

    chains = [(hh, d) for hh in range(HGRN_HEADS_PER_STEP) for d in range(2)]

    def chunk_rows(n):
        n_lat_chunks = n_chunks - n_ctx_chunks
        cf = jnp.where(n < n_ctx_chunks, n_lat_chunks + n, n - n_ctx_chunks)
        cb = n_chunks - 1 - n
        return [pl.ds(pl.multiple_of(cf * c, c), c), pl.ds(pl.multiple_of(cb * c, c), c)]

    def gate_decay_parts(n):
        rows_d = chunk_rows(n)
        out = []
        for hh, d in chains:
            cols = slice(hh * HEAD_DIM, (hh + 1) * HEAD_DIM)
            f = (ff_ref if d == 0 else fb_ref)[0, rows_d[d], cols].astype(F32)
            a = llb_ref[d:d + 1, cols] - f
            lf = (jnp.minimum(f, 0.0) - jnp.log(1.0 + jnp.exp(-jnp.abs(f)))
                  + jnp.maximum(a, 0.0) + jnp.log(1.0 + jnp.exp(-jnp.abs(a))))
            hi = lf.astype(BF16)
            r1 = lf - hi.astype(F32)
            mid = r1.astype(BF16)
            lo = (r1 - mid.astype(F32)).astype(BF16)
            cum = cum_ref[d]
            out.append((_dot(cum, hi), _dot(cum, mid), _dot(cum, lo)))
        return out

    def gate_qk(n):
        rows_d = chunk_rows(n)
        out = []
        for hh, d in chains:
            cols = slice(hh * HEAD_DIM, (hh + 1) * HEAD_DIM)
            qv = q_ref[0, rows_d[d], cols].astype(F32)
            qb = (_silu(qv) * (HEAD_DIM ** -0.5)).astype(BF16)
            f = (ff_ref if d == 0 else fb_ref)[0, rows_d[d], cols].astype(F32)
            kb = (oml_ref[d:d + 1, cols] / (1.0 + jnp.exp(f))).astype(BF16)
            out.append((qb, kb))
        return out

    def gate_join(qk, parts):
        return [(qb, kb, p0 + p1 + p2) for (qb, kb), (p0, p1, p2) in zip(qk, parts)]

    def body(n, carry):
        states, cur = carry
        n_next = jnp.minimum(n + 1, n_chunks - 1)
        rows_d = chunk_rows(n)
        vb, gtot, o, scores, o_intra, st_upd = {}, {}, {}, {}, {}, {}
        new = [None] * n_chain
        for ci, (hh, d) in enumerate(chains):
            g_scrs[ci][...] = cur[ci][2]

        def scores_of(group):
            for ci in group:
                hh, d = chains[ci]
                qb, kb, g = cur[ci]
                vb[ci] = i_ref[0, rows_d[d], hh * HEAD_DIM:(hh + 1) * HEAD_DIM]
                end = (c - 1) * (1 - d)
                gtot[ci] = g_scrs[ci][end:end + 1, :]
                o[ci] = _dot_nt(qb * jnp.exp(g).astype(BF16), states[ci].astype(BF16))
                scores[ci] = _dot_nt(qb, kb).astype(BF16) * mask_ref[d, 0]
            for lvl in range(1, HGRN_LEVELS + 1):
                for ci in group:
                    d = chains[ci][1]
                    qb, kb, g = cur[ci]
                    e = jnp.exp(-jnp.abs(g - level_ref(g_scrs[ci], d, lvl))).astype(BF16)
                    s_l = _dot_nt(qb * e, kb * e)
                    scores[ci] = scores[ci] + s_l.astype(BF16) * mask_ref[d, lvl]

        def issue_last(group):
            for ci in group:
                qb, kb, g = cur[ci]
                o_intra[ci] = _dot(scores[ci], vb[ci])
                kend = kb * jnp.exp(gtot[ci] - g).astype(BF16)
                vt = vb[ci].astype(F32).T.astype(BF16)
                st_upd[ci] = _dot(vt, kend)

        def finish(group):
            for ci in group:
                o_scrs[ci][rows_d[chains[ci][1]], :] = o[ci] + o_intra[ci]
                new[ci] = states[ci] * jnp.exp(gtot[ci]) + st_upd[ci]

        every = range(n_chain)
        scores_of(every)
        nxt_parts = gate_decay_parts(n_next)
        issue_last(every)
        nxt_qk = gate_qk(n_next)
        finish(every)
        return tuple(new), gate_join(nxt_qk, nxt_parts)

    zero = jnp.zeros((HEAD_DIM, HEAD_DIM), F32)
    lax.fori_loop(0, n_chunks, body, ((zero,) * n_chain, gate_join(gate_qk(0), gate_decay_parts(0))), unroll=4)

    n_row_tiles = (n_chunks * c) // ROW_TILE

    def readout(i, carry):
        rows = pl.ds(pl.multiple_of(i * ROW_TILE, ROW_TILE), ROW_TILE)
        for hh in range(HGRN_HEADS_PER_STEP):
            cols = slice(hh * HEAD_DIM, (hh + 1) * HEAD_DIM)
            o = o_scrs[2 * hh][rows, :] + o_scrs[2 * hh + 1][rows, :]
            o = o * lax.rsqrt(jnp.mean(o * o, axis=-1, keepdims=True) + EPS) * on_ref[:, cols]
            y_ref[0, rows, cols] = (o * _silu(gate_ref[0, rows, cols].astype(F32))).astype(y_ref.dtype)
        return carry

    lax.fori_loop(0, n_row_tiles, readout, 0)


def _hgrn(p3, one_minus_lb, log_lb, o_norm, ctx_len):
    bsz, t, _ = p3.shape
    cum, masks = _hgrn_consts()
    bw = HGRN_HEADS_PER_STEP * HEAD_DIM
    n_chain = 2 * HGRN_HEADS_PER_STEP

    def col(off):
        return pl.BlockSpec((1, t, bw), lambda b, h: (b, 0, off // bw + h))

    return pl.pallas_call(
        functools.partial(_hgrn_kernel, n_ctx_chunks=ctx_len // HGRN_CHUNK, n_chunks=t // HGRN_CHUNK),
        grid=(bsz, GROUP_WIDTH // bw),
        in_specs=[col(B_Q), col(B_I), col(B_FF), col(B_FB), col(B_G),
                  pl.BlockSpec((2, bw), lambda b, h: (0, h)),
                  pl.BlockSpec((2, bw), lambda b, h: (0, h)),
                  pl.BlockSpec((1, bw), lambda b, h: (0, h)),
                  pl.BlockSpec(cum.shape, lambda b, h: (0, 0, 0)),
                  pl.BlockSpec(masks.shape, lambda b, h: (0, 0, 0, 0))],
        out_specs=pl.BlockSpec((1, t, bw), lambda b, h: (b, 0, h)),
        out_shape=jax.ShapeDtypeStruct((bsz, t, GROUP_WIDTH), BF16),
        scratch_shapes=([pltpu.VMEM((HGRN_CHUNK, HEAD_DIM), F32)] * n_chain
                        + [pltpu.VMEM((t, HEAD_DIM), F32)] * n_chain),
        compiler_params=_cparams(("parallel", "parallel")),
        name="hgrn2",
    )(p3, p3, p3, p3, p3, one_minus_lb, log_lb, o_norm.reshape(1, -1), cum, masks)


def _conv_kernel(u_ref, up_ref, un_ref, gl_ref, glp_ref, gln_ref, gate_ref, w_ref, cb_ref, lw_ref, lb_ref, y_ref,
                 xs_scr, acc_scr, *, n_lat):
    t = pl.program_id(1)
    h = CONV_HALO
    prev_ok = jnp.logical_and(t >= 1, t < n_lat).astype(F32)
    next_ok = (t < n_lat - 1).astype(F32)
    n_lane_blocks = GROUP_WIDTH // 128
    x_prev = up_ref[0].astype(F32) * jax.nn.sigmoid(glp_ref[0].astype(F32)) * prev_ok
    x_main = u_ref[0].astype(F32) * jax.nn.sigmoid(gl_ref[0].astype(F32))
    x_next = un_ref[0].astype(F32) * jax.nn.sigmoid(gln_ref[0].astype(F32)) * next_ok
    for cb in range(n_lane_blocks):
        sl = slice(cb * 128, (cb + 1) * 128)
        xs_scr[cb, 0:h, :] = x_prev[:, sl]
        xs_scr[cb, h:h + ROW_TILE, :] = x_main[:, sl]
        xs_scr[cb, h + ROW_TILE:, :] = x_next[:, sl]
    rc = 64
    base = h - CONV_WIDTH // 2

    def lane_block(cb, carry):
        for r in range(ROW_TILE // rc):
            acc = jnp.zeros((rc, 128), F32)
            for tap in range(CONV_WIDTH):
                lo = base + r * rc + tap
                acc = acc + xs_scr[cb, lo:lo + rc, :] * w_ref[cb, tap:tap + 1, :]
            acc_scr[cb, r * rc:(r + 1) * rc, :] = acc
        return carry

    lax.fori_loop(0, n_lane_blocks, lane_block, 0)
    y = jnp.concatenate([acc_scr[cb] for cb in range(n_lane_blocks)], axis=1) + cb_ref[...]
    mu = jnp.mean(y, axis=-1, keepdims=True)
    yc = y - mu
    var = jnp.mean(yc * yc, axis=-1, keepdims=True)
    z = yc * lax.rsqrt(var + EPS) * lw_ref[...] + lb_ref[...]
    y_ref[0] = (_silu(z) * _silu(gate_ref[0].astype(F32))).astype(y_ref.dtype)


def _conv(p3, conv_w, conv_b, ln_w, ln_b, ctx_out):
    bsz, t, _ = p3.shape
    n_lat = t // ROW_TILE - 1
    nt = n_lat + (1 if ctx_out else 0)
    hpt = ROW_TILE // CONV_HALO
    n_halo = t // CONV_HALO
    n_lane_blocks = GROUP_WIDTH // 128
    wpad = jnp.pad(conv_w, ((0, 32 - CONV_WIDTH), (0, 0))).reshape(32, n_lane_blocks, 128).transpose(1, 0, 2)

    def main(off):
        return pl.BlockSpec((1, ROW_TILE, GROUP_WIDTH), lambda b, i: (b, i, off // GROUP_WIDTH))

    def prev(off):
        return pl.BlockSpec((1, CONV_HALO, GROUP_WIDTH),
                            lambda b, i: (b, jnp.maximum(i * hpt - 1, 0), off // GROUP_WIDTH))

    def nxt(off):
        return pl.BlockSpec((1, CONV_HALO, GROUP_WIDTH),
                            lambda b, i: (b, jnp.minimum((i + 1) * hpt, n_halo - 1), off // GROUP_WIDTH))

    vec = pl.BlockSpec((1, GROUP_WIDTH), lambda b, i: (0, 0))
    return pl.pallas_call(
        functools.partial(_conv_kernel, n_lat=n_lat),
        grid=(bsz, nt),
        in_specs=[main(C_U), prev(C_U), nxt(C_U), main(C_GLU), prev(C_GLU), nxt(C_GLU), main(C_G),
                  pl.BlockSpec((n_lane_blocks, 32, 128), lambda b, i: (0, 0, 0)), vec, vec, vec],
        out_specs=pl.BlockSpec((1, ROW_TILE, GROUP_WIDTH), lambda b, i: (b, i, 0)),
        out_shape=jax.ShapeDtypeStruct((bsz, nt * ROW_TILE, GROUP_WIDTH), BF16),
        scratch_shapes=[pltpu.VMEM((n_lane_blocks, ROW_TILE + 2 * CONV_HALO, 128), F32),
                        pltpu.VMEM((n_lane_blocks, ROW_TILE, 128), F32)],
        compiler_params=_cparams(("parallel", "parallel")),
        name="conformer_conv",
    )(p3, p3, p3, p3, p3, p3, p3, wpad, conv_b.reshape(1, -1), ln_w.reshape(1, -1), ln_b.reshape(1, -1))


def _out_kernel(ya_ref, yb_ref, yc_ref, yd_ref, w_ref, x_ref, gate_ref, o_ref, wb_scr, *, ctx_rows):
    b = pl.program_id(1)
    gw = GROUP_WIDTH

    @pl.when(jnp.logical_and(b == 0, pl.program_id(2) == 0))
    def _():
        wb_scr[...] = w_ref[0].astype(BF16)

    acc = _dot(ya_ref[0], wb_scr[0:gw, :])
    acc = acc + _dot(yb_ref[0], wb_scr[gw:2 * gw, :])
    acc = acc + _dot(yc_ref[0], wb_scr[2 * gw:3 * gw, :])
    acc = acc + _dot(yd_ref[0], wb_scr[3 * gw:4 * gw, :])

    row = 2 if ctx_rows else b
    o_ref[0] = x_ref[0] + gate_ref[pl.ds(row, 1), :] * acc


def _out_proj(ys, w_out, layer, x, mod, ctx_rows):
    bsz, rows, d = x.shape
    tm = ROW_TILE if ctx_rows else _pick_tile(rows, (1024, 512, 256))
    y_off = (ys[0].shape[1] - rows) // tm if ctx_rows else 0
    tn = 1024 if ctx_rows else 512
    y_spec = pl.BlockSpec((1, tm, GROUP_WIDTH), lambda j, b, i: (b, i + y_off, 0))
    return pl.pallas_call(
        functools.partial(_out_kernel, ctx_rows=ctx_rows),
        grid=(d // tn, bsz, rows // tm),
        in_specs=[y_spec, y_spec, y_spec, y_spec,
                  pl.BlockSpec((1, 4 * GROUP_WIDTH, tn), lambda j, b, i: (layer, 0, j)),
                  pl.BlockSpec((1, tm, tn), lambda j, b, i: (b, i, j)),
                  pl.BlockSpec((8, tn), lambda j, b, i: (0, 2 * d // tn + j))],
        out_specs=pl.BlockSpec((1, tm, tn), lambda j, b, i: (b, i, j)),
        out_shape=jax.ShapeDtypeStruct((bsz, rows, d), F32),
        scratch_shapes=[pltpu.VMEM((4 * GROUP_WIDTH, tn), BF16)],
        compiler_params=_cparams(("parallel", "arbitrary", "arbitrary")),
        name="out_proj",
    )(*ys, w_out, x, mod)


D_PACK_ROWS = 64


def _pack_d_kernel(w_ref, o_ref, *, n_src_blocks):
    @pl.when(pl.program_id(0) < n_src_blocks)
    def _():
        o_ref[...] = w_ref[0].astype(BF16)

    @pl.when(pl.program_id(0) >= n_src_blocks)
    def _():
        o_ref[...] = jnp.zeros(o_ref.shape, o_ref.dtype)


def _group_d_weight(w_in_t, layer):
    k = w_in_t.shape[2]
    r = D_PACK_ROWS
    base = ABC_WIDTH // r
    src_cq, src_ckv, src_kr, src_g = base, base + 768 // r, base + 1280 // r, base + 1344 // r
    n_src_blocks = (D_KR + MLA_ROPE_DIM) // r

    def src(j):
        return jnp.where(j < D_CKV // r, src_g + j,
                         jnp.where(j < D_CQ // r, src_ckv + j - D_CKV // r,
                                   jnp.where(j < D_KR // r, src_cq + j - D_CQ // r, src_kr)))

    return pl.pallas_call(
        functools.partial(_pack_d_kernel, n_src_blocks=n_src_blocks),
        grid=(D_WIDTH // r,),
        in_specs=[pl.BlockSpec((1, r, k), lambda j: (layer, src(j), 0))],
        out_specs=pl.BlockSpec((r, k), lambda j: (j, 0)),
        out_shape=jax.ShapeDtypeStruct((D_WIDTH, k), BF16),
        compiler_params=_cparams(("parallel",)),
        name="pack_group_d",
    )(w_in_t)


def kernel(x, c, ctx, c_ctx, w_mod, b_mod, norm_w, w_in, w_out, att_q_norm, att_k_norm, hgrn_lb_logits, hgrn_o_norm,
           conv_w, conv_b, conv_ln_w, conv_ln_b, mla_q_norm, mla_kv_norm, mla_w_uq, mla_w_ukv, mla_qk_q_norm,
           mla_qk_k_norm):
    bsz, seq, d = x.shape
    ctx_len = ctx.shape[1]
    depth = w_mod.shape[0]
    t = ctx_len + seq
    assert ctx_len == ROW_TILE and seq % ROW_TILE == 0 and bsz <= 2
    n_tiles = t // ROW_TILE

    lb_all = jnp.cumsum(jax.nn.softmax(hgrn_lb_logits.astype(F32), axis=0), axis=0)
    lb_all = lb_all - lb_all[0:1]

    c8 = jnp.zeros((8, d), F32).at[0:bsz].set(c).at[2].set(c_ctx)
    mod_all = _modulation(c8, w_mod, b_mod)

    gqa_tabs = _rope_tables(seq, ctx_len, HEAD_DIM)
    mla_tabs = _rope_tables(seq, ctx_len, MLA_ROPE_DIM)

    w_in_t = jnp.swapaxes(w_in, 1, 2)

    xc, xl = ctx, x
    for l in range(depth):
        ctx_out = l < depth - 1
        mod = mod_all[l]
        h = _norm_modulate(xc, xl, mod, norm_w[l], n_tiles)
        h2 = h.reshape(bsz * t, d)
        p3 = _in_proj_abc(h2, w_in_t, l).reshape(bsz, t, ABC_WIDTH)
        pd3 = _matmul_nt(h2, _group_d_weight(w_in_t, l), BF16).reshape(bsz, t, D_WIDTH)

        qa, ka, va = _gqa_prep(p3, gqa_tabs, att_q_norm[l], att_k_norm[l])
        ya = _attention(qa, ka, va, p3, A_G, N_HEADS // ATT_KV_HEADS, HEAD_DIM, True, ctx_out)

        yb = _hgrn(p3, 1.0 - lb_all[l], jnp.log(lb_all[l]), hgrn_o_norm[l], ctx_len)

        yc = _conv(p3, conv_w[l], conv_b[l], conv_ln_w[l], conv_ln_b[l], ctx_out)

        wq = jnp.pad(mla_w_uq[l].reshape(MLA_Q_RANK, N_HEADS, MLA_QK_DIM),
                     ((0, 0), (0, 0), (0, MLA_PAD_DIM - MLA_QK_DIM))).reshape(MLA_Q_RANK, -1).astype(BF16)
        wkv = mla_w_ukv[l].reshape(MLA_KV_RANK, N_HEADS, 2 * HEAD_DIM)
        wk = wkv[:, :, :HEAD_DIM].reshape(MLA_KV_RANK, -1).astype(BF16)
        wv = wkv[:, :, HEAD_DIM:].reshape(MLA_KV_RANK, -1).astype(BF16)
        qn = jnp.pad(mla_qk_q_norm[l], (0, MLA_PAD_DIM - MLA_QK_DIM))
        kn = jnp.pad(mla_qk_k_norm[l], (0, MLA_PAD_DIM - MLA_QK_DIM))
        qd, kd, vd = _mla_up(pd3, mla_tabs, mla_q_norm[l], mla_kv_norm[l], wq, wk, wv, qn, kn)
        yd = _attention(qd, kd, vd, pd3, D_G, 4, MLA_PAD_DIM, False, ctx_out)

        ys = (ya, yb, yc, yd)
        xl_new = _out_proj(ys, w_out, l, xl, mod, False)
        if ctx_out:
            xc = _out_proj(ys, w_out, l, xc, mod, True)
        xl = xl_new
    return xl
```

```python
import functools

import numpy as np
import jax
import jax.numpy as jnp
from jax import lax
from jax.experimental import pallas as pl
from jax.experimental.pallas import tpu as pltpu

F32 = jnp.float32
BF16 = jnp.bfloat16

EPS = 1e-6
ROPE_THETA = 10000.0
GRID_W = 64
HEAD_DIM = 128
GROUP_WIDTH = 1024
N_HEADS = 8
ATT_KV_HEADS = 2
MLA_Q_RANK = 768
MLA_KV_RANK = 512
MLA_ROPE_DIM = 64
MLA_QK_DIM = 192
MLA_PAD_DIM = 256
CONV_WIDTH = 31
CONV_HALO = 16
HGRN_CHUNK = 128
HGRN_LEVELS = 7
HGRN_HEADS_PER_STEP = 2
ROW_TILE = 256
NORM_SUB_ROWS = 16

IN_TILE = 512
C_U, C_GLU, C_G = 0, 1024, 2048
A_Q, A_G, A_K, A_V = 3072, 4096, 5120, 5376
B_Q, B_I, B_FF, B_FB, B_G = 5632, 6656, 7680, 8704, 9728
ABC_WIDTH = 10752
D_G, D_CKV, D_CQ, D_KR = 0, 1024, 1536, 2304
D_WIDTH = 2560

VMEM_LIMIT = 56 * 1024 * 1024


def _cparams(sem):
    return pltpu.CompilerParams(dimension_semantics=sem, vmem_limit_bytes=VMEM_LIMIT)


def _silu(x):
    return x * jax.nn.sigmoid(x)


def _dot(a, b):
    return jnp.dot(a, b, preferred_element_type=F32)


def _dot_nt(a, b):
    return lax.dot_general(a, b, (((1,), (1,)), ((), ())), preferred_element_type=F32)


def _mod_kernel(c_ref, w_ref, b_ref, o_ref):
    a = _silu(c_ref[...]).astype(BF16)
    o_ref[0] = _dot(a, w_ref[0].astype(BF16)) + b_ref[0]


def _modulation(c8, w_mod, b_mod):
    depth, d, n = w_mod.shape
    tn = 512
    return pl.pallas_call(
        _mod_kernel,
        grid=(depth, n // tn),
        in_specs=[pl.BlockSpec((8, d), lambda l, j: (0, 0)),
                  pl.BlockSpec((1, d, tn), lambda l, j: (l, 0, j)),
                  pl.BlockSpec((1, 1, tn), lambda l, j: (l, 0, j))],
        out_specs=pl.BlockSpec((1, 8, tn), lambda l, j: (l, 0, j)),
        out_shape=jax.ShapeDtypeStruct((depth, 8, n), F32),
        compiler_params=_cparams(("parallel", "parallel")),
        name="modulation",
    )(c8, w_mod, b_mod.reshape(depth, 1, n))


def _norm_kernel(xc_ref, xl_ref, mod_ref, nw_ref, o_ref, *, d, n_lat):
    b = pl.program_id(0)
    t = pl.program_id(1)

    def body(x_ref, row):
        m = mod_ref[pl.ds(row, 1), :]
        shift = m[:, :d]
        gain = nw_ref[...] * (1.0 + m[:, d:2 * d])

        def sub_tile(r, carry):
            rows = pl.ds(pl.multiple_of(r * NORM_SUB_ROWS, NORM_SUB_ROWS), NORM_SUB_ROWS)
            x = x_ref[0, rows, :]
            ms = jnp.mean(x * x, axis=-1, keepdims=True)
            o_ref[0, rows, :] = (x * lax.rsqrt(ms + EPS) * gain + shift).astype(o_ref.dtype)
            return carry

        lax.fori_loop(0, ROW_TILE // NORM_SUB_ROWS, sub_tile, 0, unroll=4)

    @pl.when(t == n_lat)
    def _():
        body(xc_ref, 2)

    @pl.when(t < n_lat)
    def _():
        body(xl_ref, b)


def _norm_modulate(xc, xl, mod, nw, n_tiles):
    bsz, _, d = xc.shape
    n_lat = n_tiles - 1
    return pl.pallas_call(
        functools.partial(_norm_kernel, d=d, n_lat=n_lat),
        grid=(bsz, n_tiles),
        in_specs=[pl.BlockSpec((1, ROW_TILE, d), lambda b, t: (b, 0, 0)),
                  pl.BlockSpec((1, ROW_TILE, d), lambda b, t: (b, jnp.minimum(t, n_lat - 1), 0)),
                  pl.BlockSpec((8, 3 * d), lambda b, t: (0, 0)),
                  pl.BlockSpec((1, d), lambda b, t: (0, 0))],
        out_specs=pl.BlockSpec((1, ROW_TILE, d), lambda b, t: (b, t, 0)),
        out_shape=jax.ShapeDtypeStruct((bsz, n_tiles * ROW_TILE, d), BF16),
        compiler_params=_cparams(("parallel", "parallel")),
        name="norm_modulate",
    )(xc, xl, mod, nw.reshape(1, d))


def _mm_kernel(a_ref, w_ref, o_ref):
    o_ref[...] = _dot_nt(a_ref[...], w_ref[...]).astype(o_ref.dtype)


def _pick_tile(n, cands):
    for c in cands:
        if n % c == 0:
            return c
    raise ValueError(f"no tile for {n}")


def _matmul_nt(a, wt, out_dtype):
    m, k = a.shape
    n, _ = wt.shape
    tm = _pick_tile(m, (1088, 1024, 768, 512, 256))
    tn = _pick_tile(n, (512, 256, 128))
    return pl.pallas_call(
        _mm_kernel,
        grid=(m // tm, n // tn),
        in_specs=[pl.BlockSpec((tm, k), lambda i, j: (i, 0)),
                  pl.BlockSpec((tn, k), lambda i, j: (j, 0))],
        out_specs=pl.BlockSpec((tm, tn), lambda i, j: (i, j)),
        out_shape=jax.ShapeDtypeStruct((m, n), out_dtype),
        compiler_params=_cparams(("parallel", "parallel")),
        name="in_proj_d",
    )(a, wt)


def _mm_wcast_kernel(a_ref, w_ref, o_ref, wb_scr):
    @pl.when(pl.program_id(1) == 0)
    def _():
        wb_scr[...] = w_ref[0].astype(BF16)

    o_ref[...] = _dot_nt(a_ref[...], wb_scr[...]).astype(o_ref.dtype)


def _in_proj_abc(a, w_in_t, layer):
    m, k = a.shape
    tm = _pick_tile(m, (1088, 1024, 768, 512, 256))
    n_tiles = ABC_WIDTH // IN_TILE

    def src_tile(j):
        return jnp.where(j < 6, j + 15, jnp.where(j < 8, j - 6, jnp.where(j < 10, j - 5, jnp.where(j < 11, 2, j - 6))))

    return pl.pallas_call(
        _mm_wcast_kernel,
        grid=(n_tiles, m // tm),
        in_specs=[pl.BlockSpec((tm, k), lambda j, i: (i, 0)),
                  pl.BlockSpec((1, IN_TILE, k), lambda j, i: (layer, src_tile(j), 0))],
        out_specs=pl.BlockSpec((tm, IN_TILE), lambda j, i: (i, j)),
        out_shape=jax.ShapeDtypeStruct((m, ABC_WIDTH), BF16),
        scratch_shapes=[pltpu.VMEM((IN_TILE, k), BF16)],
        compiler_params=_cparams(("parallel", "arbitrary")),
        name="in_proj_abc",
    )(a, w_in_t)


def _rope_tables(seq, ctx_len, rope_dim):
    quarter = rope_dim // 4
    inv = (np.float32(ROPE_THETA) ** (-np.arange(quarter, dtype=np.float32) / np.float32(quarter))).astype(np.float32)
    pos = np.arange(seq, dtype=np.int32)
    ar = (pos // GRID_W).astype(np.float32)[:, None] * inv
    ac = (pos % GRID_W).astype(np.float32)[:, None] * inv
    cos = np.concatenate([np.cos(ar), np.cos(ar), np.cos(ac), np.cos(ac)], axis=-1)
    sin = np.concatenate([np.sin(ar), np.sin(ar), np.sin(ac), np.sin(ac)], axis=-1)

    def full(tab, fill):
        tab = np.pad(tab, ((0, 0), (0, 128 - rope_dim)), constant_values=fill)
        return jnp.asarray(np.concatenate([tab, np.full((ctx_len, 128), fill, np.float32)], axis=0), F32)

    swap = np.zeros((128, 128), np.float32)
    for i in range(rope_dim):
        if (i // quarter) % 2 == 0:
            swap[i + quarter, i] = -1.0
        else:
            swap[i - quarter, i] = 1.0
    return full(cos, 1.0), full(sin, 0.0), jnp.asarray(swap, BF16)


def _rope(x, cos, sin, swap):
    return x * cos + _dot(x.astype(BF16), swap) * sin


def _gqa_prep_kernel(q_ref, k_ref, v_ref, cos_ref, sin_ref, swap_ref, qn_ref, kn_ref, qo_ref, ko_ref, vo_ref):
    cos, sin, swap = cos_ref[...], sin_ref[...], swap_ref[...]

    def head(x, w, scale):
        ms = jnp.mean(x * x, axis=-1, keepdims=True)
        y = x * lax.rsqrt(ms + EPS) * w
        return (_rope(y, cos, sin, swap) * scale).astype(BF16)

    for h in range(N_HEADS):
        sl = slice(h * HEAD_DIM, (h + 1) * HEAD_DIM)
        qo_ref[0, :, sl] = head(q_ref[0, :, sl].astype(F32), qn_ref[...], HEAD_DIM ** -0.5)
    for h in range(ATT_KV_HEADS):
        sl = slice(h * HEAD_DIM, (h + 1) * HEAD_DIM)
        ko_ref[0, :, sl] = head(k_ref[0, :, sl].astype(F32), kn_ref[...], 1.0)
        _store_value_ext(vo_ref, h, v_ref[0, :, sl])


def _store_value_ext(vo_ref, h, v):
    vo_ref[0, :, 2 * h * HEAD_DIM:(2 * h + 1) * HEAD_DIM] = v
    vo_ref[0, :, (2 * h + 1) * HEAD_DIM:(2 * h + 2) * HEAD_DIM] = jnp.ones(v.shape, v.dtype)


def _gqa_prep(p3, tabs, qn, kn):
    bsz, t, _ = p3.shape
    nt = t // ROW_TILE
    kvw = ATT_KV_HEADS * HEAD_DIM
    tab_spec = pl.BlockSpec((ROW_TILE, 128), lambda b, i: (i, 0))
    swap_spec = pl.BlockSpec((128, 128), lambda b, i: (0, 0))
    vec_spec = pl.BlockSpec((1, 128), lambda b, i: (0, 0))
    return pl.pallas_call(
        _gqa_prep_kernel,
        grid=(bsz, nt),
        in_specs=[pl.BlockSpec((1, ROW_TILE, GROUP_WIDTH), lambda b, i: (b, i, A_Q // GROUP_WIDTH)),
                  pl.BlockSpec((1, ROW_TILE, kvw), lambda b, i: (b, i, A_K // kvw)),
                  pl.BlockSpec((1, ROW_TILE, kvw), lambda b, i: (b, i, A_V // kvw)),
                  tab_spec, tab_spec, swap_spec, vec_spec, vec_spec],
        out_specs=[pl.BlockSpec((1, ROW_TILE, GROUP_WIDTH), lambda b, i: (b, i, 0)),
                   pl.BlockSpec((1, ROW_TILE, kvw), lambda b, i: (b, i, 0)),
                   pl.BlockSpec((1, ROW_TILE, 2 * kvw), lambda b, i: (b, i, 0))],
        out_shape=[jax.ShapeDtypeStruct((bsz, t, GROUP_WIDTH), BF16),
                   jax.ShapeDtypeStruct((bsz, t, kvw), BF16),
                   jax.ShapeDtypeStruct((bsz, t, 2 * kvw), BF16)],
        compiler_params=_cparams(("parallel", "parallel")),
        name="gqa_prep",
    )(p3, p3, p3, *tabs, qn.reshape(1, 128), kn.reshape(1, 128))


def _attn_kernel(q_ref, k_ref, v_ref, g_ref, o_ref, *, n_heads, dq, heads_per_kv, n_lat):
    i = pl.program_id(2)
    vw = 2 * HEAD_DIM

    def attend(lo):
        def kv(j):
            g = j // heads_per_kv
            return k_ref[0, lo:, g * dq:(g + 1) * dq], v_ref[0, lo:, g * vw:(g + 1) * vw]

        s = [None] * n_heads
        p = [None] * n_heads
        for j in range(n_heads + 2):
            if j < n_heads:
                s[j] = _dot_nt(q_ref[0, :, j * dq:(j + 1) * dq], kv(j)[0])
            if 1 <= j <= n_heads:
                m = jnp.max(s[j - 1], axis=-1, keepdims=True)
                p[j - 1] = jnp.exp((s[j - 1] - m).astype(BF16))
                s[j - 1] = None
            if j >= 2:
                oe = _dot(p[j - 2], kv(j - 2)[1])
                p[j - 2] = None
                o = oe[:, :HEAD_DIM] / oe[:, HEAD_DIM:]
                sl = slice((j - 2) * HEAD_DIM, (j - 1) * HEAD_DIM)
                o_ref[0, :, sl] = (o * _silu(g_ref[0, :, sl].astype(F32))).astype(o_ref.dtype)

    @pl.when(i < n_lat)
    def _():
        attend(0)

    @pl.when(i == n_lat)
    def _():
        attend(n_lat * ROW_TILE)


def _attention(q, k, v_ext, p3, gate_off, n_heads, dq, heads_per_kv, ctx_out):
    bsz, t, _ = q.shape
    n_lat = t // ROW_TILE - 1
    nt = n_lat + (1 if ctx_out else 0)
    n_kv = n_heads // heads_per_kv
    gw = n_heads * HEAD_DIM
    n_steps = q.shape[-1] // (n_heads * dq)
    return pl.pallas_call(
        functools.partial(_attn_kernel, n_heads=n_heads, dq=dq, heads_per_kv=heads_per_kv, n_lat=n_lat),
        grid=(bsz, n_steps, nt),
        in_specs=[pl.BlockSpec((1, ROW_TILE, n_heads * dq), lambda b, h, i: (b, i, h)),
                  pl.BlockSpec((1, t, n_kv * dq), lambda b, h, i: (b, 0, h)),
                  pl.BlockSpec((1, t, n_kv * 2 * HEAD_DIM), lambda b, h, i: (b, 0, h)),
                  pl.BlockSpec((1, ROW_TILE, gw), lambda b, h, i: (b, i, gate_off // gw + h))],
        out_specs=pl.BlockSpec((1, ROW_TILE, gw), lambda b, h, i: (b, i, h)),
        out_shape=jax.ShapeDtypeStruct((bsz, nt * ROW_TILE, n_steps * gw), BF16),
        compiler_params=_cparams(("parallel", "parallel", "parallel")),
        name="attention",
    )(q, k, v_ext, p3)


def _mla_up_kernel(cq_ref, ckv_ref, kr_ref, cos_ref, sin_ref, swap_ref, cqn_ref, ckvn_ref, wq_ref, wk_ref, wv_ref,
                   qn_ref, kn_ref, qo_ref, ko_ref, vo_ref):
    cos, sin, swap = cos_ref[...], sin_ref[...], swap_ref[...]

    def rms(x, w):
        return x * lax.rsqrt(jnp.mean(x * x, axis=-1, keepdims=True) + EPS) * w

    cq = rms(cq_ref[0].astype(F32), cqn_ref[...]).astype(BF16)
    ckv = rms(ckv_ref[0].astype(F32), ckvn_ref[...]).astype(BF16)
    qf = _dot(cq, wq_ref[...])
    kf = _dot(ckv, wk_ref[...])
    vf = _dot(ckv, wv_ref[...]).astype(BF16)
    kr = kr_ref[0].astype(F32)
    kr_ss = jnp.sum(kr * kr, axis=-1, keepdims=True)
    qw1, qw2 = qn_ref[:, :128], qn_ref[:, 128:]
    kw1, kw2 = kn_ref[:, :128], kn_ref[:, 128:]
    kr_rot = _rope(kr * kw2, cos, sin, swap)
    scale = MLA_QK_DIM ** -0.5
    for h in range(N_HEADS):
        lo = h * MLA_PAD_DIM
        q1 = qf[:, lo:lo + 128]
        q2 = qf[:, lo + 128:lo + 256]
        ss = jnp.sum(q1 * q1 + q2 * q2, axis=-1, keepdims=True)
        r = lax.rsqrt(ss / MLA_QK_DIM + EPS) * scale
        qo_ref[0, :, lo:lo + 128] = (q1 * r * qw1).astype(BF16)
        qo_ref[0, :, lo + 128:lo + 256] = _rope(q2 * r * qw2, cos, sin, swap).astype(BF16)
        k1 = kf[:, h * 128:(h + 1) * 128]
        r = lax.rsqrt((jnp.sum(k1 * k1, axis=-1, keepdims=True) + kr_ss) / MLA_QK_DIM + EPS)
        ko_ref[0, :, lo:lo + 128] = (k1 * r * kw1).astype(BF16)
        ko_ref[0, :, lo + 128:lo + 256] = (kr_rot * r).astype(BF16)
        _store_value_ext(vo_ref, h, vf[:, h * 128:(h + 1) * 128])


def _mla_up(p3, tabs, cqn, ckvn, wq, wk, wv, qn, kn):
    bsz, t, _ = p3.shape
    nt = t // ROW_TILE
    hw = N_HEADS * MLA_PAD_DIM
    tab_spec = pl.BlockSpec((ROW_TILE, 128), lambda b, i: (i, 0))

    def whole(a):
        return pl.BlockSpec(a.shape, lambda b, i: (0,) * a.ndim)

    args = (cqn.reshape(1, -1), ckvn.reshape(1, -1), wq, wk, wv, qn.reshape(1, -1), kn.reshape(1, -1))
    return pl.pallas_call(
        _mla_up_kernel,
        grid=(bsz, nt),
        in_specs=[pl.BlockSpec((1, ROW_TILE, MLA_Q_RANK), lambda b, i: (b, i, D_CQ // MLA_Q_RANK)),
                  pl.BlockSpec((1, ROW_TILE, MLA_KV_RANK), lambda b, i: (b, i, D_CKV // MLA_KV_RANK)),
                  pl.BlockSpec((1, ROW_TILE, 128), lambda b, i: (b, i, D_KR // 128)),
                  tab_spec, tab_spec, pl.BlockSpec((128, 128), lambda b, i: (0, 0))] + [whole(a) for a in args],
        out_specs=[pl.BlockSpec((1, ROW_TILE, hw), lambda b, i: (b, i, 0)),
                   pl.BlockSpec((1, ROW_TILE, hw), lambda b, i: (b, i, 0)),
                   pl.BlockSpec((1, ROW_TILE, 2 * GROUP_WIDTH), lambda b, i: (b, i, 0))],
        out_shape=[jax.ShapeDtypeStruct((bsz, t, hw), BF16),
                   jax.ShapeDtypeStruct((bsz, t, hw), BF16),
                   jax.ShapeDtypeStruct((bsz, t, 2 * GROUP_WIDTH), BF16)],
        compiler_params=_cparams(("parallel", "parallel")),
        name="mla_up",
    )(p3, p3, p3, *tabs, *args)


def _hgrn_consts():
    c = HGRN_CHUNK
    t = np.arange(c)[:, None]
    s = np.arange(c)[None, :]
    cum = np.stack([s <= t, s >= t]).astype(np.float32)
    masks = np.zeros((2, HGRN_LEVELS + 1, c, c), np.float32)
    masks[:, 0] = np.eye(c)
    for lvl in range(1, HGRN_LEVELS + 1):
        m = c >> lvl
        same = (t // (2 * m)) == (s // (2 * m))
        t_up, s_up = (t // m) % 2 == 1, (s // m) % 2 == 1
        masks[0, lvl] = same & t_up & ~s_up
        masks[1, lvl] = same & ~t_up & s_up
    return jnp.asarray(cum, BF16), jnp.asarray(masks, BF16)


def _hgrn_kernel(q_ref, i_ref, ff_ref, fb_ref, gate_ref, oml_ref, llb_ref, on_ref, cum_ref, mask_ref, y_ref,
                 *scr, n_ctx_chunks, n_chunks):
    c = HGRN_CHUNK
    n_chain = 2 * HGRN_HEADS_PER_STEP
    g_scrs, o_scrs = scr[:n_chain], scr[n_chain:]
    sub = lax.broadcasted_iota(jnp.int32, (8, 128), 0)

    def level_ref(g_scr, d, lvl):
        m = c >> lvl
        blk = 2 * m
        def row(r):
            return g_scr[r:r + 1, :]
        idx = [b * blk + m - 1 + d for b in range(c // blk)]
        if blk >= 8:
            return jnp.concatenate([jnp.broadcast_to(row(r), (blk, 128)) for r in idx], axis=0)
        per = 8 // blk
        tiles = []
        for j in range(c // 8):
            tile = jnp.broadcast_to(row(idx[j * per + per - 1]), (8, 128))
            for u in range(per - 2, -1, -1):
                tile = jnp.where(sub < (u + 1) * blk, jnp.broadcast_to(row(idx[j * per + u]), (8, 128)), tile)
            tiles.append(tile)
        return jnp.concatenate(tiles, axis=0)

    chains = [(hh, d) for hh in range(HGRN_HEADS_PER_STEP) for d in range(2)]

    def chunk_rows(n):
        n_lat_chunks = n_chunks - n_ctx_chunks
        cf = jnp.where(n < n_ctx_chunks, n_lat_chunks + n, n - n_ctx_chunks)
        cb = n_chunks - 1 - n
        return [pl.ds(pl.multiple_of(cf * c, c), c), pl.ds(pl.multiple_of(cb * c, c), c)]

    def gate_decay_parts(n):
        rows_d = chunk_rows(n)
        out = []
        for hh, d in chains:
            cols = slice(hh * HEAD_DIM, (hh + 1) * HEAD_DIM)
            f = (ff_ref if d == 0 else fb_ref)[0, rows_d[d], cols].astype(F32)
            a = llb_ref[d:d + 1, cols] - f
            lf = (jnp.minimum(f, 0.0) - jnp.log(1.0 + jnp.exp(-jnp.abs(f)))
                  + jnp.maximum(a, 0.0) + jnp.log(1.0 + jnp.exp(-jnp.abs(a))))
            hi = lf.astype(BF16)
            r1 = lf - hi.astype(F32)
            mid = r1.astype(BF16)
            lo = (r1 - mid.astype(F32)).astype(BF16)
            cum = cum_ref[d]
            out.append((_dot(cum, hi), _dot(cum, mid), _dot(cum, lo)))
        return out

    def gate_qk(n):
        rows_d = chunk_rows(n)
        out = []
        for hh, d in chains:
            cols = slice(hh * HEAD_DIM, (hh + 1) * HEAD_DIM)
            qv = q_ref[0, rows_d[d], cols].astype(F32)
            qb = (_silu(qv) * (HEAD_DIM ** -0.5)).astype(BF16)
            f = (ff_ref if d == 0 else fb_ref)[0, rows_d[d], cols].astype(F32)
            kb = (oml_ref[d:d + 1, cols] / (1.0 + jnp.exp(f))).astype(BF16)
            out.append((qb, kb))
        return out

    def gate_join(qk, parts):
        return [(qb, kb, p0 + p1 + p2) for (qb, kb), (p0, p1, p2) in zip(qk, parts)]

    def body(n, carry):
        states, cur = carry
        n_next = jnp.minimum(n + 1, n_chunks - 1)
        rows_d = chunk_rows(n)
        vb, gtot, o, scores, o_intra, st_upd = {}, {}, {}, {}, {}, {}
        new = [None] * n_chain
        for ci, (hh, d) in enumerate(chains):
            g_scrs[ci][...] = cur[ci][2]

        def scores_of(group):
            for ci in group:
                hh, d = chains[ci]
                qb, kb, g = cur[ci]
                vb[ci] = i_ref[0, rows_d[d], hh * HEAD_DIM:(hh + 1) * HEAD_DIM]
                end = (c - 1) * (1 - d)
                gtot[ci] = g_scrs[ci][end:end + 1, :]
                o[ci] = _dot_nt(qb * jnp.exp(g).astype(BF16), states[ci].astype(BF16))
                scores[ci] = _dot_nt(qb, kb).astype(BF16) * mask_ref[d, 0]
            for lvl in range(1, HGRN_LEVELS + 1):
                for ci in group:
                    d = chains[ci][1]
                    qb, kb, g = cur[ci]
                    e = jnp.exp(-jnp.abs(g - level_ref(g_scrs[ci], d, lvl))).astype(BF16)
                    s_l = _dot_nt(qb * e, kb * e)
                    scores[ci] = scores[ci] + s_l.astype(BF16) * mask_ref[d, lvl]

        def issue_last(group):
            for ci in group:
                qb, kb, g = cur[ci]
                o_intra[ci] = _dot(scores[ci], vb[ci])
                kend = kb * jnp.exp(gtot[ci] - g).astype(BF16)
                vt = vb[ci].astype(F32).T.astype(BF16)
                st_upd[ci] = _dot(vt, kend)

        def finish(group):
            for ci in group:
                o_scrs[ci][rows_d[chains[ci][1]], :] = o[ci] + o_intra[ci]
                new[ci] = states[ci] * jnp.exp(gtot[ci]) + st_upd[ci]

        every = range(n_chain)
        scores_of(every)
        nxt_parts = gate_decay_parts(n_next)
        issue_last(every)
        nxt_qk = gate_qk(n_next)
        finish(every)
        return tuple(new), gate_join(nxt_qk, nxt_parts)

    zero = jnp.zeros((HEAD_DIM, HEAD_DIM), F32)
    lax.fori_loop(0, n_chunks, body, ((zero,) * n_chain, gate_join(gate_qk(0), gate_decay_parts(0))), unroll=4)

    n_row_tiles = (n_chunks * c) // ROW_TILE

    def readout(i, carry):
        rows = pl.ds(pl.multiple_of(i * ROW_TILE, ROW_TILE), ROW_TILE)
        for hh in range(HGRN_HEADS_PER_STEP):
            cols = slice(hh * HEAD_DIM, (hh + 1) * HEAD_DIM)
            o = o_scrs[2 * hh][rows, :] + o_scrs[2 * hh + 1][rows, :]
            o = o * lax.rsqrt(jnp.mean(o * o, axis=-1, keepdims=True) + EPS) * on_ref[:, cols]
            y_ref[0, rows, cols] = (o * _silu(gate_ref[0, rows, cols].astype(F32))).astype(y_ref.dtype)
        return carry

    lax.fori_loop(0, n_row_tiles, readout, 0)


def _hgrn(p3, one_minus_lb, log_lb, o_norm, ctx_len):
    bsz, t, _ = p3.shape
    cum, masks = _hgrn_consts()
    bw = HGRN_HEADS_PER_STEP * HEAD_DIM
    n_chain = 2 * HGRN_HEADS_PER_STEP

    def col(off):
        return pl.BlockSpec((1, t, bw), lambda b, h: (b, 0, off // bw + h))

    return pl.pallas_call(
        functools.partial(_hgrn_kernel, n_ctx_chunks=ctx_len // HGRN_CHUNK, n_chunks=t // HGRN_CHUNK),
        grid=(bsz, GROUP_WIDTH // bw),
        in_specs=[col(B_Q), col(B_I), col(B_FF), col(B_FB), col(B_G),
                  pl.BlockSpec((2, bw), lambda b, h: (0, h)),
                  pl.BlockSpec((2, bw), lambda b, h: (0, h)),
                  pl.BlockSpec((1, bw), lambda b, h: (0, h)),
                  pl.BlockSpec(cum.shape, lambda b, h: (0, 0, 0)),
                  pl.BlockSpec(masks.shape, lambda b, h: (0, 0, 0, 0))],
        out_specs=pl.BlockSpec((1, t, bw), lambda b, h: (b, 0, h)),
        out_shape=jax.ShapeDtypeStruct((bsz, t, GROUP_WIDTH), BF16),
        scratch_shapes=([pltpu.VMEM((HGRN_CHUNK, HEAD_DIM), F32)] * n_chain
                        + [pltpu.VMEM((t, HEAD_DIM), F32)] * n_chain),
        compiler_params=_cparams(("parallel", "parallel")),
        name="hgrn2",
    )(p3, p3, p3, p3, p3, one_minus_lb, log_lb, o_norm.reshape(1, -1), cum, masks)


def _conv_kernel(u_ref, up_ref, un_ref, gl_ref, glp_ref, gln_ref, gate_ref, w_ref, cb_ref, lw_ref, lb_ref, y_ref,
                 xs_scr, acc_scr, *, n_lat):
    t = pl.program_id(1)
    h = CONV_HALO
    prev_ok = jnp.logical_and(t >= 1, t < n_lat).astype(F32)
    next_ok = (t < n_lat - 1).astype(F32)
    n_lane_blocks = GROUP_WIDTH // 128
    x_prev = up_ref[0].astype(F32) * jax.nn.sigmoid(glp_ref[0].astype(F32)) * prev_ok
    x_main = u_ref[0].astype(F32) * jax.nn.sigmoid(gl_ref[0].astype(F32))
    x_next = un_ref[0].astype(F32) * jax.nn.sigmoid(gln_ref[0].astype(F32)) * next_ok
    for cb in range(n_lane_blocks):
        sl = slice(cb * 128, (cb + 1) * 128)
        xs_scr[cb, 0:h, :] = x_prev[:, sl]
        xs_scr[cb, h:h + ROW_TILE, :] = x_main[:, sl]
        xs_scr[cb, h + ROW_TILE:, :] = x_next[:, sl]
    rc = 64
    base = h - CONV_WIDTH // 2

    def lane_block(cb, carry):
        for r in range(ROW_TILE // rc):
            acc = jnp.zeros((rc, 128), F32)
            for tap in range(CONV_WIDTH):
                lo = base + r * rc + tap
                acc = acc + xs_scr[cb, lo:lo + rc, :] * w_ref[cb, tap:tap + 1, :]
            acc_scr[cb, r * rc:(r + 1) * rc, :] = acc
        return carry

    lax.fori_loop(0, n_lane_blocks, lane_block, 0)
    y = jnp.concatenate([acc_scr[cb] for cb in range(n_lane_blocks)], axis=1) + cb_ref[...]
    mu = jnp.mean(y, axis=-1, keepdims=True)
    yc = y - mu
    var = jnp.mean(yc * yc, axis=-1, keepdims=True)
    z = yc * lax.rsqrt(var + EPS) * lw_ref[...] + lb_ref[...]
    y_ref[0] = (_silu(z) * _silu(gate_ref[0].astype(F32))).astype(y_ref.dtype)


def _conv(p3, conv_w, conv_b, ln_w, ln_b, ctx_out):
    bsz, t, _ = p3.shape
    n_lat = t // ROW_TILE - 1
    nt = n_lat + (1 if ctx_out else 0)
    hpt = ROW_TILE // CONV_HALO
    n_halo = t // CONV_HALO
    n_lane_blocks = GROUP_WIDTH // 128
    wpad = jnp.pad(conv_w, ((0, 32 - CONV_WIDTH), (0, 0))).reshape(32, n_lane_blocks, 128).transpose(1, 0, 2)

    def main(off):
        return pl.BlockSpec((1, ROW_TILE, GROUP_WIDTH), lambda b, i: (b, i, off // GROUP_WIDTH))

    def prev(off):
        return pl.BlockSpec((1, CONV_HALO, GROUP_WIDTH),
                            lambda b, i: (b, jnp.maximum(i * hpt - 1, 0), off // GROUP_WIDTH))

    def nxt(off):
        return pl.BlockSpec((1, CONV_HALO, GROUP_WIDTH),
                            lambda b, i: (b, jnp.minimum((i + 1) * hpt, n_halo - 1), off // GROUP_WIDTH))

    vec = pl.BlockSpec((1, GROUP_WIDTH), lambda b, i: (0, 0))
    return pl.pallas_call(
        functools.partial(_conv_kernel, n_lat=n_lat),
        grid=(bsz, nt),
        in_specs=[main(C_U), prev(C_U), nxt(C_U), main(C_GLU), prev(C_GLU), nxt(C_GLU), main(C_G),
                  pl.BlockSpec((n_lane_blocks, 32, 128), lambda b, i: (0, 0, 0)), vec, vec, vec],
        out_specs=pl.BlockSpec((1, ROW_TILE, GROUP_WIDTH), lambda b, i: (b, i, 0)),
        out_shape=jax.ShapeDtypeStruct((bsz, nt * ROW_TILE, GROUP_WIDTH), BF16),
        scratch_shapes=[pltpu.VMEM((n_lane_blocks, ROW_TILE + 2 * CONV_HALO, 128), F32),
                        pltpu.VMEM((n_lane_blocks, ROW_TILE, 128), F32)],
        compiler_params=_cparams(("parallel", "parallel")),
        name="conformer_conv",
    )(p3, p3, p3, p3, p3, p3, p3, wpad, conv_b.reshape(1, -1), ln_w.reshape(1, -1), ln_b.reshape(1, -1))


def _out_kernel(ya_ref, yb_ref, yc_ref, yd_ref, w_ref, x_ref, gate_ref, o_ref, wb_scr, *, ctx_rows):
    b = pl.program_id(1)
    gw = GROUP_WIDTH

    @pl.when(jnp.logical_and(b == 0, pl.program_id(2) == 0))
    def _():
        wb_scr[...] = w_ref[0].astype(BF16)

    acc = _dot(ya_ref[0], wb_scr[0:gw, :])
    acc = acc + _dot(yb_ref[0], wb_scr[gw:2 * gw, :])
    acc = acc + _dot(yc_ref[0], wb_scr[2 * gw:3 * gw, :])
    acc = acc + _dot(yd_ref[0], wb_scr[3 * gw:4 * gw, :])

    row = 2 if ctx_rows else b
    o_ref[0] = x_ref[0] + gate_ref[pl.ds(row, 1), :] * acc


def _out_proj(ys, w_out, layer, x, mod, ctx_rows):
    bsz, rows, d = x.shape
    tm = ROW_TILE if ctx_rows else _pick_tile(rows, (1024, 512, 256))
    y_off = (ys[0].shape[1] - rows) // tm if ctx_rows else 0
    tn = 1024 if ctx_rows else 512
    y_spec = pl.BlockSpec((1, tm, GROUP_WIDTH), lambda j, b, i: (b, i + y_off, 0))
    return pl.pallas_call(
        functools.partial(_out_kernel, ctx_rows=ctx_rows),
        grid=(d // tn, bsz, rows // tm),
        in_specs=[y_spec, y_spec, y_spec, y_spec,
                  pl.BlockSpec((1, 4 * GROUP_WIDTH, tn), lambda j, b, i: (layer, 0, j)),
                  pl.BlockSpec((1, tm, tn), lambda j, b, i: (b, i, j)),
                  pl.BlockSpec((8, tn), lambda j, b, i: (0, 2 * d // tn + j))],
        out_specs=pl.BlockSpec((1, tm, tn), lambda j, b, i: (b, i, j)),
        out_shape=jax.ShapeDtypeStruct((bsz, rows, d), F32),
        scratch_shapes=[pltpu.VMEM((4 * GROUP_WIDTH, tn), BF16)],
        compiler_params=_cparams(("parallel", "arbitrary", "arbitrary")),
        name="out_proj",
    )(*ys, w_out, x, mod)


D_PACK_ROWS = 64


def _pack_d_kernel(w_ref, o_ref, *, n_src_blocks):
    @pl.when(pl.program_id(0) < n_src_blocks)
    def _():
        o_ref[...] = w_ref[0].astype(BF16)

    @pl.when(pl.program_id(0) >= n_src_blocks)
    def _():
        o_ref[...] = jnp.zeros(o_ref.shape, o_ref.dtype)


def _group_d_weight(w_in_t, layer):
    k = w_in_t.shape[2]
    r = D_PACK_ROWS
    base = ABC_WIDTH // r
    src_cq, src_ckv, src_kr, src_g = base, base + 768 // r, base + 1280 // r, base + 1344 // r
    n_src_blocks = (D_KR + MLA_ROPE_DIM) // r

    def src(j):
        return jnp.where(j < D_CKV // r, src_g + j,
                         jnp.where(j < D_CQ // r, src_ckv + j - D_CKV // r,
                                   jnp.where(j < D_KR // r, src_cq + j - D_CQ // r, src_kr)))

    return pl.pallas_call(
        functools.partial(_pack_d_kernel, n_src_blocks=n_src_blocks),
        grid=(D_WIDTH // r,),
        in_specs=[pl.BlockSpec((1, r, k), lambda j: (layer, src(j), 0))],
        out_specs=pl.BlockSpec((r, k), lambda j: (j, 0)),
        out_shape=jax.ShapeDtypeStruct((D_WIDTH, k), BF16),
        compiler_params=_cparams(("parallel",)),
        name="pack_group_d",
    )(w_in_t)


def kernel(x, c, ctx, c_ctx, w_mod, b_mod, norm_w, w_in, w_out, att_q_norm, att_k_norm, hgrn_lb_logits, hgrn_o_norm,
           conv_w, conv_b, conv_ln_w, conv_ln_b, mla_q_norm, mla_kv_norm, mla_w_uq, mla_w_ukv, mla_qk_q_norm,
           mla_qk_k_norm):
    bsz, seq, d = x.shape
    ctx_len = ctx.shape[1]
    depth = w_mod.shape[0]
    t = ctx_len + seq
    assert ctx_len == ROW_TILE and seq % ROW_TILE == 0 and bsz <= 2
    n_tiles = t // ROW_TILE

    lb_all = jnp.cumsum(jax.nn.softmax(hgrn_lb_logits.astype(F32), axis=0), axis=0)
    lb_all = lb_all - lb_all[0:1]

    c8 = jnp.zeros((8, d), F32).at[0:bsz].set(c).at[2].set(c_ctx)
    mod_all = _modulation(c8, w_mod, b_mod)

    gqa_tabs = _rope_tables(seq, ctx_len, HEAD_DIM)
    mla_tabs = _rope_tables(seq, ctx_len, MLA_ROPE_DIM)

    w_in_t = jnp.swapaxes(w_in, 1, 2)

    xc, xl = ctx, x
    for l in range(depth):
        ctx_out = l < depth - 1
        mod = mod_all[l]
        h = _norm_modulate(xc, xl, mod, norm_w[l], n_tiles)
        h2 = h.reshape(bsz * t, d)
        p3 = _in_proj_abc(h2, w_in_t, l).reshape(bsz, t, ABC_WIDTH)
        pd3 = _matmul_nt(h2, _group_d_weight(w_in_t, l), BF16).reshape(bsz, t, D_WIDTH)

        qa, ka, va = _gqa_prep(p3, gqa_tabs, att_q_norm[l], att_k_norm[l])
        ya = _attention(qa, ka, va, p3, A_G, N_HEADS, HEAD_DIM, N_HEADS // ATT_KV_HEADS, ctx_out)

        yb = _hgrn(p3, 1.0 - lb_all[l], jnp.log(lb_all[l]), hgrn_o_norm[l], ctx_len)

        yc = _conv(p3, conv_w[l], conv_b[l], conv_ln_w[l], conv_ln_b[l], ctx_out)

        wq = jnp.pad(mla_w_uq[l].reshape(MLA_Q_RANK, N_HEADS, MLA_QK_DIM),
                     ((0, 0), (0, 0), (0, MLA_PAD_DIM - MLA_QK_DIM))).reshape(MLA_Q_RANK, -1).astype(BF16)
        wkv = mla_w_ukv[l].reshape(MLA_KV_RANK, N_HEADS, 2 * HEAD_DIM)
        wk = wkv[:, :, :HEAD_DIM].reshape(MLA_KV_RANK, -1).astype(BF16)
        wv = wkv[:, :, HEAD_DIM:].reshape(MLA_KV_RANK, -1).astype(BF16)
        qn = jnp.pad(mla_qk_q_norm[l], (0, MLA_PAD_DIM - MLA_QK_DIM))
        kn = jnp.pad(mla_qk_k_norm[l], (0, MLA_PAD_DIM - MLA_QK_DIM))
        qd, kd, vd = _mla_up(pd3, mla_tabs, mla_q_norm[l], mla_kv_norm[l], wq, wk, wv, qn, kn)
        yd = _attention(qd, kd, vd, pd3, D_G, 4, MLA_PAD_DIM, 1, ctx_out)

        ys = (ya, yb, yc, yd)
        xl_new = _out_proj(ys, w_out, l, xl, mod, False)
        if ctx_out:
            xc = _out_proj(ys, w_out, l, xc, mod, True)
        xl = xl_new
    return xl
```

```python
import functools

import numpy as np
import jax
import jax.numpy as jnp
from jax import lax
from jax.experimental import pallas as pl
from jax.experimental.pallas import tpu as pltpu

F32 = jnp.float32
BF16 = jnp.bfloat16

EPS = 1e-6
ROPE_THETA = 10000.0
GRID_W = 64
HEAD_DIM = 128
GROUP_WIDTH = 1024
N_HEADS = 8
ATT_KV_HEADS = 2
MLA_Q_RANK = 768
MLA_KV_RANK = 512
MLA_ROPE_DIM = 64
MLA_QK_DIM = 192
MLA_PAD_DIM = 256
CONV_WIDTH = 31
CONV_HALO = 16
HGRN_CHUNK = 128
HGRN_LEVELS = 7
HGRN_HEADS_PER_STEP = 2
ROW_TILE = 256
NORM_SUB_ROWS = 16

IN_TILE = 512
C_U, C_GLU, C_G = 0, 1024, 2048
A_Q, A_G, A_K, A_V = 3072, 4096, 5120, 5376
B_Q, B_I, B_FF, B_FB, B_G = 5632, 6656, 7680, 8704, 9728
ABC_WIDTH = 10752
D_G, D_CKV, D_CQ, D_KR = 0, 1024, 1536, 2304
D_WIDTH = 2560

VMEM_LIMIT = 56 * 1024 * 1024


def _cparams(sem):
    return pltpu.CompilerParams(dimension_semantics=sem, vmem_limit_bytes=VMEM_LIMIT)


def _silu(x):
    return x * jax.nn.sigmoid(x)


def _dot(a, b):
    return jnp.dot(a, b, preferred_element_type=F32)


def _dot_nt(a, b):
    return lax.dot_general(a, b, (((1,), (1,)), ((), ())), preferred_element_type=F32)


def _mod_kernel(c_ref, w_ref, b_ref, o_ref):
    a = _silu(c_ref[...]).astype(BF16)
    o_ref[0] = _dot(a, w_ref[0].astype(BF16)) + b_ref[0]


def _modulation(c8, w_mod, b_mod):
    depth, d, n = w_mod.shape
    tn = 512
    return pl.pallas_call(
        _mod_kernel,
        grid=(depth, n // tn),
        in_specs=[pl.BlockSpec((8, d), lambda l, j: (0, 0)),
                  pl.BlockSpec((1, d, tn), lambda l, j: (l, 0, j)),
                  pl.BlockSpec((1, 1, tn), lambda l, j: (l, 0, j))],
        out_specs=pl.BlockSpec((1, 8, tn), lambda l, j: (l, 0, j)),
        out_shape=jax.ShapeDtypeStruct((depth, 8, n), F32),
        compiler_params=_cparams(("parallel", "parallel")),
        name="modulation",
    )(c8, w_mod, b_mod.reshape(depth, 1, n))


def _norm_kernel(xc_ref, xl_ref, mod_ref, nw_ref, o_ref, *, d, n_lat):
    b = pl.program_id(0)
    t = pl.program_id(1)

    def body(x_ref, row):
        m = mod_ref[pl.ds(row, 1), :]
        shift = m[:, :d]
        gain = nw_ref[...] * (1.0 + m[:, d:2 * d])

        def sub_tile(r, carry):
            rows = pl.ds(pl.multiple_of(r * NORM_SUB_ROWS, NORM_SUB_ROWS), NORM_SUB_ROWS)
            x = x_ref[0, rows, :]
            ms = jnp.mean(x * x, axis=-1, keepdims=True)
            o_ref[0, rows, :] = (x * lax.rsqrt(ms + EPS) * gain + shift).astype(o_ref.dtype)
            return carry

        lax.fori_loop(0, ROW_TILE // NORM_SUB_ROWS, sub_tile, 0, unroll=4)

    @pl.when(t == n_lat)
    def _():
        body(xc_ref, 2)

    @pl.when(t < n_lat)
    def _():
        body(xl_ref, b)


def _norm_modulate(xc, xl, mod, nw, n_tiles):
    bsz, _, d = xc.shape
    n_lat = n_tiles - 1
    return pl.pallas_call(
        functools.partial(_norm_kernel, d=d, n_lat=n_lat),
        grid=(bsz, n_tiles),
        in_specs=[pl.BlockSpec((1, ROW_TILE, d), lambda b, t: (b, 0, 0)),
                  pl.BlockSpec((1, ROW_TILE, d), lambda b, t: (b, jnp.minimum(t, n_lat - 1), 0)),
                  pl.BlockSpec((8, 3 * d), lambda b, t: (0, 0)),
                  pl.BlockSpec((1, d), lambda b, t: (0, 0))],
        out_specs=pl.BlockSpec((1, ROW_TILE, d), lambda b, t: (b, t, 0)),
        out_shape=jax.ShapeDtypeStruct((bsz, n_tiles * ROW_TILE, d), BF16),
        compiler_params=_cparams(("parallel", "parallel")),
        name="norm_modulate",
    )(xc, xl, mod, nw.reshape(1, d))


def _mm_kernel(a_ref, w_ref, o_ref):
    o_ref[...] = _dot_nt(a_ref[...], w_ref[...]).astype(o_ref.dtype)


def _pick_tile(n, cands):
    for c in cands:
        if n % c == 0:
            return c
    raise ValueError(f"no tile for {n}")


def _matmul_nt(a, wt, out_dtype):
    m, k = a.shape
    n, _ = wt.shape
    tm = _pick_tile(m, (2176, 1088, 1024, 768, 512, 256))
    tn = _pick_tile(n, (512, 256, 128))
    return pl.pallas_call(
        _mm_kernel,
        grid=(m // tm, n // tn),
        in_specs=[pl.BlockSpec((tm, k), lambda i, j: (i, 0)),
                  pl.BlockSpec((tn, k), lambda i, j: (j, 0))],
        out_specs=pl.BlockSpec((tm, tn), lambda i, j: (i, j)),
        out_shape=jax.ShapeDtypeStruct((m, n), out_dtype),
        compiler_params=_cparams(("parallel", "parallel")),
        name="in_proj_d",
    )(a, wt)


def _mm_wcast_kernel(a_ref, w_ref, o_ref, wb_scr):
    @pl.when(pl.program_id(1) == 0)
    def _():
        wb_scr[...] = w_ref[0].astype(BF16)

    o_ref[...] = _dot_nt(a_ref[...], wb_scr[...]).astype(o_ref.dtype)


def _in_proj_abc(a, w_in_t, layer):
    m, k = a.shape
    tm = _pick_tile(m, (1088, 1024, 768, 512, 256))
    n_tiles = ABC_WIDTH // IN_TILE

    def src_tile(j):
        return jnp.where(j < 6, j + 15, jnp.where(j < 8, j - 6, jnp.where(j < 10, j - 5, jnp.where(j < 11, 2, j - 6))))

    return pl.pallas_call(
        _mm_wcast_kernel,
        grid=(n_tiles, m // tm),
        in_specs=[pl.BlockSpec((tm, k), lambda j, i: (i, 0)),
                  pl.BlockSpec((1, IN_TILE, k), lambda j, i: (layer, src_tile(j), 0))],
        out_specs=pl.BlockSpec((tm, IN_TILE), lambda j, i: (i, j)),
        out_shape=jax.ShapeDtypeStruct((m, ABC_WIDTH), BF16),
        scratch_shapes=[pltpu.VMEM((IN_TILE, k), BF16)],
        compiler_params=_cparams(("parallel", "arbitrary")),
        name="in_proj_abc",
    )(a, w_in_t)


def _rope_tables(seq, ctx_len, rope_dim):
    quarter = rope_dim // 4
    inv = (np.float32(ROPE_THETA) ** (-np.arange(quarter, dtype=np.float32) / np.float32(quarter))).astype(np.float32)
    pos = np.arange(seq, dtype=np.int32)
    ar = (pos // GRID_W).astype(np.float32)[:, None] * inv
    ac = (pos % GRID_W).astype(np.float32)[:, None] * inv
    cos = np.concatenate([np.cos(ar), np.cos(ar), np.cos(ac), np.cos(ac)], axis=-1)
    sin = np.concatenate([np.sin(ar), np.sin(ar), np.sin(ac), np.sin(ac)], axis=-1)

    def full(tab, fill):
        tab = np.pad(tab, ((0, 0), (0, 128 - rope_dim)), constant_values=fill)
        return jnp.asarray(np.concatenate([tab, np.full((ctx_len, 128), fill, np.float32)], axis=0), F32)

    swap = np.zeros((128, 128), np.float32)
    for i in range(rope_dim):
        if (i // quarter) % 2 == 0:
            swap[i + quarter, i] = -1.0
        else:
            swap[i - quarter, i] = 1.0
    return full(cos, 1.0), full(sin, 0.0), jnp.asarray(swap, BF16)


def _rope(x, cos, sin, swap):
    return x * cos + _dot(x.astype(BF16), swap) * sin


def _gqa_prep_kernel(q_ref, k_ref, v_ref, cos_ref, sin_ref, swap_ref, qn_ref, kn_ref, qo_ref, ko_ref, vo_ref):
    cos, sin, swap = cos_ref[...], sin_ref[...], swap_ref[...]

    def head(x, w, scale):
        ms = jnp.mean(x * x, axis=-1, keepdims=True)
        y = x * lax.rsqrt(ms + EPS) * w
        return (_rope(y, cos, sin, swap) * scale).astype(BF16)

    for h in range(N_HEADS):
        sl = slice(h * HEAD_DIM, (h + 1) * HEAD_DIM)
        qo_ref[0, :, sl] = head(q_ref[0, :, sl].astype(F32), qn_ref[...], HEAD_DIM ** -0.5)
    for h in range(ATT_KV_HEADS):
        sl = slice(h * HEAD_DIM, (h + 1) * HEAD_DIM)
        ko_ref[0, :, sl] = head(k_ref[0, :, sl].astype(F32), kn_ref[...], 1.0)
        _store_value_ext(vo_ref, h, v_ref[0, :, sl])


def _store_value_ext(vo_ref, h, v):
    vo_ref[0, :, 2 * h * HEAD_DIM:(2 * h + 1) * HEAD_DIM] = v
    vo_ref[0, :, (2 * h + 1) * HEAD_DIM:(2 * h + 2) * HEAD_DIM] = jnp.ones(v.shape, v.dtype)


def _gqa_prep(p3, tabs, qn, kn):
    bsz, t, _ = p3.shape
    nt = t // ROW_TILE
    kvw = ATT_KV_HEADS * HEAD_DIM
    tab_spec = pl.BlockSpec((ROW_TILE, 128), lambda b, i: (i, 0))
    swap_spec = pl.BlockSpec((128, 128), lambda b, i: (0, 0))
    vec_spec = pl.BlockSpec((1, 128), lambda b, i: (0, 0))
    return pl.pallas_call(
        _gqa_prep_kernel,
        grid=(bsz, nt),
        in_specs=[pl.BlockSpec((1, ROW_TILE, GROUP_WIDTH), lambda b, i: (b, i, A_Q // GROUP_WIDTH)),
                  pl.BlockSpec((1, ROW_TILE, kvw), lambda b, i: (b, i, A_K // kvw)),
                  pl.BlockSpec((1, ROW_TILE, kvw), lambda b, i: (b, i, A_V // kvw)),
                  tab_spec, tab_spec, swap_spec, vec_spec, vec_spec],
        out_specs=[pl.BlockSpec((1, ROW_TILE, GROUP_WIDTH), lambda b, i: (b, i, 0)),
                   pl.BlockSpec((1, ROW_TILE, kvw), lambda b, i: (b, i, 0)),
                   pl.BlockSpec((1, ROW_TILE, 2 * kvw), lambda b, i: (b, i, 0))],
        out_shape=[jax.ShapeDtypeStruct((bsz, t, GROUP_WIDTH), BF16),
                   jax.ShapeDtypeStruct((bsz, t, kvw), BF16),
                   jax.ShapeDtypeStruct((bsz, t, 2 * kvw), BF16)],
        compiler_params=_cparams(("parallel", "parallel")),
        name="gqa_prep",
    )(p3, p3, p3, *tabs, qn.reshape(1, 128), kn.reshape(1, 128))


def _attn_kernel(q_ref, k_ref, v_ref, g_ref, o_ref, *, n_heads, dq, heads_per_kv, n_lat):
    i = pl.program_id(2)
    vw = 2 * HEAD_DIM

    def attend(lo):
        def kv(j):
            g = j // heads_per_kv
            return k_ref[0, lo:, g * dq:(g + 1) * dq], v_ref[0, lo:, g * vw:(g + 1) * vw]

        s = [None] * n_heads
        p = [None] * n_heads
        for j in range(n_heads + 2):
            if j < n_heads:
                s[j] = _dot_nt(q_ref[0, :, j * dq:(j + 1) * dq], kv(j)[0])
            if 1 <= j <= n_heads:
                m = jnp.max(s[j - 1], axis=-1, keepdims=True)
                p[j - 1] = jnp.exp((s[j - 1] - m).astype(BF16))
                s[j - 1] = None
            if j >= 2:
                oe = _dot(p[j - 2], kv(j - 2)[1])
                p[j - 2] = None
                o = oe[:, :HEAD_DIM] / oe[:, HEAD_DIM:]
                sl = slice((j - 2) * HEAD_DIM, (j - 1) * HEAD_DIM)
                o_ref[0, :, sl] = (o * _silu(g_ref[0, :, sl].astype(F32))).astype(o_ref.dtype)

    @pl.when(i < n_lat)
    def _():
        attend(0)

    @pl.when(i == n_lat)
    def _():
        attend(n_lat * ROW_TILE)


def _attention(q, k, v_ext, p3, gate_off, n_heads, dq, heads_per_kv, ctx_out):
    bsz, t, _ = q.shape
    n_lat = t // ROW_TILE - 1
    nt = n_lat + (1 if ctx_out else 0)
    n_kv = n_heads // heads_per_kv
    gw = n_heads * HEAD_DIM
    n_steps = q.shape[-1] // (n_heads * dq)
    return pl.pallas_call(
        functools.partial(_attn_kernel, n_heads=n_heads, dq=dq, heads_per_kv=heads_per_kv, n_lat=n_lat),
        grid=(bsz, n_steps, nt),
        in_specs=[pl.BlockSpec((1, ROW_TILE, n_heads * dq), lambda b, h, i: (b, i, h)),
                  pl.BlockSpec((1, t, n_kv * dq), lambda b, h, i: (b, 0, h)),
                  pl.BlockSpec((1, t, n_kv * 2 * HEAD_DIM), lambda b, h, i: (b, 0, h)),
                  pl.BlockSpec((1, ROW_TILE, gw), lambda b, h, i: (b, i, gate_off // gw + h))],
        out_specs=pl.BlockSpec((1, ROW_TILE, gw), lambda b, h, i: (b, i, h)),
        out_shape=jax.ShapeDtypeStruct((bsz, nt * ROW_TILE, n_steps * gw), BF16),
        compiler_params=_cparams(("parallel", "parallel", "parallel")),
        name="attention",
    )(q, k, v_ext, p3)


def _mla_up_kernel(cq_ref, ckv_ref, kr_ref, cos_ref, sin_ref, swap_ref, cqn_ref, ckvn_ref, wq_ref, wk_ref, wv_ref,
                   qn_ref, kn_ref, qo_ref, ko_ref, vo_ref):
    cos, sin, swap = cos_ref[...], sin_ref[...], swap_ref[...]

    def rms(x, w):
        return x * lax.rsqrt(jnp.mean(x * x, axis=-1, keepdims=True) + EPS) * w

    cq = rms(cq_ref[0].astype(F32), cqn_ref[...]).astype(BF16)
    ckv = rms(ckv_ref[0].astype(F32), ckvn_ref[...]).astype(BF16)
    qf = _dot(cq, wq_ref[...])
    kf = _dot(ckv, wk_ref[...])
    vf = _dot(ckv, wv_ref[...]).astype(BF16)
    kr = kr_ref[0].astype(F32)
    kr_ss = jnp.sum(kr * kr, axis=-1, keepdims=True)
    qw1, qw2 = qn_ref[:, :128], qn_ref[:, 128:]
    kw1, kw2 = kn_ref[:, :128], kn_ref[:, 128:]
    kr_rot = _rope(kr * kw2, cos, sin, swap)
    scale = MLA_QK_DIM ** -0.5
    for h in range(N_HEADS):
        lo = h * MLA_PAD_DIM
        q1 = qf[:, lo:lo + 128]
        q2 = qf[:, lo + 128:lo + 256]
        ss = jnp.sum(q1 * q1 + q2 * q2, axis=-1, keepdims=True)
        r = lax.rsqrt(ss / MLA_QK_DIM + EPS) * scale
        qo_ref[0, :, lo:lo + 128] = (q1 * r * qw1).astype(BF16)
        qo_ref[0, :, lo + 128:lo + 256] = _rope(q2 * r * qw2, cos, sin, swap).astype(BF16)
        k1 = kf[:, h * 128:(h + 1) * 128]
        r = lax.rsqrt((jnp.sum(k1 * k1, axis=-1, keepdims=True) + kr_ss) / MLA_QK_DIM + EPS)
        ko_ref[0, :, lo:lo + 128] = (k1 * r * kw1).astype(BF16)
        ko_ref[0, :, lo + 128:lo + 256] = (kr_rot * r).astype(BF16)
        _store_value_ext(vo_ref, h, vf[:, h * 128:(h + 1) * 128])


def _mla_up(p3, tabs, cqn, ckvn, wq, wk, wv, qn, kn):
    bsz, t, _ = p3.shape
    nt = t // ROW_TILE
    hw = N_HEADS * MLA_PAD_DIM
    tab_spec = pl.BlockSpec((ROW_TILE, 128), lambda b, i: (i, 0))

    def whole(a):
        return pl.BlockSpec(a.shape, lambda b, i: (0,) * a.ndim)

    args = (cqn.reshape(1, -1), ckvn.reshape(1, -1), wq, wk, wv, qn.reshape(1, -1), kn.reshape(1, -1))
    return pl.pallas_call(
        _mla_up_kernel,
        grid=(bsz, nt),
        in_specs=[pl.BlockSpec((1, ROW_TILE, MLA_Q_RANK), lambda b, i: (b, i, D_CQ // MLA_Q_RANK)),
                  pl.BlockSpec((1, ROW_TILE, MLA_KV_RANK), lambda b, i: (b, i, D_CKV // MLA_KV_RANK)),
                  pl.BlockSpec((1, ROW_TILE, 128), lambda b, i: (b, i, D_KR // 128)),
                  tab_spec, tab_spec, pl.BlockSpec((128, 128), lambda b, i: (0, 0))] + [whole(a) for a in args],
        out_specs=[pl.BlockSpec((1, ROW_TILE, hw), lambda b, i: (b, i, 0)),
                   pl.BlockSpec((1, ROW_TILE, hw), lambda b, i: (b, i, 0)),
                   pl.BlockSpec((1, ROW_TILE, 2 * GROUP_WIDTH), lambda b, i: (b, i, 0))],
        out_shape=[jax.ShapeDtypeStruct((bsz, t, hw), BF16),
                   jax.ShapeDtypeStruct((bsz, t, hw), BF16),
                   jax.ShapeDtypeStruct((bsz, t, 2 * GROUP_WIDTH), BF16)],
        compiler_params=_cparams(("parallel", "parallel")),
        name="mla_up",
    )(p3, p3, p3, *tabs, *args)


def _hgrn_consts():
    c = HGRN_CHUNK
    t = np.arange(c)[:, None]
    s = np.arange(c)[None, :]
    cum = np.stack([s <= t, s >= t]).astype(np.float32)
    masks = np.zeros((2, HGRN_LEVELS + 1, c, c), np.float32)
    masks[:, 0] = np.eye(c)
    for lvl in range(1, HGRN_LEVELS + 1):
        m = c >> lvl
        same = (t // (2 * m)) == (s // (2 * m))
        t_up, s_up = (t // m) % 2 == 1, (s // m) % 2 == 1
        masks[0, lvl] = same & t_up & ~s_up
        masks[1, lvl] = same & ~t_up & s_up
    return jnp.asarray(cum, BF16), jnp.asarray(masks, BF16)


def _hgrn_kernel(q_ref, i_ref, ff_ref, fb_ref, gate_ref, oml_ref, llb_ref, on_ref, cum_ref, mask_ref, y_ref,
                 *scr, n_ctx_chunks, n_chunks):
    c = HGRN_CHUNK
    n_chain = 2 * HGRN_HEADS_PER_STEP
    g_scrs, o_scrs = scr[:n_chain], scr[n_chain:]
    sub = lax.broadcasted_iota(jnp.int32, (8, 128), 0)

    def level_ref(g_scr, d, lvl):
        m = c >> lvl
        blk = 2 * m
        def row(r):
            return g_scr[r:r + 1, :]
        idx = [b * blk + m - 1 + d for b in range(c // blk)]
        if blk >= 8:
            return jnp.concatenate([jnp.broadcast_to(row(r), (blk, 128)) for r in idx], axis=0)
        per = 8 // blk
        tiles = []
        for j in range(c // 8):
            tile = jnp.broadcast_to(row(idx[j * per + per - 1]), (8, 128))
            for u in range(per - 2, -1, -1):
                tile = jnp.where(sub < (u + 1) * blk, jnp.broadcast_to(row(idx[j * per + u]), (8, 128)), tile)
            tiles.append(tile)
        return jnp.concatenate(tiles, axis=0)

    chains = [(hh, d) for hh in range(HGRN_HEADS_PER_STEP) for d in range(2)]

    def chunk_rows(n):
        n_lat_chunks = n_chunks - n_ctx_chunks
        cf = jnp.where(n < n_ctx_chunks, n_lat_chunks + n, n - n_ctx_chunks)
        cb = n_chunks - 1 - n
        return [pl.ds(pl.multiple_of(cf * c, c), c), pl.ds(pl.multiple_of(cb * c, c), c)]

    def gate_decay_parts(n):
        rows_d = chunk_rows(n)
        out = []
        for hh, d in chains:
            cols = slice(hh * HEAD_DIM, (hh + 1) * HEAD_DIM)
            f = (ff_ref if d == 0 else fb_ref)[0, rows_d[d], cols].astype(F32)
            a = llb_ref[d:d + 1, cols] - f
            lf = (jnp.minimum(f, 0.0) - jnp.log(1.0 + jnp.exp(-jnp.abs(f)))
                  + jnp.maximum(a, 0.0) + jnp.log(1.0 + jnp.exp(-jnp.abs(a))))
            hi = lf.astype(BF16)
            r1 = lf - hi.astype(F32)
            mid = r1.astype(BF16)
            lo = (r1 - mid.astype(F32)).astype(BF16)
            cum = cum_ref[d]
            out.append((_dot(cum, hi), _dot(cum, mid), _dot(cum, lo)))
        return out

    def gate_qk(n):
        rows_d = chunk_rows(n)
        out = []
        for hh, d in chains:
            cols = slice(hh * HEAD_DIM, (hh + 1) * HEAD_DIM)
            qv = q_ref[0, rows_d[d], cols].astype(F32)
            qb = (_silu(qv) * (HEAD_DIM ** -0.5)).astype(BF16)
            f = (ff_ref if d == 0 else fb_ref)[0, rows_d[d], cols].astype(F32)
            kb = (oml_ref[d:d + 1, cols] / (1.0 + jnp.exp(f))).astype(BF16)
            out.append((qb, kb))
        return out

    def gate_join(qk, parts):
        return [(qb, kb, p0 + p1 + p2) for (qb, kb), (p0, p1, p2) in zip(qk, parts)]

    def body(n, carry):
        states, cur = carry
        n_next = jnp.minimum(n + 1, n_chunks - 1)
        rows_d = chunk_rows(n)
        vb, gtot, o, scores, o_intra, st_upd = {}, {}, {}, {}, {}, {}
        new = [None] * n_chain
        for ci, (hh, d) in enumerate(chains):
            g_scrs[ci][...] = cur[ci][2]

        def scores_of(group):
            for ci in group:
                hh, d = chains[ci]
                qb, kb, g = cur[ci]
                vb[ci] = i_ref[0, rows_d[d], hh * HEAD_DIM:(hh + 1) * HEAD_DIM]
                end = (c - 1) * (1 - d)
                gtot[ci] = g_scrs[ci][end:end + 1, :]
                o[ci] = _dot_nt(qb * jnp.exp(g).astype(BF16), states[ci].astype(BF16))
                scores[ci] = _dot_nt(qb, kb).astype(BF16) * mask_ref[d, 0]
            for lvl in range(1, HGRN_LEVELS + 1):
                for ci in group:
                    d = chains[ci][1]
                    qb, kb, g = cur[ci]
                    e = jnp.exp(-jnp.abs(g - level_ref(g_scrs[ci], d, lvl))).astype(BF16)
                    s_l = _dot_nt(qb * e, kb * e)
                    scores[ci] = scores[ci] + s_l.astype(BF16) * mask_ref[d, lvl]

        def issue_last(group):
            for ci in group:
                qb, kb, g = cur[ci]
                o_intra[ci] = _dot(scores[ci], vb[ci])
                kend = kb * jnp.exp(gtot[ci] - g).astype(BF16)
                vt = vb[ci].astype(F32).T.astype(BF16)
                st_upd[ci] = _dot(vt, kend)

        def finish(group):
            for ci in group:
                o_scrs[ci][rows_d[chains[ci][1]], :] = o[ci] + o_intra[ci]
                new[ci] = states[ci] * jnp.exp(gtot[ci]) + st_upd[ci]

        every = range(n_chain)
        scores_of(every)
        nxt_parts = gate_decay_parts(n_next)
        issue_last(every)
        nxt_qk = gate_qk(n_next)
        finish(every)
        return tuple(new), gate_join(nxt_qk, nxt_parts)

    zero = jnp.zeros((HEAD_DIM, HEAD_DIM), F32)
    lax.fori_loop(0, n_chunks, body, ((zero,) * n_chain, gate_join(gate_qk(0), gate_decay_parts(0))), unroll=8)

    n_row_tiles = (n_chunks * c) // ROW_TILE

    def readout(i, carry):
        rows = pl.ds(pl.multiple_of(i * ROW_TILE, ROW_TILE), ROW_TILE)
        for hh in range(HGRN_HEADS_PER_STEP):
            cols = slice(hh * HEAD_DIM, (hh + 1) * HEAD_DIM)
            o = o_scrs[2 * hh][rows, :] + o_scrs[2 * hh + 1][rows, :]
            o = o * lax.rsqrt(jnp.mean(o * o, axis=-1, keepdims=True) + EPS) * on_ref[:, cols]
            y_ref[0, rows, cols] = (o * _silu(gate_ref[0, rows, cols].astype(F32))).astype(y_ref.dtype)
        return carry

    lax.fori_loop(0, n_row_tiles, readout, 0)


def _hgrn(p3, one_minus_lb, log_lb, o_norm, ctx_len):
    bsz, t, _ = p3.shape
    cum, masks = _hgrn_consts()
    bw = HGRN_HEADS_PER_STEP * HEAD_DIM
    n_chain = 2 * HGRN_HEADS_PER_STEP

    def col(off):
        return pl.BlockSpec((1, t, bw), lambda b, h: (b, 0, off // bw + h))

    return pl.pallas_call(
        functools.partial(_hgrn_kernel, n_ctx_chunks=ctx_len // HGRN_CHUNK, n_chunks=t // HGRN_CHUNK),
        grid=(bsz, GROUP_WIDTH // bw),
        in_specs=[col(B_Q), col(B_I), col(B_FF), col(B_FB), col(B_G),
                  pl.BlockSpec((2, bw), lambda b, h: (0, h)),
                  pl.BlockSpec((2, bw), lambda b, h: (0, h)),
                  pl.BlockSpec((1, bw), lambda b, h: (0, h)),
                  pl.BlockSpec(cum.shape, lambda b, h: (0, 0, 0)),
                  pl.BlockSpec(masks.shape, lambda b, h: (0, 0, 0, 0))],
        out_specs=pl.BlockSpec((1, t, bw), lambda b, h: (b, 0, h)),
        out_shape=jax.ShapeDtypeStruct((bsz, t, GROUP_WIDTH), BF16),
        scratch_shapes=([pltpu.VMEM((HGRN_CHUNK, HEAD_DIM), F32)] * n_chain
                        + [pltpu.VMEM((t, HEAD_DIM), F32)] * n_chain),
        compiler_params=_cparams(("parallel", "parallel")),
        name="hgrn2",
    )(p3, p3, p3, p3, p3, one_minus_lb, log_lb, o_norm.reshape(1, -1), cum, masks)


def _conv_kernel(u_ref, up_ref, un_ref, gl_ref, glp_ref, gln_ref, gate_ref, w_ref, cb_ref, lw_ref, lb_ref, y_ref,
                 xs_scr, acc_scr, *, n_lat):
    t = pl.program_id(1)
    h = CONV_HALO
    prev_ok = jnp.logical_and(t >= 1, t < n_lat).astype(F32)
    next_ok = (t < n_lat - 1).astype(F32)
    n_lane_blocks = GROUP_WIDTH // 128
    x_prev = up_ref[0].astype(F32) * jax.nn.sigmoid(glp_ref[0].astype(F32)) * prev_ok
    x_main = u_ref[0].astype(F32) * jax.nn.sigmoid(gl_ref[0].astype(F32))
    x_next = un_ref[0].astype(F32) * jax.nn.sigmoid(gln_ref[0].astype(F32)) * next_ok
    for cb in range(n_lane_blocks):
        sl = slice(cb * 128, (cb + 1) * 128)
        xs_scr[cb, 0:h, :] = x_prev[:, sl]
        xs_scr[cb, h:h + ROW_TILE, :] = x_main[:, sl]
        xs_scr[cb, h + ROW_TILE:, :] = x_next[:, sl]
    rc = 64
    base = h - CONV_WIDTH // 2

    def lane_block(cb, carry):
        for r in range(ROW_TILE // rc):
            acc = jnp.zeros((rc, 128), F32)
            for tap in range(CONV_WIDTH):
                lo = base + r * rc + tap
                acc = acc + xs_scr[cb, lo:lo + rc, :] * w_ref[cb, tap:tap + 1, :]
            acc_scr[cb, r * rc:(r + 1) * rc, :] = acc
        return carry

    lax.fori_loop(0, n_lane_blocks, lane_block, 0)
    y = jnp.concatenate([acc_scr[cb] for cb in range(n_lane_blocks)], axis=1) + cb_ref[...]
    mu = jnp.mean(y, axis=-1, keepdims=True)
    yc = y - mu
    var = jnp.mean(yc * yc, axis=-1, keepdims=True)
    z = yc * lax.rsqrt(var + EPS) * lw_ref[...] + lb_ref[...]
    y_ref[0] = (_silu(z) * _silu(gate_ref[0].astype(F32))).astype(y_ref.dtype)


def _conv(p3, conv_w, conv_b, ln_w, ln_b, ctx_out):
    bsz, t, _ = p3.shape
    n_lat = t // ROW_TILE - 1
    nt = n_lat + (1 if ctx_out else 0)
    hpt = ROW_TILE // CONV_HALO
    n_halo = t // CONV_HALO
    n_lane_blocks = GROUP_WIDTH // 128
    wpad = jnp.pad(conv_w, ((0, 32 - CONV_WIDTH), (0, 0))).reshape(32, n_lane_blocks, 128).transpose(1, 0, 2)

    def main(off):
        return pl.BlockSpec((1, ROW_TILE, GROUP_WIDTH), lambda b, i: (b, i, off // GROUP_WIDTH))

    def prev(off):
        return pl.BlockSpec((1, CONV_HALO, GROUP_WIDTH),
                            lambda b, i: (b, jnp.maximum(i * hpt - 1, 0), off // GROUP_WIDTH))

    def nxt(off):
        return pl.BlockSpec((1, CONV_HALO, GROUP_WIDTH),
                            lambda b, i: (b, jnp.minimum((i + 1) * hpt, n_halo - 1), off // GROUP_WIDTH))

    vec = pl.BlockSpec((1, GROUP_WIDTH), lambda b, i: (0, 0))
    return pl.pallas_call(
        functools.partial(_conv_kernel, n_lat=n_lat),
        grid=(bsz, nt),
        in_specs=[main(C_U), prev(C_U), nxt(C_U), main(C_GLU), prev(C_GLU), nxt(C_GLU), main(C_G),
                  pl.BlockSpec((n_lane_blocks, 32, 128), lambda b, i: (0, 0, 0)), vec, vec, vec],
        out_specs=pl.BlockSpec((1, ROW_TILE, GROUP_WIDTH), lambda b, i: (b, i, 0)),
        out_shape=jax.ShapeDtypeStruct((bsz, nt * ROW_TILE, GROUP_WIDTH), BF16),
        scratch_shapes=[pltpu.VMEM((n_lane_blocks, ROW_TILE + 2 * CONV_HALO, 128), F32),
                        pltpu.VMEM((n_lane_blocks, ROW_TILE, 128), F32)],
        compiler_params=_cparams(("parallel", "parallel")),
        name="conformer_conv",
    )(p3, p3, p3, p3, p3, p3, p3, wpad, conv_b.reshape(1, -1), ln_w.reshape(1, -1), ln_b.reshape(1, -1))


def _out_kernel(ya_ref, yb_ref, yc_ref, yd_ref, w_ref, x_ref, gate_ref, o_ref, wb_scr, *, ctx_rows):
    b = pl.program_id(1)
    gw = GROUP_WIDTH

    @pl.when(jnp.logical_and(b == 0, pl.program_id(2) == 0))
    def _():
        wb_scr[...] = w_ref[0].astype(BF16)

    acc = _dot(ya_ref[0], wb_scr[0:gw, :])
    acc = acc + _dot(yb_ref[0], wb_scr[gw:2 * gw, :])
    acc = acc + _dot(yc_ref[0], wb_scr[2 * gw:3 * gw, :])
    acc = acc + _dot(yd_ref[0], wb_scr[3 * gw:4 * gw, :])

    row = 2 if ctx_rows else b
    o_ref[0] = x_ref[0] + gate_ref[pl.ds(row, 1), :] * acc


def _out_proj(ys, w_out, layer, x, mod, ctx_rows):
    bsz, rows, d = x.shape
    tm = ROW_TILE if ctx_rows else _pick_tile(rows, (1024, 512, 256))
    y_off = (ys[0].shape[1] - rows) // tm if ctx_rows else 0
    tn = 1024 if ctx_rows else 512
    y_spec = pl.BlockSpec((1, tm, GROUP_WIDTH), lambda j, b, i: (b, i + y_off, 0))
    return pl.pallas_call(
        functools.partial(_out_kernel, ctx_rows=ctx_rows),
        grid=(d // tn, bsz, rows // tm),
        in_specs=[y_spec, y_spec, y_spec, y_spec,
                  pl.BlockSpec((1, 4 * GROUP_WIDTH, tn), lambda j, b, i: (layer, 0, j)),
                  pl.BlockSpec((1, tm, tn), lambda j, b, i: (b, i, j)),
                  pl.BlockSpec((8, tn), lambda j, b, i: (0, 2 * d // tn + j))],
        out_specs=pl.BlockSpec((1, tm, tn), lambda j, b, i: (b, i, j)),
        out_shape=jax.ShapeDtypeStruct((bsz, rows, d), F32),
        scratch_shapes=[pltpu.VMEM((4 * GROUP_WIDTH, tn), BF16)],
        compiler_params=_cparams(("parallel", "arbitrary", "arbitrary")),
        name="out_proj",
    )(*ys, w_out, x, mod)


D_PACK_ROWS = 64


def _pack_d_kernel(w_ref, o_ref, *, n_src_blocks):
    @pl.when(pl.program_id(0) < n_src_blocks)
    def _():
        o_ref[...] = w_ref[0].astype(BF16)

    @pl.when(pl.program_id(0) >= n_src_blocks)
    def _():
        o_ref[...] = jnp.zeros(o_ref.shape, o_ref.dtype)


def _group_d_weight(w_in_t, layer):
    k = w_in_t.shape[2]
    r = D_PACK_ROWS
    base = ABC_WIDTH // r
    src_cq, src_ckv, src_kr, src_g = base, base + 768 // r, base + 1280 // r, base + 1344 // r
    n_src_blocks = (D_KR + MLA_ROPE_DIM) // r

    def src(j):
        return jnp.where(j < D_CKV // r, src_g + j,
                         jnp.where(j < D_CQ // r, src_ckv + j - D_CKV // r,
                                   jnp.where(j < D_KR // r, src_cq + j - D_CQ // r, src_kr)))

    return pl.pallas_call(
        functools.partial(_pack_d_kernel, n_src_blocks=n_src_blocks),
        grid=(D_WIDTH // r,),
        in_specs=[pl.BlockSpec((1, r, k), lambda j: (layer, src(j), 0))],
        out_specs=pl.BlockSpec((r, k), lambda j: (j, 0)),
        out_shape=jax.ShapeDtypeStruct((D_WIDTH, k), BF16),
        compiler_params=_cparams(("parallel",)),
        name="pack_group_d",
    )(w_in_t)


def kernel(x, c, ctx, c_ctx, w_mod, b_mod, norm_w, w_in, w_out, att_q_norm, att_k_norm, hgrn_lb_logits, hgrn_o_norm,
           conv_w, conv_b, conv_ln_w, conv_ln_b, mla_q_norm, mla_kv_norm, mla_w_uq, mla_w_ukv, mla_qk_q_norm,
           mla_qk_k_norm):
    bsz, seq, d = x.shape
    ctx_len = ctx.shape[1]
    depth = w_mod.shape[0]
    t = ctx_len + seq
    assert ctx_len == ROW_TILE and seq % ROW_TILE == 0 and bsz <= 2
    n_tiles = t // ROW_TILE

    lb_all = jnp.cumsum(jax.nn.softmax(hgrn_lb_logits.astype(F32), axis=0), axis=0)
    lb_all = lb_all - lb_all[0:1]

    c8 = jnp.zeros((8, d), F32).at[0:bsz].set(c).at[2].set(c_ctx)
    mod_all = _modulation(c8, w_mod, b_mod)

    gqa_tabs = _rope_tables(seq, ctx_len, HEAD_DIM)
    mla_tabs = _rope_tables(seq, ctx_len, MLA_ROPE_DIM)

    w_in_t = jnp.swapaxes(w_in, 1, 2)

    xc, xl = ctx, x
    for l in range(depth):
        ctx_out = l < depth - 1
        mod = mod_all[l]
        h = _norm_modulate(xc, xl, mod, norm_w[l], n_tiles)
        h2 = h.reshape(bsz * t, d)
        p3 = _in_proj_abc(h2, w_in_t, l).reshape(bsz, t, ABC_WIDTH)
        pd3 = _matmul_nt(h2, _group_d_weight(w_in_t, l), BF16).reshape(bsz, t, D_WIDTH)

        qa, ka, va = _gqa_prep(p3, gqa_tabs, att_q_norm[l], att_k_norm[l])
        ya = _attention(qa, ka, va, p3, A_G, N_HEADS, HEAD_DIM, N_HEADS // ATT_KV_HEADS, ctx_out)

        yb = _hgrn(p3, 1.0 - lb_all[l], jnp.log(lb_all[l]), hgrn_o_norm[l], ctx_len)

        yc = _conv(p3, conv_w[l], conv_b[l], conv_ln_w[l], conv_ln_b[l], ctx_out)

        wq = jnp.pad(mla_w_uq[l].reshape(MLA_Q_RANK, N_HEADS, MLA_QK_DIM),
                     ((0, 0), (0, 0), (0, MLA_PAD_DIM - MLA_QK_DIM))).reshape(MLA_Q_RANK, -1).astype(BF16)
        wkv = mla_w_ukv[l].reshape(MLA_KV_RANK, N_HEADS, 2 * HEAD_DIM)
        wk = wkv[:, :, :HEAD_DIM].reshape(MLA_KV_RANK, -1).astype(BF16)
        wv = wkv[:, :, HEAD_DIM:].reshape(MLA_KV_RANK, -1).astype(BF16)
        qn = jnp.pad(mla_qk_q_norm[l], (0, MLA_PAD_DIM - MLA_QK_DIM))
        kn = jnp.pad(mla_qk_k_norm[l], (0, MLA_PAD_DIM - MLA_QK_DIM))
        qd, kd, vd = _mla_up(pd3, mla_tabs, mla_q_norm[l], mla_kv_norm[l], wq, wk, wv, qn, kn)
        yd = _attention(qd, kd, vd, pd3, D_G, 4, MLA_PAD_DIM, 1, ctx_out)

        ys = (ya, yb, yc, yd)
        xl_new = _out_proj(ys, w_out, l, xl, mod, False)
        if ctx_out:
            xc = _out_proj(ys, w_out, l, xc, mod, True)
        xl = xl_new
    return xl
```

```python
import functools

import numpy as np
import jax
import jax.numpy as jnp
from jax import lax
from jax.experimental import pallas as pl
from jax.experimental.pallas import tpu as pltpu

F32 = jnp.float32
BF16 = jnp.bfloat16

EPS = 1e-6
ROPE_THETA = 10000.0
GRID_W = 64
HEAD_DIM = 128
GROUP_WIDTH = 1024
N_HEADS = 8
ATT_KV_HEADS = 2
MLA_Q_RANK = 768
MLA_KV_RANK = 512
MLA_ROPE_DIM = 64
MLA_QK_DIM = 192
MLA_PAD_DIM = 256
CONV_WIDTH = 31
CONV_HALO = 16
HGRN_CHUNK = 128
HGRN_LEVELS = 7
HGRN_HEADS_PER_STEP = 2
ROW_TILE = 256
NORM_SUB_ROWS = 16

IN_TILE = 512
C_U, C_GLU, C_G = 0, 1024, 2048
A_Q, A_G, A_K, A_V = 3072, 4096, 5120, 5376
B_Q, B_I, B_FF, B_FB, B_G = 5632, 6656, 7680, 8704, 9728
ABC_WIDTH = 10752
D_G, D_CKV, D_CQ, D_KR = 0, 1024, 1536, 2304
D_WIDTH = 2560

VMEM_LIMIT = 56 * 1024 * 1024


def _cparams(sem):
    return pltpu.CompilerParams(dimension_semantics=sem, vmem_limit_bytes=VMEM_LIMIT)


def _silu(x):
    return x * jax.nn.sigmoid(x)


def _dot(a, b):
    return jnp.dot(a, b, preferred_element_type=F32)


def _dot_nt(a, b):
    return lax.dot_general(a, b, (((1,), (1,)), ((), ())), preferred_element_type=F32)


def _mod_kernel(c_ref, w_ref, b_ref, o_ref):
    a = _silu(c_ref[...]).astype(BF16)
    o_ref[0] = _dot(a, w_ref[0].astype(BF16)) + b_ref[0]


def _modulation(c8, w_mod, b_mod):
    depth, d, n = w_mod.shape
    tn = 512
    return pl.pallas_call(
        _mod_kernel,
        grid=(depth, n // tn),
        in_specs=[pl.BlockSpec((8, d), lambda l, j: (0, 0)),
                  pl.BlockSpec((1, d, tn), lambda l, j: (l, 0, j)),
                  pl.BlockSpec((1, 1, tn), lambda l, j: (l, 0, j))],
        out_specs=pl.BlockSpec((1, 8, tn), lambda l, j: (l, 0, j)),
        out_shape=jax.ShapeDtypeStruct((depth, 8, n), F32),
        compiler_params=_cparams(("parallel", "parallel")),
        name="modulation",
    )(c8, w_mod, b_mod.reshape(depth, 1, n))


NORM_RING = 3


def _norm_kernel(xc_hbm, xl_hbm, mod_ref, nw_ref, o_ref, buf, sem, *, d, n_lat, n_steps):
    b = pl.program_id(0)
    t = pl.program_id(1)
    n_tiles = n_lat + 1
    s = b * n_tiles + t

    def tile_copy(step, slot):
        sb, st = step // n_tiles, step % n_tiles
        lat = pltpu.make_async_copy(xl_hbm.at[sb, pl.ds(jnp.minimum(st, n_lat - 1) * ROW_TILE, ROW_TILE), :],
                                    buf.at[slot], sem.at[slot])
        ctx = pltpu.make_async_copy(xc_hbm.at[sb], buf.at[slot], sem.at[slot])
        return st < n_lat, lat, ctx

    def start(step, slot):
        is_lat, lat, ctx = tile_copy(step, slot)

        @pl.when(is_lat)
        def _():
            lat.start()

        @pl.when(jnp.logical_not(is_lat))
        def _():
            ctx.start()

    def wait(step, slot):
        is_lat, lat, ctx = tile_copy(step, slot)

        @pl.when(is_lat)
        def _():
            lat.wait()

        @pl.when(jnp.logical_not(is_lat))
        def _():
            ctx.wait()

    @pl.when(s == 0)
    def _():
        for k in range(min(NORM_RING, n_steps)):
            start(k, k)

    slot = s % NORM_RING
    wait(s, slot)

    row = jnp.where(t == n_lat, 2, b)
    m = mod_ref[pl.ds(row, 1), :]
    shift = m[:, :d]
    gain = nw_ref[...] * (1.0 + m[:, d:2 * d])

    def sub_tile(r, carry):
        rows = pl.ds(pl.multiple_of(r * NORM_SUB_ROWS, NORM_SUB_ROWS), NORM_SUB_ROWS)
        x = buf[slot, rows, :]
        ms = jnp.mean(x * x, axis=-1, keepdims=True)
        o_ref[0, rows, :] = (x * lax.rsqrt(ms + EPS) * gain + shift).astype(o_ref.dtype)
        return carry

    lax.fori_loop(0, ROW_TILE // NORM_SUB_ROWS, sub_tile, 0, unroll=4)

    @pl.when(s + NORM_RING < n_steps)
    def _():
        start(s + NORM_RING, slot)


def _norm_modulate(xc, xl, mod, nw, n_tiles):
    bsz, _, d = xc.shape
    n_lat = n_tiles - 1
    return pl.pallas_call(
        functools.partial(_norm_kernel, d=d, n_lat=n_lat, n_steps=bsz * n_tiles),
        grid=(bsz, n_tiles),
        in_specs=[pl.BlockSpec(memory_space=pl.ANY),
                  pl.BlockSpec(memory_space=pl.ANY),
                  pl.BlockSpec((8, 3 * d), lambda b, t: (0, 0)),
                  pl.BlockSpec((1, d), lambda b, t: (0, 0))],
        out_specs=pl.BlockSpec((1, ROW_TILE, d), lambda b, t: (b, t, 0)),
        out_shape=jax.ShapeDtypeStruct((bsz, n_tiles * ROW_TILE, d), BF16),
        scratch_shapes=[pltpu.VMEM((NORM_RING, ROW_TILE, d), F32), pltpu.SemaphoreType.DMA((NORM_RING,))],
        compiler_params=_cparams(("arbitrary", "arbitrary")),
        name="norm_modulate",
    )(xc, xl, mod, nw.reshape(1, d))


def _mm_kernel(a_ref, w_ref, o_ref):
    o_ref[...] = _dot_nt(a_ref[...], w_ref[...]).astype(o_ref.dtype)


def _pick_tile(n, cands):
    for c in cands:
        if n % c == 0:
            return c
    raise ValueError(f"no tile for {n}")


def _matmul_nt(a, wt, out_dtype):
    m, k = a.shape
    n, _ = wt.shape
    tm = _pick_tile(m, (2176, 1088, 1024, 768, 512, 256))
    tn = _pick_tile(n, (512, 256, 128))
    return pl.pallas_call(
        _mm_kernel,
        grid=(m // tm, n // tn),
        in_specs=[pl.BlockSpec((tm, k), lambda i, j: (i, 0)),
                  pl.BlockSpec((tn, k), lambda i, j: (j, 0))],
        out_specs=pl.BlockSpec((tm, tn), lambda i, j: (i, j)),
        out_shape=jax.ShapeDtypeStruct((m, n), out_dtype),
        compiler_params=_cparams(("parallel", "parallel")),
        name="in_proj_d",
    )(a, wt)


def _mm_wcast_kernel(a_ref, w_ref, o_ref, wb_scr):
    @pl.when(pl.program_id(1) == 0)
    def _():
        wb_scr[...] = w_ref[0].astype(BF16)

    o_ref[...] = _dot_nt(a_ref[...], wb_scr[...]).astype(o_ref.dtype)


def _in_proj_abc(a, w_in_t, layer):
    m, k = a.shape
    tm = _pick_tile(m, (1088, 1024, 768, 512, 256))
    n_tiles = ABC_WIDTH // IN_TILE

    def src_tile(j):
        return jnp.where(j < 6, j + 15, jnp.where(j < 8, j - 6, jnp.where(j < 10, j - 5, jnp.where(j < 11, 2, j - 6))))

    return pl.pallas_call(
        _mm_wcast_kernel,
        grid=(n_tiles, m // tm),
        in_specs=[pl.BlockSpec((tm, k), lambda j, i: (i, 0)),
                  pl.BlockSpec((1, IN_TILE, k), lambda j, i: (layer, src_tile(j), 0))],
        out_specs=pl.BlockSpec((tm, IN_TILE), lambda j, i: (i, j)),
        out_shape=jax.ShapeDtypeStruct((m, ABC_WIDTH), BF16),
        scratch_shapes=[pltpu.VMEM((IN_TILE, k), BF16)],
        compiler_params=_cparams(("parallel", "arbitrary")),
        name="in_proj_abc",
    )(a, w_in_t)


def _rope_tables(seq, ctx_len, rope_dim):
    quarter = rope_dim // 4
    inv = (np.float32(ROPE_THETA) ** (-np.arange(quarter, dtype=np.float32) / np.float32(quarter))).astype(np.float32)
    pos = np.arange(seq, dtype=np.int32)
    ar = (pos // GRID_W).astype(np.float32)[:, None] * inv
    ac = (pos % GRID_W).astype(np.float32)[:, None] * inv
    cos = np.concatenate([np.cos(ar), np.cos(ar), np.cos(ac), np.cos(ac)], axis=-1)
    sin = np.concatenate([np.sin(ar), np.sin(ar), np.sin(ac), np.sin(ac)], axis=-1)

    def full(tab, fill):
        tab = np.pad(tab, ((0, 0), (0, 128 - rope_dim)), constant_values=fill)
        return jnp.asarray(np.concatenate([tab, np.full((ctx_len, 128), fill, np.float32)], axis=0), F32)

    swap = np.zeros((128, 128), np.float32)
    for i in range(rope_dim):
        if (i // quarter) % 2 == 0:
            swap[i + quarter, i] = -1.0
        else:
            swap[i - quarter, i] = 1.0
    return full(cos, 1.0), full(sin, 0.0), jnp.asarray(swap, BF16)


def _rope(x, cos, sin, swap):
    return x * cos + _dot(x.astype(BF16), swap) * sin


def _gqa_prep_kernel(q_ref, k_ref, v_ref, cos_ref, sin_ref, swap_ref, qn_ref, kn_ref, qo_ref, ko_ref, vo_ref):
    cos, sin, swap = cos_ref[...], sin_ref[...], swap_ref[...]

    def head(x, w, scale):
        ms = jnp.mean(x * x, axis=-1, keepdims=True)
        y = x * lax.rsqrt(ms + EPS) * w
        return (_rope(y, cos, sin, swap) * scale).astype(BF16)

    for h in range(N_HEADS):
        sl = slice(h * HEAD_DIM, (h + 1) * HEAD_DIM)
        qo_ref[0, :, sl] = head(q_ref[0, :, sl].astype(F32), qn_ref[...], HEAD_DIM ** -0.5)
    for h in range(ATT_KV_HEADS):
        sl = slice(h * HEAD_DIM, (h + 1) * HEAD_DIM)
        ko_ref[0, :, sl] = head(k_ref[0, :, sl].astype(F32), kn_ref[...], 1.0)
        _store_value_ext(vo_ref, h, v_ref[0, :, sl])


def _store_value_ext(vo_ref, h, v):
    vo_ref[0, :, 2 * h * HEAD_DIM:(2 * h + 1) * HEAD_DIM] = v
    vo_ref[0, :, (2 * h + 1) * HEAD_DIM:(2 * h + 2) * HEAD_DIM] = jnp.ones(v.shape, v.dtype)


def _gqa_prep(p3, tabs, qn, kn):
    bsz, t, _ = p3.shape
    nt = t // ROW_TILE
    kvw = ATT_KV_HEADS * HEAD_DIM
    tab_spec = pl.BlockSpec((ROW_TILE, 128), lambda b, i: (i, 0))
    swap_spec = pl.BlockSpec((128, 128), lambda b, i: (0, 0))
    vec_spec = pl.BlockSpec((1, 128), lambda b, i: (0, 0))
    return pl.pallas_call(
        _gqa_prep_kernel,
        grid=(bsz, nt),
        in_specs=[pl.BlockSpec((1, ROW_TILE, GROUP_WIDTH), lambda b, i: (b, i, A_Q // GROUP_WIDTH)),
                  pl.BlockSpec((1, ROW_TILE, kvw), lambda b, i: (b, i, A_K // kvw)),
                  pl.BlockSpec((1, ROW_TILE, kvw), lambda b, i: (b, i, A_V // kvw)),
                  tab_spec, tab_spec, swap_spec, vec_spec, vec_spec],
        out_specs=[pl.BlockSpec((1, ROW_TILE, GROUP_WIDTH), lambda b, i: (b, i, 0)),
                   pl.BlockSpec((1, ROW_TILE, kvw), lambda b, i: (b, i, 0)),
                   pl.BlockSpec((1, ROW_TILE, 2 * kvw), lambda b, i: (b, i, 0))],
        out_shape=[jax.ShapeDtypeStruct((bsz, t, GROUP_WIDTH), BF16),
                   jax.ShapeDtypeStruct((bsz, t, kvw), BF16),
                   jax.ShapeDtypeStruct((bsz, t, 2 * kvw), BF16)],
        compiler_params=_cparams(("parallel", "parallel")),
        name="gqa_prep",
    )(p3, p3, p3, *tabs, qn.reshape(1, 128), kn.reshape(1, 128))


def _attn_kernel(q_ref, k_ref, v_ref, g_ref, o_ref, *, n_heads, dq, heads_per_kv, n_lat):
    i = pl.program_id(2)
    vw = 2 * HEAD_DIM

    def attend(lo):
        def kv(j):
            g = j // heads_per_kv
            return k_ref[0, lo:, g * dq:(g + 1) * dq], v_ref[0, lo:, g * vw:(g + 1) * vw]

        s = [None] * n_heads
        p = [None] * n_heads
        for j in range(n_heads + 2):
            if j < n_heads:
                s[j] = _dot_nt(q_ref[0, :, j * dq:(j + 1) * dq], kv(j)[0])
            if 1 <= j <= n_heads:
                m = jnp.max(s[j - 1], axis=-1, keepdims=True)
                p[j - 1] = jnp.exp((s[j - 1] - m).astype(BF16))
                s[j - 1] = None
            if j >= 2:
                oe = _dot(p[j - 2], kv(j - 2)[1])
                p[j - 2] = None
                o = oe[:, :HEAD_DIM] / oe[:, HEAD_DIM:]
                sl = slice((j - 2) * HEAD_DIM, (j - 1) * HEAD_DIM)
                o_ref[0, :, sl] = (o * _silu(g_ref[0, :, sl].astype(F32))).astype(o_ref.dtype)

    @pl.when(i < n_lat)
    def _():
        attend(0)

    @pl.when(i == n_lat)
    def _():
        attend(n_lat * ROW_TILE)


def _attention(q, k, v_ext, p3, gate_off, n_heads, dq, heads_per_kv, ctx_out):
    bsz, t, _ = q.shape
    n_lat = t // ROW_TILE - 1
    nt = n_lat + (1 if ctx_out else 0)
    n_kv = n_heads // heads_per_kv
    gw = n_heads * HEAD_DIM
    n_steps = q.shape[-1] // (n_heads * dq)
    return pl.pallas_call(
        functools.partial(_attn_kernel, n_heads=n_heads, dq=dq, heads_per_kv=heads_per_kv, n_lat=n_lat),
        grid=(bsz, n_steps, nt),
        in_specs=[pl.BlockSpec((1, ROW_TILE, n_heads * dq), lambda b, h, i: (b, i, h)),
                  pl.BlockSpec((1, t, n_kv * dq), lambda b, h, i: (b, 0, h)),
                  pl.BlockSpec((1, t, n_kv * 2 * HEAD_DIM), lambda b, h, i: (b, 0, h)),
                  pl.BlockSpec((1, ROW_TILE, gw), lambda b, h, i: (b, i, gate_off // gw + h))],
        out_specs=pl.BlockSpec((1, ROW_TILE, gw), lambda b, h, i: (b, i, h)),
        out_shape=jax.ShapeDtypeStruct((bsz, nt * ROW_TILE, n_steps * gw), BF16),
        compiler_params=_cparams(("parallel", "parallel", "parallel")),
        name="attention",
    )(q, k, v_ext, p3)


def _mla_up_kernel(cq_ref, ckv_ref, kr_ref, cos_ref, sin_ref, swap_ref, cqn_ref, ckvn_ref, wq_ref, wk_ref, wv_ref,
                   qn_ref, kn_ref, qo_ref, ko_ref, vo_ref):
    cos, sin, swap = cos_ref[...], sin_ref[...], swap_ref[...]

    def rms(x, w):
        return x * lax.rsqrt(jnp.mean(x * x, axis=-1, keepdims=True) + EPS) * w

    cq = rms(cq_ref[0].astype(F32), cqn_ref[...]).astype(BF16)
    ckv = rms(ckv_ref[0].astype(F32), ckvn_ref[...]).astype(BF16)
    qf = _dot(cq, wq_ref[...])
    kf = _dot(ckv, wk_ref[...])
    vf = _dot(ckv, wv_ref[...]).astype(BF16)
    kr = kr_ref[0].astype(F32)
    kr_ss = jnp.sum(kr * kr, axis=-1, keepdims=True)
    qw1, qw2 = qn_ref[:, :128], qn_ref[:, 128:]
    kw1, kw2 = kn_ref[:, :128], kn_ref[:, 128:]
    kr_rot = _rope(kr * kw2, cos, sin, swap)
    scale = MLA_QK_DIM ** -0.5
    for h in range(N_HEADS):
        lo = h * MLA_PAD_DIM
        q1 = qf[:, lo:lo + 128]
        q2 = qf[:, lo + 128:lo + 256]
        ss = jnp.sum(q1 * q1 + q2 * q2, axis=-1, keepdims=True)
        r = lax.rsqrt(ss / MLA_QK_DIM + EPS) * scale
        qo_ref[0, :, lo:lo + 128] = (q1 * r * qw1).astype(BF16)
        qo_ref[0, :, lo + 128:lo + 256] = _rope(q2 * r * qw2, cos, sin, swap).astype(BF16)
        k1 = kf[:, h * 128:(h + 1) * 128]
        r = lax.rsqrt((jnp.sum(k1 * k1, axis=-1, keepdims=True) + kr_ss) / MLA_QK_DIM + EPS)
        ko_ref[0, :, lo:lo + 128] = (k1 * r * kw1).astype(BF16)
        ko_ref[0, :, lo + 128:lo + 256] = (kr_rot * r).astype(BF16)
        _store_value_ext(vo_ref, h, vf[:, h * 128:(h + 1) * 128])


def _mla_up(p3, tabs, cqn, ckvn, wq, wk, wv, qn, kn):
    bsz, t, _ = p3.shape
    nt = t // ROW_TILE
    hw = N_HEADS * MLA_PAD_DIM
    tab_spec = pl.BlockSpec((ROW_TILE, 128), lambda b, i: (i, 0))

    def whole(a):
        return pl.BlockSpec(a.shape, lambda b, i: (0,) * a.ndim)

    args = (cqn.reshape(1, -1), ckvn.reshape(1, -1), wq, wk, wv, qn.reshape(1, -1), kn.reshape(1, -1))
    return pl.pallas_call(
        _mla_up_kernel,
        grid=(bsz, nt),
        in_specs=[pl.BlockSpec((1, ROW_TILE, MLA_Q_RANK), lambda b, i: (b, i, D_CQ // MLA_Q_RANK)),
                  pl.BlockSpec((1, ROW_TILE, MLA_KV_RANK), lambda b, i: (b, i, D_CKV // MLA_KV_RANK)),
                  pl.BlockSpec((1, ROW_TILE, 128), lambda b, i: (b, i, D_KR // 128)),
                  tab_spec, tab_spec, pl.BlockSpec((128, 128), lambda b, i: (0, 0))] + [whole(a) for a in args],
        out_specs=[pl.BlockSpec((1, ROW_TILE, hw), lambda b, i: (b, i, 0)),
                   pl.BlockSpec((1, ROW_TILE, hw), lambda b, i: (b, i, 0)),
                   pl.BlockSpec((1, ROW_TILE, 2 * GROUP_WIDTH), lambda b, i: (b, i, 0))],
        out_shape=[jax.ShapeDtypeStruct((bsz, t, hw), BF16),
                   jax.ShapeDtypeStruct((bsz, t, hw), BF16),
                   jax.ShapeDtypeStruct((bsz, t, 2 * GROUP_WIDTH), BF16)],
        compiler_params=_cparams(("parallel", "parallel")),
        name="mla_up",
    )(p3, p3, p3, *tabs, *args)


def _hgrn_consts():
    c = HGRN_CHUNK
    t = np.arange(c)[:, None]
    s = np.arange(c)[None, :]
    cum = np.stack([s <= t, s >= t]).astype(np.float32)
    masks = np.zeros((2, HGRN_LEVELS + 1, c, c), np.float32)
    masks[:, 0] = np.eye(c)
    for lvl in range(1, HGRN_LEVELS + 1):
        m = c >> lvl
        same = (t // (2 * m)) == (s // (2 * m))
        t_up, s_up = (t // m) % 2 == 1, (s // m) % 2 == 1
        masks[0, lvl] = same & t_up & ~s_up
        masks[1, lvl] = same & ~t_up & s_up
    return jnp.asarray(cum, BF16), jnp.asarray(masks, BF16)


def _hgrn_kernel(q_ref, i_ref, ff_ref, fb_ref, gate_ref, oml_ref, llb_ref, on_ref, cum_ref, mask_ref, y_ref,
                 *scr, n_ctx_chunks, n_chunks):
    c = HGRN_CHUNK
    n_chain = 2 * HGRN_HEADS_PER_STEP
    g_scrs, o_scrs = scr[:n_chain], scr[n_chain:]
    sub = lax.broadcasted_iota(jnp.int32, (8, 128), 0)

    def level_ref(g_scr, d, lvl):
        m = c >> lvl
        blk = 2 * m
        def row(r):
            return g_scr[r:r + 1, :]
        idx = [b * blk + m - 1 + d for b in range(c // blk)]
        if blk >= 8:
            return jnp.concatenate([jnp.broadcast_to(row(r), (blk, 128)) for r in idx], axis=0)
        per = 8 // blk
        tiles = []
        for j in range(c // 8):
            tile = jnp.broadcast_to(row(idx[j * per + per - 1]), (8, 128))
            for u in range(per - 2, -1, -1):
                tile = jnp.where(sub < (u + 1) * blk, jnp.broadcast_to(row(idx[j * per + u]), (8, 128)), tile)
            tiles.append(tile)
        return jnp.concatenate(tiles, axis=0)

    chains = [(hh, d) for hh in range(HGRN_HEADS_PER_STEP) for d in range(2)]

    def chunk_rows(n):
        n_lat_chunks = n_chunks - n_ctx_chunks
        cf = jnp.where(n < n_ctx_chunks, n_lat_chunks + n, n - n_ctx_chunks)
        cb = n_chunks - 1 - n
        return [pl.ds(pl.multiple_of(cf * c, c), c), pl.ds(pl.multiple_of(cb * c, c), c)]

    def gate_decay_parts(n):
        rows_d = chunk_rows(n)
        out = []
        for hh, d in chains:
            cols = slice(hh * HEAD_DIM, (hh + 1) * HEAD_DIM)
            f = (ff_ref if d == 0 else fb_ref)[0, rows_d[d], cols].astype(F32)
            a = llb_ref[d:d + 1, cols] - f
            lf = (jnp.minimum(f, 0.0) - jnp.log(1.0 + jnp.exp(-jnp.abs(f)))
                  + jnp.maximum(a, 0.0) + jnp.log(1.0 + jnp.exp(-jnp.abs(a))))
            hi = lf.astype(BF16)
            r1 = lf - hi.astype(F32)
            mid = r1.astype(BF16)
            lo = (r1 - mid.astype(F32)).astype(BF16)
            cum = cum_ref[d]
            out.append((_dot(cum, hi), _dot(cum, mid), _dot(cum, lo)))
        return out

    def gate_qk(n):
        rows_d = chunk_rows(n)
        out = []
        for hh, d in chains:
            cols = slice(hh * HEAD_DIM, (hh + 1) * HEAD_DIM)
            qv = q_ref[0, rows_d[d], cols].astype(F32)
            qb = (_silu(qv) * (HEAD_DIM ** -0.5)).astype(BF16)
            f = (ff_ref if d == 0 else fb_ref)[0, rows_d[d], cols].astype(F32)
            kb = (oml_ref[d:d + 1, cols] / (1.0 + jnp.exp(f))).astype(BF16)
            out.append((qb, kb))
        return out

    def gate_join(qk, parts):
        return [(qb, kb, p0 + p1 + p2) for (qb, kb), (p0, p1, p2) in zip(qk, parts)]

    def body(n, carry):
        states, cur = carry
        n_next = jnp.minimum(n + 1, n_chunks - 1)
        rows_d = chunk_rows(n)
        vb, gtot, o, scores, o_intra, st_upd = {}, {}, {}, {}, {}, {}
        new = [None] * n_chain
        for ci, (hh, d) in enumerate(chains):
            g_scrs[ci][...] = cur[ci][2]

        def scores_of(group):
            for ci in group:
                hh, d = chains[ci]
                qb, kb, g = cur[ci]
                vb[ci] = i_ref[0, rows_d[d], hh * HEAD_DIM:(hh + 1) * HEAD_DIM]
                end = (c - 1) * (1 - d)
                gtot[ci] = g_scrs[ci][end:end + 1, :]
                o[ci] = _dot_nt(qb * jnp.exp(g).astype(BF16), states[ci].astype(BF16))
                scores[ci] = _dot_nt(qb, kb).astype(BF16) * mask_ref[d, 0]
            for lvl in range(1, HGRN_LEVELS + 1):
                for ci in group:
                    d = chains[ci][1]
                    qb, kb, g = cur[ci]
                    e = jnp.exp(-jnp.abs(g - level_ref(g_scrs[ci], d, lvl))).astype(BF16)
                    s_l = _dot_nt(qb * e, kb * e)
                    scores[ci] = scores[ci] + s_l.astype(BF16) * mask_ref[d, lvl]

        def issue_last(group):
            for ci in group:
                qb, kb, g = cur[ci]
                o_intra[ci] = _dot(scores[ci], vb[ci])
                kend = kb * jnp.exp(gtot[ci] - g).astype(BF16)
                vt = vb[ci].astype(F32).T.astype(BF16)
                st_upd[ci] = _dot(vt, kend)

        def finish(group):
            for ci in group:
                o_scrs[ci][rows_d[chains[ci][1]], :] = o[ci] + o_intra[ci]
                new[ci] = states[ci] * jnp.exp(gtot[ci]) + st_upd[ci]

        every = range(n_chain)
        scores_of(every)
        nxt_parts = gate_decay_parts(n_next)
        issue_last(every)
        nxt_qk = gate_qk(n_next)
        finish(every)
        return tuple(new), gate_join(nxt_qk, nxt_parts)

    zero = jnp.zeros((HEAD_DIM, HEAD_DIM), F32)
    lax.fori_loop(0, n_chunks, body, ((zero,) * n_chain, gate_join(gate_qk(0), gate_decay_parts(0))), unroll=8)

    n_row_tiles = (n_chunks * c) // ROW_TILE

    def readout(i, carry):
        rows = pl.ds(pl.multiple_of(i * ROW_TILE, ROW_TILE), ROW_TILE)
        for hh in range(HGRN_HEADS_PER_STEP):
            cols = slice(hh * HEAD_DIM, (hh + 1) * HEAD_DIM)
            o = o_scrs[2 * hh][rows, :] + o_scrs[2 * hh + 1][rows, :]
            o = o * lax.rsqrt(jnp.mean(o * o, axis=-1, keepdims=True) + EPS) * on_ref[:, cols]
            y_ref[0, rows, cols] = (o * _silu(gate_ref[0, rows, cols].astype(F32))).astype(y_ref.dtype)
        return carry

    lax.fori_loop(0, n_row_tiles, readout, 0)


def _hgrn(p3, one_minus_lb, log_lb, o_norm, ctx_len):
    bsz, t, _ = p3.shape
    cum, masks = _hgrn_consts()
    bw = HGRN_HEADS_PER_STEP * HEAD_DIM
    n_chain = 2 * HGRN_HEADS_PER_STEP

    def col(off):
        return pl.BlockSpec((1, t, bw), lambda b, h: (b, 0, off // bw + h))

    return pl.pallas_call(
        functools.partial(_hgrn_kernel, n_ctx_chunks=ctx_len // HGRN_CHUNK, n_chunks=t // HGRN_CHUNK),
        grid=(bsz, GROUP_WIDTH // bw),
        in_specs=[col(B_Q), col(B_I), col(B_FF), col(B_FB), col(B_G),
                  pl.BlockSpec((2, bw), lambda b, h: (0, h)),
                  pl.BlockSpec((2, bw), lambda b, h: (0, h)),
                  pl.BlockSpec((1, bw), lambda b, h: (0, h)),
                  pl.BlockSpec(cum.shape, lambda b, h: (0, 0, 0)),
                  pl.BlockSpec(masks.shape, lambda b, h: (0, 0, 0, 0))],
        out_specs=pl.BlockSpec((1, t, bw), lambda b, h: (b, 0, h)),
        out_shape=jax.ShapeDtypeStruct((bsz, t, GROUP_WIDTH), BF16),
        scratch_shapes=([pltpu.VMEM((HGRN_CHUNK, HEAD_DIM), F32)] * n_chain
                        + [pltpu.VMEM((t, HEAD_DIM), F32)] * n_chain),
        compiler_params=_cparams(("parallel", "parallel")),
        name="hgrn2",
    )(p3, p3, p3, p3, p3, one_minus_lb, log_lb, o_norm.reshape(1, -1), cum, masks)


def _conv_kernel(u_ref, up_ref, un_ref, gl_ref, glp_ref, gln_ref, gate_ref, w_ref, cb_ref, lw_ref, lb_ref, y_ref,
                 xs_scr, acc_scr, *, n_lat):
    t = pl.program_id(1)
    h = CONV_HALO
    prev_ok = jnp.logical_and(t >= 1, t < n_lat).astype(F32)
    next_ok = (t < n_lat - 1).astype(F32)
    n_lane_blocks = GROUP_WIDTH // 128
    x_prev = up_ref[0].astype(F32) * jax.nn.sigmoid(glp_ref[0].astype(F32)) * prev_ok
    x_main = u_ref[0].astype(F32) * jax.nn.sigmoid(gl_ref[0].astype(F32))
    x_next = un_ref[0].astype(F32) * jax.nn.sigmoid(gln_ref[0].astype(F32)) * next_ok
    for cb in range(n_lane_blocks):
        sl = slice(cb * 128, (cb + 1) * 128)
        xs_scr[cb, 0:h, :] = x_prev[:, sl]
        xs_scr[cb, h:h + ROW_TILE, :] = x_main[:, sl]
        xs_scr[cb, h + ROW_TILE:, :] = x_next[:, sl]
    rc = 64
    base = h - CONV_WIDTH // 2

    def lane_block(cb, carry):
        for r in range(ROW_TILE // rc):
            acc = jnp.zeros((rc, 128), F32)
            for tap in range(CONV_WIDTH):
                lo = base + r * rc + tap
                acc = acc + xs_scr[cb, lo:lo + rc, :] * w_ref[cb, tap:tap + 1, :]
            acc_scr[cb, r * rc:(r + 1) * rc, :] = acc
        return carry

    lax.fori_loop(0, n_lane_blocks, lane_block, 0)
    y = jnp.concatenate([acc_scr[cb] for cb in range(n_lane_blocks)], axis=1) + cb_ref[...]
    mu = jnp.mean(y, axis=-1, keepdims=True)
    yc = y - mu
    var = jnp.mean(yc * yc, axis=-1, keepdims=True)
    z = yc * lax.rsqrt(var + EPS) * lw_ref[...] + lb_ref[...]
    y_ref[0] = (_silu(z) * _silu(gate_ref[0].astype(F32))).astype(y_ref.dtype)


def _conv(p3, conv_w, conv_b, ln_w, ln_b, ctx_out):
    bsz, t, _ = p3.shape
    n_lat = t // ROW_TILE - 1
    nt = n_lat + (1 if ctx_out else 0)
    hpt = ROW_TILE // CONV_HALO
    n_halo = t // CONV_HALO
    n_lane_blocks = GROUP_WIDTH // 128
    wpad = jnp.pad(conv_w, ((0, 32 - CONV_WIDTH), (0, 0))).reshape(32, n_lane_blocks, 128).transpose(1, 0, 2)

    def main(off):
        return pl.BlockSpec((1, ROW_TILE, GROUP_WIDTH), lambda b, i: (b, i, off // GROUP_WIDTH))

    def prev(off):
        return pl.BlockSpec((1, CONV_HALO, GROUP_WIDTH),
                            lambda b, i: (b, jnp.maximum(i * hpt - 1, 0), off // GROUP_WIDTH))

    def nxt(off):
        return pl.BlockSpec((1, CONV_HALO, GROUP_WIDTH),
                            lambda b, i: (b, jnp.minimum((i + 1) * hpt, n_halo - 1), off // GROUP_WIDTH))

    vec = pl.BlockSpec((1, GROUP_WIDTH), lambda b, i: (0, 0))
    return pl.pallas_call(
        functools.partial(_conv_kernel, n_lat=n_lat),
        grid=(bsz, nt),
        in_specs=[main(C_U), prev(C_U), nxt(C_U), main(C_GLU), prev(C_GLU), nxt(C_GLU), main(C_G),
                  pl.BlockSpec((n_lane_blocks, 32, 128), lambda b, i: (0, 0, 0)), vec, vec, vec],
        out_specs=pl.BlockSpec((1, ROW_TILE, GROUP_WIDTH), lambda b, i: (b, i, 0)),
        out_shape=jax.ShapeDtypeStruct((bsz, nt * ROW_TILE, GROUP_WIDTH), BF16),
        scratch_shapes=[pltpu.VMEM((n_lane_blocks, ROW_TILE + 2 * CONV_HALO, 128), F32),
                        pltpu.VMEM((n_lane_blocks, ROW_TILE, 128), F32)],
        compiler_params=_cparams(("parallel", "parallel")),
        name="conformer_conv",
    )(p3, p3, p3, p3, p3, p3, p3, wpad, conv_b.reshape(1, -1), ln_w.reshape(1, -1), ln_b.reshape(1, -1))


def _out_kernel(ya_ref, yb_ref, yc_ref, yd_ref, w_ref, x_ref, gate_ref, o_ref, wb_scr, *, ctx_rows):
    b = pl.program_id(1)
    gw = GROUP_WIDTH

    @pl.when(jnp.logical_and(b == 0, pl.program_id(2) == 0))
    def _():
        wb_scr[...] = w_ref[0].astype(BF16)

    acc = _dot(ya_ref[0], wb_scr[0:gw, :])
    acc = acc + _dot(yb_ref[0], wb_scr[gw:2 * gw, :])
    acc = acc + _dot(yc_ref[0], wb_scr[2 * gw:3 * gw, :])
    acc = acc + _dot(yd_ref[0], wb_scr[3 * gw:4 * gw, :])

    row = 2 if ctx_rows else b
    o_ref[0] = x_ref[0] + gate_ref[pl.ds(row, 1), :] * acc


def _out_proj(ys, w_out, layer, x, mod, ctx_rows):
    bsz, rows, d = x.shape
    tm = ROW_TILE if ctx_rows else _pick_tile(rows, (1024, 512, 256))
    y_off = (ys[0].shape[1] - rows) // tm if ctx_rows else 0
    tn = 1024 if ctx_rows else 512
    y_spec = pl.BlockSpec((1, tm, GROUP_WIDTH), lambda j, b, i: (b, i + y_off, 0))
    return pl.pallas_call(
        functools.partial(_out_kernel, ctx_rows=ctx_rows),
        grid=(d // tn, bsz, rows // tm),
        in_specs=[y_spec, y_spec, y_spec, y_spec,
                  pl.BlockSpec((1, 4 * GROUP_WIDTH, tn), lambda j, b, i: (layer, 0, j)),
                  pl.BlockSpec((1, tm, tn), lambda j, b, i: (b, i, j)),
                  pl.BlockSpec((8, tn), lambda j, b, i: (0, 2 * d // tn + j))],
        out_specs=pl.BlockSpec((1, tm, tn), lambda j, b, i: (b, i, j)),
        out_shape=jax.ShapeDtypeStruct((bsz, rows, d), F32),
        scratch_shapes=[pltpu.VMEM((4 * GROUP_WIDTH, tn), BF16)],
        compiler_params=_cparams(("parallel", "arbitrary", "arbitrary")),
        name="out_proj",
    )(*ys, w_out, x, mod)


D_PACK_ROWS = 64


def _pack_d_kernel(w_ref, o_ref, *, n_src_blocks):
    @pl.when(pl.program_id(0) < n_src_blocks)
    def _():
        o_ref[...] = w_ref[0].astype(BF16)

    @pl.when(pl.program_id(0) >= n_src_blocks)
    def _():
        o_ref[...] = jnp.zeros(o_ref.shape, o_ref.dtype)


def _group_d_weight(w_in_t, layer):
    k = w_in_t.shape[2]
    r = D_PACK_ROWS
    base = ABC_WIDTH // r
    src_cq, src_ckv, src_kr, src_g = base, base + 768 // r, base + 1280 // r, base + 1344 // r
    n_src_blocks = (D_KR + MLA_ROPE_DIM) // r

    def src(j):
        return jnp.where(j < D_CKV // r, src_g + j,
                         jnp.where(j < D_CQ // r, src_ckv + j - D_CKV // r,
                                   jnp.where(j < D_KR // r, src_cq + j - D_CQ // r, src_kr)))

    return pl.pallas_call(
        functools.partial(_pack_d_kernel, n_src_blocks=n_src_blocks),
        grid=(D_WIDTH // r,),
        in_specs=[pl.BlockSpec((1, r, k), lambda j: (layer, src(j), 0))],
        out_specs=pl.BlockSpec((r, k), lambda j: (j, 0)),
        out_shape=jax.ShapeDtypeStruct((D_WIDTH, k), BF16),
        compiler_params=_cparams(("parallel",)),
        name="pack_group_d",
    )(w_in_t)


def kernel(x, c, ctx, c_ctx, w_mod, b_mod, norm_w, w_in, w_out, att_q_norm, att_k_norm, hgrn_lb_logits, hgrn_o_norm,
           conv_w, conv_b, conv_ln_w, conv_ln_b, mla_q_norm, mla_kv_norm, mla_w_uq, mla_w_ukv, mla_qk_q_norm,
           mla_qk_k_norm):
    bsz, seq, d = x.shape
    ctx_len = ctx.shape[1]
    depth = w_mod.shape[0]
    t = ctx_len + seq
    assert ctx_len == ROW_TILE and seq % ROW_TILE == 0 and bsz <= 2
    n_tiles = t // ROW_TILE

    lb_all = jnp.cumsum(jax.nn.softmax(hgrn_lb_logits.astype(F32), axis=0), axis=0)
    lb_all = lb_all - lb_all[0:1]

    c8 = jnp.zeros((8, d), F32).at[0:bsz].set(c).at[2].set(c_ctx)
    mod_all = _modulation(c8, w_mod, b_mod)

    gqa_tabs = _rope_tables(seq, ctx_len, HEAD_DIM)
    mla_tabs = _rope_tables(seq, ctx_len, MLA_ROPE_DIM)

    w_in_t = jnp.swapaxes(w_in, 1, 2)

    xc, xl = ctx, x
    for l in range(depth):
        ctx_out = l < depth - 1
        mod = mod_all[l]
        h = _norm_modulate(xc, xl, mod, norm_w[l], n_tiles)
        h2 = h.reshape(bsz * t, d)
        p3 = _in_proj_abc(h2, w_in_t, l).reshape(bsz, t, ABC_WIDTH)
        pd3 = _matmul_nt(h2, _group_d_weight(w_in_t, l), BF16).reshape(bsz, t, D_WIDTH)

        qa, ka, va = _gqa_prep(p3, gqa_tabs, att_q_norm[l], att_k_norm[l])
        ya = _attention(qa, ka, va, p3, A_G, N_HEADS, HEAD_DIM, N_HEADS // ATT_KV_HEADS, ctx_out)

        yb = _hgrn(p3, 1.0 - lb_all[l], jnp.log(lb_all[l]), hgrn_o_norm[l], ctx_len)

        yc = _conv(p3, conv_w[l], conv_b[l], conv_ln_w[l], conv_ln_b[l], ctx_out)

        wq = jnp.pad(mla_w_uq[l].reshape(MLA_Q_RANK, N_HEADS, MLA_QK_DIM),
                     ((0, 0), (0, 0), (0, MLA_PAD_DIM - MLA_QK_DIM))).reshape(MLA_Q_RANK, -1).astype(BF16)
        wkv = mla_w_ukv[l].reshape(MLA_KV_RANK, N_HEADS, 2 * HEAD_DIM)
        wk = wkv[:, :, :HEAD_DIM].reshape(MLA_KV_RANK, -1).astype(BF16)
        wv = wkv[:, :, HEAD_DIM:].reshape(MLA_KV_RANK, -1).astype(BF16)
        qn = jnp.pad(mla_qk_q_norm[l], (0, MLA_PAD_DIM - MLA_QK_DIM))
        kn = jnp.pad(mla_qk_k_norm[l], (0, MLA_PAD_DIM - MLA_QK_DIM))
        qd, kd, vd = _mla_up(pd3, mla_tabs, mla_q_norm[l], mla_kv_norm[l], wq, wk, wv, qn, kn)
        yd = _attention(qd, kd, vd, pd3, D_G, 4, MLA_PAD_DIM, 1, ctx_out)

        ys = (ya, yb, yc, yd)
        xl_new = _out_proj(ys, w_out, l, xl, mod, False)
        if ctx_out:
            xc = _out_proj(ys, w_out, l, xc, mod, True)
        xl = xl_new
    return xl
```
